```python
import math
import jax, jax.numpy as jnp
from jax import lax
import numpy as np

D_MODEL = 1024
BATCH = 8
SEQ = 8192
DEPTH = 2

GRID_W = 64
CTX_LEN = 256
N_MIXERS = 2
N_HG_LAYERS = (DEPTH + N_MIXERS - 1) // N_MIXERS
N_DA_LAYERS = DEPTH // N_MIXERS
HG_HEADS = D_MODEL // 128
HG_DK = 128
HG_DV = D_MODEL // HG_HEADS
HG_CHUNK = 64
DA_HEADS = 8
DA_DH = D_MODEL // DA_HEADS // 2
DA_DV = 2 * DA_DH
Q_BLOCK = 128
ROPE_BASE = 10000.0
N_EXPERTS = 32
TOP_K = 4
D_EXPERT = D_MODEL
SWIGLU_LIMIT = 7.0
SWIGLU_ALPHA = 1.702
MOE_BLOCK = 128
DN_ALPHA = (2 * DEPTH) ** 0.25
DN_BETA = (8 * DEPTH) ** -0.25
LN_EPS = 1e-5
RMS_EPS = 1e-6

kernel_name = 'hybrid_hgrn2_diffattn_moe_dit'


def layer_norm(x, g, b):
    xf = x.astype(jnp.float32)
    mu = jnp.mean(xf, axis=-1, keepdims=True)
    var = jnp.mean(jnp.square(xf - mu), axis=-1, keepdims=True)
    y = (xf - mu) * lax.rsqrt(var + LN_EPS) * g.astype(jnp.float32) + b.astype(jnp.float32)
    return y.astype(x.dtype)


def rms_norm(x, g):
    xf = x.astype(jnp.float32)
    return xf * lax.rsqrt(jnp.mean(xf * xf, axis=-1, keepdims=True) + RMS_EPS) * g.astype(jnp.float32)


def rope_2d(t):
    L = t.shape[1]
    rows_n = L // GRID_W
    row = jnp.repeat(jnp.arange(rows_n), GRID_W)
    col = jnp.tile(jnp.arange(GRID_W), rows_n)
    half = t.shape[-1] // 2
    n_freq = half // 2
    inv = ROPE_BASE ** (-jnp.arange(n_freq, dtype=jnp.float32) / n_freq)

    def rot(u, pos):
        ang = pos.astype(jnp.float32)[:, None] * inv[None, :]
        cos = jnp.cos(ang)[None, :, None, :]
        sin = jnp.sin(ang)[None, :, None, :]
        u1, u2 = jnp.split(u, 2, axis=-1)
        return jnp.concatenate([u1 * cos - u2 * sin, u1 * sin + u2 * cos], axis=-1)

    tf = t.astype(jnp.float32)
    out = jnp.concatenate([rot(tf[..., :half], row), rot(tf[..., half:], col)], axis=-1)
    return out.astype(t.dtype)


def gla_chunkwise(q, k, v, logf, s0):
    bsz, L, H, _ = q.shape
    DV = v.shape[-1]
    n = L // HG_CHUNK

    def chunks(t):
        return t.reshape(bsz, n, HG_CHUNK, H, t.shape[-1]).transpose(1, 0, 3, 2, 4)

    qc, kc, vc, gc = chunks(q), chunks(k), chunks(v), chunks(logf)
    b = jnp.cumsum(gc, axis=3)
    b_end = b[:, :, :, -1:, :]
    q_dec = qc * jnp.exp(b)
    k_inv = kc * jnp.exp(-b)
    k_end = kc * jnp.exp(b_end - b)
    tri = jnp.tril(jnp.ones((HG_CHUNK, HG_CHUNK), dtype=bool))
    a = jnp.where(tri, jnp.einsum('nbhtk,nbhsk->nbhts', q_dec, k_inv), 0.0)
    o_intra = jnp.einsum('nbhts,nbhsv->nbhtv', a, vc)

    def step(state, xs):
        qn, kn, vn, dn = xs
        o = jnp.einsum('bhtk,bhkv->bhtv', qn, state)
        state = state * jnp.exp(dn)[..., None] + jnp.einsum('bhsk,bhsv->bhkv', kn, vn)
        return state, o

    s_final, o_inter = lax.scan(step, s0, (q_dec, k_end, vc, b_end[:, :, :, 0, :]))
    o = (o_intra + o_inter).transpose(1, 0, 3, 2, 4).reshape(bsz, L, H, DV)
    return o, s_final


def hgrn2_mixer(h_lat, h_ctx, w_in, lb_tab, norm_g, w_out, layer, need_ctx):
    lb = jnp.cumsum(jax.nn.softmax(lb_tab.astype(jnp.float32), axis=1), axis=1)[:, layer]
    lb = lb.reshape(2, HG_HEADS, HG_DK)

    def heads(t):
        return t.reshape(t.shape[0], t.shape[1], HG_HEADS, -1).astype(jnp.float32)

    def project(h):
        q, i, g, zf, zb = jnp.split(h @ w_in, 5, axis=-1)
        return heads(q), heads(i), g, (heads(zf), heads(zb))

    def scan_dir(q, i, z, lbd, s0, reverse):
        if reverse:
            q, i, z = jnp.flip(q, 1), jnp.flip(i, 1), jnp.flip(z, 1)
        f = lbd + (1.0 - lbd) * jax.nn.sigmoid(z)
        o, s = gla_chunkwise(q, 1.0 - f, i, jnp.log(f), s0)
        if reverse:
            o = jnp.flip(o, 1)
        return o, s

    def readout(o, g):
        o = rms_norm(o, norm_g.reshape(HG_HEADS, HG_DV)) * jax.nn.silu(heads(g))
        return o.reshape(o.shape[0], o.shape[1], D_MODEL).astype(w_out.dtype) @ w_out

    qc, ic, gc, zc = project(h_ctx)
    ql, il, gl, zl = project(h_lat)
    s0 = jnp.zeros((h_ctx.shape[0], HG_HEADS, HG_DK, HG_DV), jnp.float32)
    oc_f, sc_f = scan_dir(qc, ic, zc[0], lb[0], s0, False)
    oc_b, sc_b = scan_dir(qc, ic, zc[1], lb[1], s0, True)
    ol_f, _ = scan_dir(ql, il, zl[0], lb[0], sc_f, False)
    ol_b, _ = scan_dir(ql, il, zl[1], lb[1], sc_b, True)
    out_lat = readout(ol_f + ol_b, gl)
    out_ctx = readout(oc_f + oc_b, gc) if need_ctx else None
    return out_lat, out_ctx


def diff_attn_mixer(h_lat, h_ctx, w_in, lam_p, norm_g, w_out, layer, need_ctx):
    lam_init = 0.8 - 0.6 * math.exp(-0.3 * layer)
    lp = lam_p.astype(jnp.float32)
    lam = jnp.exp(jnp.sum(lp[0] * lp[1])) - jnp.exp(jnp.sum(lp[2] * lp[3])) + lam_init
    w_q, w_kv = w_in[:, :D_MODEL], w_in[:, D_MODEL:]

    def proj_q(h):
        return (h @ w_q).reshape(h.shape[0], h.shape[1], 2 * DA_HEADS, DA_DH)

    def proj_kv(h):
        k, v = jnp.split(h @ w_kv, 2, axis=-1)
        return (k.reshape(h.shape[0], h.shape[1], 2 * DA_HEADS, DA_DH),
                v.reshape(h.shape[0], h.shape[1], DA_HEADS, DA_DV))

    def maps(t):
        return t.reshape(t.shape[0], t.shape[1], DA_HEADS, 2, DA_DH)

    def attend(qb, k, v):
        s = jnp.einsum('bqhcd,bkhcd->bhcqk', qb, k).astype(jnp.float32) * (DA_DH ** -0.5)
        p = jax.nn.softmax(s, axis=-1)
        w = p[:, :, 0] - lam * p[:, :, 1]
        return jnp.einsum('bhqk,bkhv->bqhv', w.astype(v.dtype), v)

    def readout(o):
        o = rms_norm(o, norm_g) * (1.0 - lam_init)
        return o.reshape(o.shape[0], o.shape[1], D_MODEL).astype(w_out.dtype) @ w_out

    ql = rope_2d(proj_q(h_lat))
    kl, vl = proj_kv(h_lat)
    kl = rope_2d(kl)
    kc, vc = proj_kv(h_ctx)
    k_all = maps(jnp.concatenate([kl, kc], axis=1))
    v_all = jnp.concatenate([vl, vc], axis=1)
    bsz, L = ql.shape[0], ql.shape[1]
    nb = L // Q_BLOCK
    q_blocks = maps(ql).reshape(bsz, nb, Q_BLOCK, DA_HEADS, 2, DA_DH).transpose(1, 0, 2, 3, 4, 5)
    o_lat = lax.map(lambda qb: attend(qb, k_all, v_all), q_blocks)
    o_lat = o_lat.transpose(1, 0, 2, 3, 4).reshape(bsz, L, DA_HEADS, DA_DV)
    out_lat = readout(o_lat)
    out_ctx = readout(attend(maps(proj_q(h_ctx)), maps(kc), vc)) if need_ctx else None
    return out_lat, out_ctx


def moe_ffn(h, w_r, b_r, w_gu, b_gu, w_dn, b_dn):
    T, D = h.shape
    logits = (h @ w_r + b_r).astype(jnp.float32)
    top_v, top_i = lax.top_k(logits, TOP_K)
    gates = jax.nn.softmax(top_v, axis=-1)
    A = T * TOP_K
    e_flat = top_i.reshape(-1)
    order = jnp.argsort(e_flat, stable=True)
    e_sorted = e_flat[order]
    tok_sorted = order // TOP_K
    g_sorted = gates.reshape(-1)[order].astype(h.dtype)
    counts = jnp.bincount(e_flat, length=N_EXPERTS)
    padded = (counts + MOE_BLOCK - 1) // MOE_BLOCK * MOE_BLOCK
    start = jnp.cumsum(counts) - counts
    pend = jnp.cumsum(padded)
    pstart = pend - padded
    dest = pstart[e_sorted] + (jnp.arange(A) - start[e_sorted])
    n_blocks = -(-A // MOE_BLOCK) + N_EXPERTS
    buf = jnp.zeros((n_blocks * MOE_BLOCK, D), h.dtype).at[dest].set(h[tok_sorted])
    block_expert = jnp.minimum(jnp.searchsorted(pend, jnp.arange(n_blocks) * MOE_BLOCK, side='right'), N_EXPERTS - 1)

    def expert_block(args):
        xb, e = args
        gate, up = jnp.split(xb @ w_gu[e] + b_gu[e], 2, axis=-1)
        gate = jnp.minimum(gate, SWIGLU_LIMIT)
        up = jnp.clip(up, -SWIGLU_LIMIT, SWIGLU_LIMIT)
        act = (up + 1.0) * gate * jax.nn.sigmoid(SWIGLU_ALPHA * gate)
        return act @ w_dn[e] + b_dn[e]

    out = lax.map(expert_block, (buf.reshape(n_blocks, MOE_BLOCK, D), block_expert)).reshape(-1, D)
    return jax.ops.segment_sum(out[dest] * g_sorted[:, None], tok_sorted, num_segments=T)


def setup_inputs(seed: int = 0) -> dict:
    key = jax.random.key(seed)
    ks = jax.random.split(key, 22)
    D = D_MODEL

    def nrm(k, shape, scale):
        return jax.random.normal(k, shape, jnp.float32) * scale

    return {
        'x': nrm(ks[0], (BATCH, SEQ, D), 1.0),
        'c': nrm(ks[1], (BATCH, D), 1.0),
        'ctx': nrm(ks[2], (BATCH, CTX_LEN, D), 1.0),
        'c_ctx': nrm(ks[3], (D,), 1.0),
        'w_ada': nrm(ks[4], (DEPTH, D, 6 * D), 0.5 * D ** -0.5),
        'b_ada': nrm(ks[5], (DEPTH, 6 * D), 0.02),
        'ln_g': 1.0 + nrm(ks[6], (DEPTH, 2, D), 0.02),
        'ln_b': nrm(ks[7], (DEPTH, 2, D), 0.02),
        'hg_w_in': nrm(ks[8], (N_HG_LAYERS, D, 5 * D), D ** -0.5),
        'hg_lb': nrm(ks[9], (2, DEPTH + 1, HG_HEADS * HG_DK), 0.1),
        'hg_norm_g': 1.0 + nrm(ks[10], (N_HG_LAYERS, D), 0.02),
        'hg_w_out': nrm(ks[11], (N_HG_LAYERS, D, D), DN_BETA * D ** -0.5),
        'da_w_in': nrm(ks[12], (N_DA_LAYERS, D, 3 * D), D ** -0.5),
        'da_lam': nrm(ks[13], (N_DA_LAYERS, 4, DA_DH), 0.1),
        'da_norm_g': 1.0 + nrm(ks[14], (N_DA_LAYERS, DA_DV), 0.02),
        'da_w_out': nrm(ks[15], (N_DA_LAYERS, D, D), DN_BETA * D ** -0.5),
        'moe_w_router': nrm(ks[16], (DEPTH, D, N_EXPERTS), D ** -0.5),
        'moe_b_router': nrm(ks[17], (DEPTH, N_EXPERTS), 0.01),
        'moe_w_gu': nrm(ks[18], (DEPTH, N_EXPERTS, D, 2 * D_EXPERT), D ** -0.5),
        'moe_b_gu': nrm(ks[19], (DEPTH, N_EXPERTS, 2 * D_EXPERT), 0.01),
        'moe_w_dn': nrm(ks[20], (DEPTH, N_EXPERTS, D_EXPERT, D), DN_BETA * D_EXPERT ** -0.5),
        'moe_b_dn': nrm(ks[21], (DEPTH, N_EXPERTS, D), 0.01),
    }


def reference(x, c, ctx, c_ctx, w_ada, b_ada, ln_g, ln_b, hg_w_in, hg_lb, hg_norm_g, hg_w_out,
              da_w_in, da_lam, da_norm_g, da_w_out, moe_w_router, moe_b_router, moe_w_gu, moe_b_gu,
              moe_w_dn, moe_b_dn):
    bsz, L, D = x.shape
    xl, xc = x, ctx
    silu_c = jax.nn.silu(c)
    silu_cc = jax.nn.silu(c_ctx)
    for l in range(DEPTH):
        last = l == DEPTH - 1
        j = l // N_MIXERS
        mod_l = (silu_c @ w_ada[l] + b_ada[l])[:, None, :]
        mod_c = (silu_cc @ w_ada[l] + b_ada[l])[None, None, :]
        sh1, sc1, g1, sh2, sc2, g2 = jnp.split(mod_l, 6, axis=-1)
        csh1, csc1, cg1, csh2, csc2, cg2 = jnp.split(mod_c, 6, axis=-1)
        hl = xl * (1.0 + sc1) + sh1
        hc = xc * (1.0 + csc1) + csh1
        if l % N_MIXERS == 0:
            ml, mc = hgrn2_mixer(hl, hc, hg_w_in[j], hg_lb, hg_norm_g[j], hg_w_out[j], l, not last)
        else:
            ml, mc = diff_attn_mixer(hl, hc, da_w_in[j], da_lam[j], da_norm_g[j], da_w_out[j], l, not last)
        xl = layer_norm(DN_ALPHA * xl + g1 * ml, ln_g[l, 0], ln_b[l, 0])
        hl2 = (xl * (1.0 + sc2) + sh2).reshape(-1, D)
        if last:
            f = moe_ffn(hl2, moe_w_router[l], moe_b_router[l], moe_w_gu[l], moe_b_gu[l], moe_w_dn[l], moe_b_dn[l])
            xl = layer_norm(DN_ALPHA * xl + g2 * f.reshape(bsz, L, D), ln_g[l, 1], ln_b[l, 1])
        else:
            xc = layer_norm(DN_ALPHA * xc + cg1 * mc, ln_g[l, 0], ln_b[l, 0])
            hc2 = (xc * (1.0 + csc2) + csh2).reshape(-1, D)
            f = moe_ffn(jnp.concatenate([hl2, hc2], axis=0), moe_w_router[l], moe_b_router[l],
                        moe_w_gu[l], moe_b_gu[l], moe_w_dn[l], moe_b_dn[l])
            fl = f[:bsz * L].reshape(bsz, L, D)
            fc = f[bsz * L:].reshape(xc.shape)
            xl = layer_norm(DN_ALPHA * xl + g2 * fl, ln_g[l, 1], ln_b[l, 1])
            xc = layer_norm(DN_ALPHA * xc + cg2 * fc, ln_g[l, 1], ln_b[l, 1])
    return xl
```

```python
import functools
import math

import jax
import jax.numpy as jnp
from jax import lax
from jax.experimental import pallas as pl
from jax.experimental.pallas import tpu as pltpu

F32 = jnp.float32
BF16 = jnp.bfloat16
HIGHEST = lax.Precision.HIGHEST

LANES = 128
HG_CHUNK = 64
GRID_W = 64
ROPE_BASE = 10000.0
DA_HEADS = 8
N_EXPERTS = 32
TOP_K = 4
SWIGLU_LIMIT = 7.0
SWIGLU_ALPHA = 1.702
LN_EPS = 1e-5
RMS_EPS = 1e-6
VMEM_LIMIT = 56 * 1024 * 1024


def _cparams(sem):
    return pltpu.CompilerParams(dimension_semantics=sem, vmem_limit_bytes=VMEM_LIMIT)


def _ada_kernel(c_ref, w_ref, b_ref, o_ref):
    c = c_ref[...]
    s = c * jax.nn.sigmoid(c)
    o_ref[0] = jnp.dot(s, w_ref[0], precision=HIGHEST, preferred_element_type=F32) + b_ref[0]


def _ada(c_pad, w_ada, b_ada):
    depth, d, n = w_ada.shape
    rows = c_pad.shape[0]
    tn = 1536
    return pl.pallas_call(
        _ada_kernel,
        grid=(depth, n // tn),
        in_specs=[pl.BlockSpec((rows, d), lambda l, j: (0, 0)),
                  pl.BlockSpec((1, d, tn), lambda l, j: (l, 0, j)),
                  pl.BlockSpec((1, 1, tn), lambda l, j: (l, 0, j))],
        out_specs=pl.BlockSpec((1, rows, tn), lambda l, j: (l, 0, j)),
        out_shape=jax.ShapeDtypeStruct((depth, rows, n), F32),
        compiler_params=_cparams(("arbitrary", "arbitrary")),
        name="ada_mod",
    )(c_pad, w_ada, b_ada.reshape(depth, 1, n))


def _rope(piece, cos, sin, lo_mask):
    rot = jnp.where(lo_mask, pltpu.roll(piece, LANES - 16, 1), pltpu.roll(piece, 16, 1))
    return piece * cos + rot * sin


def _proj_kernel(x_ref, sc_ref, sh_ref, w_ref, *rest, n_rope, cw):
    if n_rope:
        cq_ref, sq_ref, ck_ref, sk_ref, o_ref = rest
    else:
        (o_ref,) = rest
    n = w_ref.shape[1]
    h = (x_ref[0] * (1.0 + sc_ref[0]) + sh_ref[0]).astype(BF16)
    if n_rope:
        lane = lax.broadcasted_iota(jnp.int32, (x_ref.shape[1], LANES), 1)
        lo_mask = (lane % 32) < 16
    per = cw // LANES
    for j in range(n // cw):
        r = jnp.dot(h, w_ref[:, j * cw:(j + 1) * cw], preferred_element_type=F32)
        for g in range(per):
            hd = j * per + g
            piece = r[:, g * LANES:(g + 1) * LANES]
            if hd < n_rope:
                piece = _rope(piece, cq_ref[...], sq_ref[...], lo_mask)
            elif hd < 2 * n_rope:
                piece = _rope(piece, ck_ref[...], sk_ref[...], lo_mask)
            o_ref[0, hd] = piece.astype(BF16)


def _proj(x, sc, sh, w, n_lat_tiles, tm, rope_tabs=None):
    b, ltot, d = x.shape
    n = w.shape[1]
    nb = sc.shape[0] - 1
    n_rope = DA_HEADS if rope_tabs is not None else 0

    def mod_map(i, j):
        return (jnp.where(j < n_lat_tiles, i, nb), 0, 0)

    in_specs = [pl.BlockSpec((1, tm, d), lambda i, j: (i, j, 0)),
                pl.BlockSpec((1, 1, d), mod_map),
                pl.BlockSpec((1, 1, d), mod_map),
                pl.BlockSpec((d, n), lambda i, j: (0, 0))]
    args = [x, sc, sh, w]
    if rope_tabs is not None:
        in_specs += [pl.BlockSpec((tm, LANES), lambda i, j: (j, 0))] * 4
        args += list(rope_tabs)
    return pl.pallas_call(
        functools.partial(_proj_kernel, n_rope=n_rope, cw=512),
        grid=(b, ltot // tm),
        in_specs=in_specs,
        out_specs=pl.BlockSpec((1, n // LANES, tm, LANES), lambda i, j: (i, 0, j, 0)),
        out_shape=jax.ShapeDtypeStruct((b, n // LANES, ltot, LANES), BF16),
        compiler_params=_cparams(("arbitrary", "arbitrary")),
        name="mod_proj_rope" if n_rope else "mod_proj",
    )(*args)


def _gla_chunk(q, v, z, lb, st_ref, tri, mask, end_row):
    f = lb + (1.0 - lb) * jax.nn.sigmoid(z)
    logf = jnp.log(f)
    kk = 1.0 - f
    hi = logf.astype(BF16)
    r1 = logf - hi.astype(F32)
    mid = r1.astype(BF16)
    lo = (r1 - mid.astype(F32)).astype(BF16)
    parts = jnp.dot(tri, jnp.concatenate([hi, mid, lo], axis=1), preferred_element_type=F32)
    bcum = parts[:, :LANES] + parts[:, LANES:2 * LANES] + parts[:, 2 * LANES:]
    b_end = bcum[end_row:end_row + 1, :]
    q_dec = (q * jnp.exp(bcum)).astype(BF16)
    k_inv = (kk * jnp.exp(-bcum)).astype(BF16)
    k_end = (kk * jnp.exp(b_end - bcum)).astype(BF16)
    vb = v.astype(BF16)
    a = lax.dot_general(q_dec, k_inv, (((1,), (1,)), ((), ())), preferred_element_type=F32)
    a = jnp.where(mask, a, 0.0).astype(BF16)
    st = st_ref[...]
    o = jnp.dot(a, vb, preferred_element_type=F32)
    o = o + lax.dot_general(q_dec, st.astype(BF16), (((1,), (1,)), ((), ())),
                            preferred_element_type=F32)
    st_ref[...] = st * jnp.exp(b_end) + lax.dot_general(
        vb, k_end, (((0,), (0,)), ((), ())), preferred_element_type=F32)
    return o


def _gla_kernel(q_ref, i_ref, g_ref, zf_ref, zb_ref, lb_ref, ng_ref, y_ref,
                of_ref, ob_ref, sf_ref, sb_ref, *, n_lat, n_ctx):
    c = HG_CHUNK
    ltot = (n_lat + n_ctx) * c
    row = lax.broadcasted_iota(jnp.int32, (c, c), 0)
    col = lax.broadcasted_iota(jnp.int32, (c, c), 1)
    mask_f = col <= row
    mask_b = col >= row
    tri_f = mask_f.astype(BF16)
    tri_b = mask_b.astype(BF16)
    lb_f = lb_ref[0, 0:1, :]
    lb_b = lb_ref[0, 1:2, :]
    sf_ref[...] = jnp.zeros_like(sf_ref)
    sb_ref[...] = jnp.zeros_like(sb_ref)

    def segment(base, n):
        def body(j, carry):
            rf = pl.multiple_of(base + j * c, c)
            rb = pl.multiple_of(base + (n - 1 - j) * c, c)
            of_ref[pl.ds(rf, c), :] = _gla_chunk(
                q_ref[0, 0, pl.ds(rf, c), :].astype(F32), i_ref[0, 0, pl.ds(rf, c), :].astype(F32),
                zf_ref[0, 0, pl.ds(rf, c), :].astype(F32), lb_f, sf_ref, tri_f, mask_f, c - 1)
            ob_ref[pl.ds(rb, c), :] = _gla_chunk(
                q_ref[0, 0, pl.ds(rb, c), :].astype(F32), i_ref[0, 0, pl.ds(rb, c), :].astype(F32),
                zb_ref[0, 0, pl.ds(rb, c), :].astype(F32), lb_b, sb_ref, tri_b, mask_b, 0)
            return carry
        lax.fori_loop(0, n, body, 0)

    segment(n_lat * c, n_ctx)
    segment(0, n_lat)

    ng = ng_ref[0]
    rt = 256

    def readout(t, carry):
        r0 = pl.multiple_of(t * rt, rt)
        o = of_ref[pl.ds(r0, rt), :] + ob_ref[pl.ds(r0, rt), :]
        ms = jnp.mean(o * o, axis=-1, keepdims=True)
        g = g_ref[0, 0, pl.ds(r0, rt), :].astype(F32)
        y = o * lax.rsqrt(ms + RMS_EPS) * ng * (g * jax.nn.sigmoid(g))
        y_ref[0, 0, pl.ds(r0, rt), :] = y.astype(BF16)
        return carry
    lax.fori_loop(0, ltot // rt, readout, 0)


def _gla(p, lb, ng, n_lat_rows):
    b, nh5, ltot, _ = p.shape
    h = nh5 // 5
    n_lat = n_lat_rows // HG_CHUNK
    n_ctx = (ltot - n_lat_rows) // HG_CHUNK

    def spec(k):
        return pl.BlockSpec((1, 1, ltot, LANES), lambda i, j, k=k: (i, k * h + j, 0, 0))

    return pl.pallas_call(
        functools.partial(_gla_kernel, n_lat=n_lat, n_ctx=n_ctx),
        grid=(b, h),
        in_specs=[spec(0), spec(1), spec(2), spec(3), spec(4),
                  pl.BlockSpec((1, 2, LANES), lambda i, j: (j, 0, 0)),
                  pl.BlockSpec((1, 1, LANES), lambda i, j: (j, 0, 0))],
        out_specs=pl.BlockSpec((1, 1, ltot, LANES), lambda i, j: (i, j, 0, 0)),
        out_shape=jax.ShapeDtypeStruct((b, h, ltot, LANES), BF16),
        scratch_shapes=[pltpu.VMEM((ltot, LANES), F32), pltpu.VMEM((ltot, LANES), F32),
                        pltpu.VMEM((LANES, LANES), F32), pltpu.VMEM((LANES, LANES), F32)],
        compiler_params=_cparams(("arbitrary", "arbitrary")),
        name="hgrn2_gla",
    )(p, p, p, p, p, lb, ng)


def _attn_kernel(lam_ref, q_ref, k_ref, v_ref, ng_ref, y_ref, *, tk, out_scale):
    tq = q_ref.shape[2]
    lk = k_ref.shape[2]
    lam = lam_ref[0]
    q = q_ref[0, 0]
    lane = lax.broadcasted_iota(jnp.int32, (tq, LANES), 1)
    zero = jnp.zeros_like(q)
    qm = (jnp.where(lane < 64, q, zero), jnp.where(lane >= 64, q, zero))

    def body(ci, carry):
        r0 = pl.multiple_of(ci * tk, tk)
        kc = k_ref[0, 0, pl.ds(r0, tk), :]
        vc = v_ref[0, 0, pl.ds(r0, tk), :]
        out = []
        for c in range(2):
            m, l, acc = carry[c]
            s = lax.dot_general(qm[c], kc, (((1,), (1,)), ((), ())), preferred_element_type=F32)
            m_new = jnp.maximum(m, jnp.max(s, axis=-1, keepdims=True))
            alpha = jnp.exp(m - m_new)
            p = jnp.exp(s - m_new)
            l = alpha * l + jnp.sum(p, axis=-1, keepdims=True)
            acc = alpha * acc + jnp.dot(p.astype(BF16), vc, preferred_element_type=F32)
            out.append((m_new, l, acc))
        return tuple(out)

    init = tuple((jnp.full((tq, 1), -1e30, F32), jnp.zeros((tq, 1), F32),
                  jnp.zeros((tq, LANES), F32)) for _ in range(2))
    (m0, l0, a0), (m1, l1, a1) = lax.fori_loop(0, lk // tk, body, init)
    o = a0 / l0 - lam * (a1 / l1)
    ms = jnp.mean(o * o, axis=-1, keepdims=True)
    y_ref[0, 0] = (o * lax.rsqrt(ms + RMS_EPS) * ng_ref[...] * out_scale).astype(BF16)


def _attn(p, lam, ng, n_lat_rows, out_scale, tq=256, tk=768):
    b, nh3, ltot, _ = p.shape
    h = nh3 // 3
    assert ltot % tk == 0 and n_lat_rows % tq == 0
    return pl.pallas_call(
        functools.partial(_attn_kernel, tk=tk, out_scale=out_scale),
        grid=(b, h, n_lat_rows // tq),
        in_specs=[pl.BlockSpec(memory_space=pltpu.SMEM),
                  pl.BlockSpec((1, 1, tq, LANES), lambda i, j, t: (i, j, t, 0)),
                  pl.BlockSpec((1, 1, ltot, LANES), lambda i, j, t: (i, h + j, 0, 0)),
                  pl.BlockSpec((1, 1, ltot, LANES), lambda i, j, t: (i, 2 * h + j, 0, 0)),
                  pl.BlockSpec((1, LANES), lambda i, j, t: (0, 0))],
        out_specs=pl.BlockSpec((1, 1, tq, LANES), lambda i, j, t: (i, j, t, 0)),
        out_shape=jax.ShapeDtypeStruct((b, h, n_lat_rows, LANES), BF16),
        compiler_params=_cparams(("arbitrary", "arbitrary", "arbitrary")),
        name="diff_attn",
    )(lam, p, p, p, ng)


def _layer_norm(x, g, b):
    mu = jnp.mean(x, axis=-1, keepdims=True)
    xc = x - mu
    var = jnp.mean(xc * xc, axis=-1, keepdims=True)
    return xc * lax.rsqrt(var + LN_EPS) * g + b


def _post_kernel(y_ref, w_ref, x_ref, g1_ref, lng_ref, lnb_ref, sc2_ref, sh2_ref, wr_ref, br_ref,
                 xo_ref, h2_ref, lg_ref, *, alpha):
    nh = y_ref.shape[1]
    y = jnp.concatenate([y_ref[0, k] for k in range(nh)], axis=-1)
    m = jnp.dot(y, w_ref[...], preferred_element_type=F32)
    xl = _layer_norm(alpha * x_ref[0] + g1_ref[0] * m, lng_ref[...], lnb_ref[...])
    xo_ref[0] = xl
    h2 = xl * (1.0 + sc2_ref[0]) + sh2_ref[0]
    h2_ref[0] = h2
    lg_ref[0] = jnp.dot(h2, wr_ref[...], precision=HIGHEST, preferred_element_type=F32) + br_ref[...]


def _post(y, w_out, x, g1, lng, lnb, sc2, sh2, wr, br, n_lat_tiles, n_tiles, tm, alpha):
    b, nh, _, _ = y.shape
    d = x.shape[2]
    nb = g1.shape[0] - 1
    rows = n_tiles * tm

    def mod_map(i, j):
        return (jnp.where(j < n_lat_tiles, i, nb), 0, 0)

    tok = pl.BlockSpec((1, tm, d), lambda i, j: (i, j, 0))
    vec = pl.BlockSpec((1, d), lambda i, j: (0, 0))
    return pl.pallas_call(
        functools.partial(_post_kernel, alpha=alpha),
        grid=(b, n_tiles),
        in_specs=[pl.BlockSpec((1, nh, tm, LANES), lambda i, j: (i, 0, j, 0)),
                  pl.BlockSpec((d, d), lambda i, j: (0, 0)),
                  tok, pl.BlockSpec((1, 1, d), mod_map), vec, vec,
                  pl.BlockSpec((1, 1, d), mod_map), pl.BlockSpec((1, 1, d), mod_map),
                  pl.BlockSpec((d, LANES), lambda i, j: (0, 0)),
                  pl.BlockSpec((1, LANES), lambda i, j: (0, 0))],
        out_specs=[tok, tok, pl.BlockSpec((1, tm, LANES), lambda i, j: (i, j, 0))],
        out_shape=[jax.ShapeDtypeStruct((b, rows, d), F32),
                   jax.ShapeDtypeStruct((b, rows, d), F32),
                   jax.ShapeDtypeStruct((b, rows, LANES), F32)],
        compiler_params=_cparams(("arbitrary", "arbitrary")),
        name="out_proj_ln",
    )(y, w_out, x, g1, lng, lnb, sc2, sh2, wr, br)


def _router_kernel(lg_ref, idx_ref, gate_ref, rank_ref, cnt_ref, carry_ref):
    tm = lg_ref.shape[0]

    @pl.when(pl.program_id(0) == 0)
    def _():
        carry_ref[...] = jnp.zeros_like(carry_ref)

    work = lg_ref[...]
    lane = lax.broadcasted_iota(jnp.int32, (tm, LANES), 1)
    lane_f = lane.astype(F32)
    vals, onehots = [], []
    idx_out = jnp.zeros((tm, LANES), jnp.int32)
    for k in range(TOP_K):
        mx = jnp.max(work, axis=-1, keepdims=True)
        am = jnp.min(jnp.where(work == mx, lane_f, float(LANES)), axis=-1, keepdims=True)
        oh = lane_f == am
        vals.append(mx)
        onehots.append(oh)
        idx_out = jnp.where(lane == k, am.astype(jnp.int32), idx_out)
        work = jnp.where(oh, -3e38, work)
    es = [jnp.exp(v - vals[0]) for v in vals]
    denom = es[0] + es[1] + es[2] + es[3]
    gate_out = jnp.zeros((tm, LANES), F32)
    for k in range(TOP_K):
        gate_out = jnp.where(lane == k, es[k] / denom, gate_out)
    member = (onehots[0] | onehots[1] | onehots[2] | onehots[3])
    r = lax.broadcasted_iota(jnp.int32, (tm, tm), 0)
    c = lax.broadcasted_iota(jnp.int32, (tm, tm), 1)
    strict = (c < r).astype(BF16)
    prefix = jnp.dot(strict, member.astype(BF16), preferred_element_type=F32) + carry_ref[...]
    rank_out = jnp.zeros((tm, LANES), jnp.int32)
    for k in range(TOP_K):
        rk = jnp.sum(jnp.where(onehots[k], prefix, 0.0), axis=-1, keepdims=True)
        rank_out = jnp.where(lane == k, rk.astype(jnp.int32), rank_out)
    carry_ref[...] = carry_ref[...] + jnp.sum(member.astype(F32), axis=0, keepdims=True)
    idx_ref[...] = idx_out
    gate_ref[...] = gate_out
    rank_ref[...] = rank_out
    cnt_ref[...] = carry_ref[...]


def _router(logits, tm):
    t = logits.shape[0]
    tile = pl.BlockSpec((tm, LANES), lambda i: (i, 0))
    return pl.pallas_call(
        _router_kernel,
        grid=(t // tm,),
        in_specs=[tile],
        out_specs=[tile, tile, tile, pl.BlockSpec((1, LANES), lambda i: (0, 0))],
        out_shape=[jax.ShapeDtypeStruct((t, LANES), jnp.int32),
                   jax.ShapeDtypeStruct((t, LANES), F32),
                   jax.ShapeDtypeStruct((t, LANES), jnp.int32),
                   jax.ShapeDtypeStruct((1, LANES), F32)],
        scratch_shapes=[pltpu.VMEM((1, LANES), F32)],
        compiler_params=_cparams(("arbitrary",)),
        name="router_topk",
    )(logits)


def _dispatch_kernel(pos_ref, h_ref, xs_in_ref, xs_ref, sem):
    del xs_in_ref
    tm = h_ref.shape[0]

    def row_copy(r, p):
        return pltpu.make_async_copy(h_ref.at[pl.ds(r, 1)], xs_ref.at[pl.ds(p, 1)], sem)

    def issue(r, carry):
        for k in range(TOP_K):
            row_copy(r, pos_ref[r * TOP_K + k]).start()
        return carry
    lax.fori_loop(0, tm, issue, 0)

    def drain(r, carry):
        for k in range(TOP_K):
            row_copy(r, pos_ref[r * TOP_K + k]).wait()
        return carry
    lax.fori_loop(0, tm, drain, 0)


def _dispatch(h2, pos, n_rows, tm):
    t, d = h2.shape
    xs0 = jnp.zeros((n_rows, d), F32)
    return pl.pallas_call(
        _dispatch_kernel,
        grid=(t // tm,),
        in_specs=[pl.BlockSpec((tm * TOP_K,), lambda i: (i,), memory_space=pltpu.SMEM),
                  pl.BlockSpec((tm, d), lambda i: (i, 0)),
                  pl.BlockSpec(memory_space=pl.ANY)],
        out_specs=pl.BlockSpec(memory_space=pl.ANY),
        out_shape=jax.ShapeDtypeStruct((n_rows, d), F32),
        scratch_shapes=[pltpu.SemaphoreType.DMA(())],
        input_output_aliases={2: 0},
        compiler_params=_cparams(("arbitrary",)),
        name="moe_dispatch",
    )(pos, h2, xs0)


def _ffn_kernel(be_ref, nu_ref, x_ref, wgu_ref, bgu_ref, wdn_ref, bdn_ref, y_ref):
    de = wdn_ref.shape[1]

    @pl.when(pl.program_id(0) < nu_ref[0])
    def _():
        x = x_ref[...].astype(BF16)
        gu = jnp.dot(x, wgu_ref[0], preferred_element_type=F32) + bgu_ref[0]
        gate = jnp.minimum(gu[:, :de], SWIGLU_LIMIT)
        up = jnp.clip(gu[:, de:], -SWIGLU_LIMIT, SWIGLU_LIMIT)
        act = (up + 1.0) * gate * jax.nn.sigmoid(SWIGLU_ALPHA * gate)
        y_ref[...] = jnp.dot(act.astype(BF16), wdn_ref[0], preferred_element_type=F32) + bdn_ref[0]

    @pl.when(pl.program_id(0) >= nu_ref[0])
    def _():
        y_ref[...] = jnp.zeros_like(y_ref)


def _ffn(xs, block_expert, n_used, wgu, bgu, wdn, bdn, tb):
    p, d = xs.shape
    e, _, n2 = wgu.shape
    de = wdn.shape[1]
    grid_spec = pltpu.PrefetchScalarGridSpec(
        num_scalar_prefetch=2,
        grid=(p // tb,),
        in_specs=[pl.BlockSpec((tb, d), lambda i, be, nu: (i, 0)),
                  pl.BlockSpec((1, d, n2), lambda i, be, nu: (be[i], 0, 0)),
                  pl.BlockSpec((1, 1, n2), lambda i, be, nu: (be[i], 0, 0)),
                  pl.BlockSpec((1, de, d), lambda i, be, nu: (be[i], 0, 0)),
                  pl.BlockSpec((1, 1, d), lambda i, be, nu: (be[i], 0, 0))],
        out_specs=pl.BlockSpec((tb, d), lambda i, be, nu: (i, 0)),
    )
    return pl.pallas_call(
        _ffn_kernel,
        grid_spec=grid_spec,
        out_shape=jax.ShapeDtypeStruct((p, d), F32),
        compiler_params=_cparams(("arbitrary",)),
        name="moe_ffn",
    )(block_expert, n_used, xs, wgu, bgu.reshape(e, 1, n2), wdn, bdn.reshape(e, 1, d))


def _combine_kernel(pos_ref, gate_ref, x_ref, g2_ref, lng_ref, lnb_ref, ys_ref, o_ref, buf_ref, sem,
                    *, alpha):
    tm = x_ref.shape[1]

    def row_copy(r, k, p):
        return pltpu.make_async_copy(ys_ref.at[pl.ds(p, 1)], buf_ref.at[k, pl.ds(r, 1)], sem)

    def issue(r, carry):
        for k in range(TOP_K):
            row_copy(r, k, pos_ref[r * TOP_K + k]).start()
        return carry
    lax.fori_loop(0, tm, issue, 0)

    def drain(r, carry):
        for k in range(TOP_K):
            row_copy(r, k, pos_ref[r * TOP_K + k]).wait()
        return carry
    lax.fori_loop(0, tm, drain, 0)

    gate = gate_ref[...]
    f = gate[:, 0:1] * buf_ref[0]
    for k in range(1, TOP_K):
        f = f + gate[:, k:k + 1] * buf_ref[k]
    o_ref[0] = _layer_norm(alpha * x_ref[0] + g2_ref[0] * f, lng_ref[...], lnb_ref[...])


def _combine(pos, gates, x, g2, lng, lnb, ys, n_lat_tiles, n_tiles, tm, alpha):
    b, _, d = x.shape
    nb = g2.shape[0] - 1

    def mod_map(i, j):
        return (jnp.where(j < n_lat_tiles, i, nb), 0, 0)

    tok = pl.BlockSpec((1, tm, d), lambda i, j: (i, j, 0))
    vec = pl.BlockSpec((1, d), lambda i, j: (0, 0))
    return pl.pallas_call(
        functools.partial(_combine_kernel, alpha=alpha),
        grid=(b, n_tiles),
        in_specs=[pl.BlockSpec((tm * TOP_K,), lambda i, j: (i * n_tiles + j,), memory_space=pltpu.SMEM),
                  pl.BlockSpec((tm, LANES), lambda i, j: (i * n_tiles + j, 0)),
                  tok, pl.BlockSpec((1, 1, d), mod_map), vec, vec,
                  pl.BlockSpec(memory_space=pl.ANY)],
        out_specs=tok,
        out_shape=jax.ShapeDtypeStruct((b, n_tiles * tm, d), F32),
        scratch_shapes=[pltpu.VMEM((TOP_K, tm, d), F32), pltpu.SemaphoreType.DMA(())],
        compiler_params=_cparams(("arbitrary", "arbitrary")),
        name="moe_combine_ln",
    )(pos, gates, x, g2, lng, lnb, ys)


def _moe_layer(h2, logits, x_res, g2, lng, lnb, wgu, bgu, wdn, bdn, n_lat_tiles, n_tiles, tm, alpha):
    b, rows, d = h2.shape
    t = b * rows
    tb = 512
    idx, gates, rank, counts = _router(logits.reshape(t, LANES), tm)
    counts = counts[0, :N_EXPERTS].astype(jnp.int32)
    padded = (counts + tb - 1) // tb * tb
    pend = jnp.cumsum(padded)
    pstart = pend - padded
    idx4 = idx[:, :TOP_K]
    start_of = jnp.sum(jnp.where(idx4[:, :, None] == jnp.arange(N_EXPERTS)[None, None, :],
                                 pstart[None, None, :], 0), axis=-1)
    pos = (start_of + rank[:, :TOP_K]).reshape(t * TOP_K)
    n_rows = (t * TOP_K + tb - 1) // tb * tb + N_EXPERTS * tb
    n_blocks = n_rows // tb
    block_start = jnp.arange(n_blocks, dtype=jnp.int32) * tb
    block_expert = jnp.minimum(jnp.sum((pend[None, :] <= block_start[:, None]).astype(jnp.int32), axis=1),
                               N_EXPERTS - 1).astype(jnp.int32)
    n_used = (pend[-1:] // tb).astype(jnp.int32)
    xs = _dispatch(h2.reshape(t, d), pos, n_rows, tm)
    ys = _ffn(xs, block_expert, n_used, wgu, bgu, wdn, bdn, tb)
    return _combine(pos, gates, x_res, g2, lng, lnb, ys, n_lat_tiles, n_tiles, tm, alpha)


def _rope_tables(n_lat_rows, n_rows, scale):
    pos = jnp.arange(n_lat_rows)
    lane = jnp.arange(LANES)
    m = lane % 64
    n_freq = 16
    inv = ROPE_BASE ** (-(m % n_freq).astype(F32) / n_freq)
    p = jnp.where((m // 32)[None, :] == 0, (pos // GRID_W)[:, None], (pos % GRID_W)[:, None]).astype(F32)
    ang = p * inv[None, :]
    sign = jnp.where((m % 32) < n_freq, -1.0, 1.0)[None, :]
    cos = jnp.concatenate([jnp.cos(ang), jnp.ones((n_rows - n_lat_rows, LANES), F32)], axis=0)
    sin = jnp.concatenate([jnp.sin(ang) * sign, jnp.zeros((n_rows - n_lat_rows, LANES), F32)], axis=0)
    return cos * scale, sin * scale


def kernel(x, c, ctx, c_ctx, w_ada, b_ada, ln_g, ln_b, hg_w_in, hg_lb, hg_norm_g, hg_w_out, da_w_in, da_lam,
           da_norm_g, da_w_out, moe_w_router, moe_b_router, moe_w_gu, moe_b_gu, moe_w_dn, moe_b_dn):
    bsz, seq, d = x.shape
    lc = ctx.shape[1]
    depth = w_ada.shape[0]
    assert depth == 2 and hg_w_in.shape[0] == 1 and da_w_in.shape[0] == 1
    ltot = seq + lc
    tm = 256
    assert seq % tm == 0 and lc % tm == 0 and seq % GRID_W == 0 and d == DA_HEADS * LANES
    alpha = (2 * depth) ** 0.25
    n_lat_tiles = seq // tm
    n_all_tiles = ltot // tm
    h_heads = d // LANES

    n_mod = bsz + 1
    c_pad = jnp.zeros((16, d), F32).at[:bsz].set(c).at[bsz].set(c_ctx)
    mod = _ada(c_pad, w_ada, b_ada)[:, :n_mod]

    def mods(l):
        return [mod[l, :, k * d:(k + 1) * d].reshape(n_mod, 1, d) for k in range(6)]

    def router_params(l):
        wr = jnp.zeros((d, LANES), F32).at[:, :N_EXPERTS].set(moe_w_router[l])
        br = jnp.full((1, LANES), -1e30, F32).at[0, :N_EXPERTS].set(moe_b_router[l])
        return wr, br

    xall = jnp.concatenate([x, ctx], axis=1)

    sh1, sc1, g1, sh2, sc2, g2 = mods(0)
    p0 = _proj(xall, sc1, sh1, hg_w_in[0].astype(BF16), n_lat_tiles, tm)
    lb = jnp.cumsum(jax.nn.softmax(hg_lb.astype(F32), axis=1), axis=1)[:, 0]
    lb = lb.reshape(2, h_heads, LANES).transpose(1, 0, 2)
    y0 = _gla(p0, lb, hg_norm_g[0].reshape(h_heads, 1, LANES), seq)
    wr, br = router_params(0)
    x0, h20, lg0 = _post(y0, hg_w_out[0].astype(BF16), xall, g1, ln_g[0, 0].reshape(1, d),
                         ln_b[0, 0].reshape(1, d), sc2, sh2, wr, br, n_lat_tiles, n_all_tiles, tm, alpha)
    x1 = _moe_layer(h20, lg0, x0, g2, ln_g[0, 1].reshape(1, d), ln_b[0, 1].reshape(1, d),
                    moe_w_gu[0].astype(BF16), moe_b_gu[0], moe_w_dn[0].astype(BF16), moe_b_dn[0],
                    n_lat_tiles, n_all_tiles, tm, alpha)

    sh1, sc1, g1, sh2, sc2, g2 = mods(1)
    dh = d // DA_HEADS // 2
    cq, sq = _rope_tables(seq, ltot, dh ** -0.5)
    ck, sk = _rope_tables(seq, ltot, 1.0)
    p1 = _proj(x1, sc1, sh1, da_w_in[0].astype(BF16), n_lat_tiles, tm, rope_tabs=(cq, sq, ck, sk))
    lam_init = 0.8 - 0.6 * math.exp(-0.3 * 1)
    lp = da_lam[0].astype(F32)
    lam = (jnp.exp(jnp.sum(lp[0] * lp[1])) - jnp.exp(jnp.sum(lp[2] * lp[3])) + lam_init).reshape(1)
    y1 = _attn(p1, lam, da_norm_g[0].reshape(1, LANES), seq, 1.0 - lam_init)
    wr, br = router_params(1)
    x2, h21, lg1 = _post(y1, da_w_out[0].astype(BF16), x1, g1, ln_g[1, 0].reshape(1, d),
                         ln_b[1, 0].reshape(1, d), sc2, sh2, wr, br, n_lat_tiles, n_lat_tiles, tm, alpha)
    return _moe_layer(h21, lg1, x2, g2, ln_g[1, 1].reshape(1, d), ln_b[1, 1].reshape(1, d),
                      moe_w_gu[1].astype(BF16), moe_b_gu[1], moe_w_dn[1].astype(BF16), moe_b_dn[1],
                      n_lat_tiles, n_lat_tiles, tm, alpha)
```

```python
import functools
import math

import jax
import jax.numpy as jnp
from jax import lax
from jax.experimental import pallas as pl
from jax.experimental.pallas import tpu as pltpu

F32 = jnp.float32
BF16 = jnp.bfloat16
HIGHEST = lax.Precision.HIGHEST

LANES = 128
HG_CHUNK = 64
GRID_W = 64
ROPE_BASE = 10000.0
DA_HEADS = 8
N_EXPERTS = 32
TOP_K = 4
SWIGLU_LIMIT = 7.0
SWIGLU_ALPHA = 1.702
LN_EPS = 1e-5
RMS_EPS = 1e-6
VMEM_LIMIT = 56 * 1024 * 1024


def _cparams(sem):
    return pltpu.CompilerParams(dimension_semantics=sem, vmem_limit_bytes=VMEM_LIMIT)


def _ada_kernel(c_ref, w_ref, b_ref, o_ref):
    c = c_ref[...]
    s = c * jax.nn.sigmoid(c)
    o_ref[0] = jnp.dot(s, w_ref[0], precision=HIGHEST, preferred_element_type=F32) + b_ref[0]


def _ada(c_pad, w_ada, b_ada):
    depth, d, n = w_ada.shape
    rows = c_pad.shape[0]
    tn = 1536
    return pl.pallas_call(
        _ada_kernel,
        grid=(depth, n // tn),
        in_specs=[pl.BlockSpec((rows, d), lambda l, j: (0, 0)),
                  pl.BlockSpec((1, d, tn), lambda l, j: (l, 0, j)),
                  pl.BlockSpec((1, 1, tn), lambda l, j: (l, 0, j))],
        out_specs=pl.BlockSpec((1, rows, tn), lambda l, j: (l, 0, j)),
        out_shape=jax.ShapeDtypeStruct((depth, rows, n), F32),
        compiler_params=_cparams(("arbitrary", "arbitrary")),
        name="ada_mod",
    )(c_pad, w_ada, b_ada.reshape(depth, 1, n))


def _rope(piece, cos, sin, lo_mask):
    rot = jnp.where(lo_mask, pltpu.roll(piece, LANES - 16, 1), pltpu.roll(piece, 16, 1))
    return piece * cos + rot * sin


def _proj_kernel(x_ref, sc_ref, sh_ref, w_ref, *rest, n_rope, cw):
    if n_rope:
        cq_ref, sq_ref, ck_ref, sk_ref, o_ref = rest
    else:
        (o_ref,) = rest
    n = w_ref.shape[1]
    h = (x_ref[0] * (1.0 + sc_ref[0]) + sh_ref[0]).astype(BF16)
    if n_rope:
        lane = lax.broadcasted_iota(jnp.int32, (x_ref.shape[1], LANES), 1)
        lo_mask = (lane % 32) < 16
    per = cw // LANES
    for j in range(n // cw):
        r = jnp.dot(h, w_ref[:, j * cw:(j + 1) * cw], preferred_element_type=F32)
        for g in range(per):
            hd = j * per + g
            piece = r[:, g * LANES:(g + 1) * LANES]
            if hd < n_rope:
                piece = _rope(piece, cq_ref[...], sq_ref[...], lo_mask)
            elif hd < 2 * n_rope:
                piece = _rope(piece, ck_ref[...], sk_ref[...], lo_mask)
            o_ref[0, hd] = piece.astype(BF16)


def _proj(x, sc, sh, w, n_lat_tiles, tm, rope_tabs=None):
    b, ltot, d = x.shape
    n = w.shape[1]
    nb = sc.shape[0] - 1
    n_rope = DA_HEADS if rope_tabs is not None else 0

    def mod_map(i, j):
        return (jnp.where(j < n_lat_tiles, i, nb), 0, 0)

    in_specs = [pl.BlockSpec((1, tm, d), lambda i, j: (i, j, 0)),
                pl.BlockSpec((1, 1, d), mod_map),
                pl.BlockSpec((1, 1, d), mod_map),
                pl.BlockSpec((d, n), lambda i, j: (0, 0))]
    args = [x, sc, sh, w]
    if rope_tabs is not None:
        in_specs += [pl.BlockSpec((tm, LANES), lambda i, j: (j, 0))] * 4
        args += list(rope_tabs)
    return pl.pallas_call(
        functools.partial(_proj_kernel, n_rope=n_rope, cw=512),
        grid=(b, ltot // tm),
        in_specs=in_specs,
        out_specs=pl.BlockSpec((1, n // LANES, tm, LANES), lambda i, j: (i, 0, j, 0)),
        out_shape=jax.ShapeDtypeStruct((b, n // LANES, ltot, LANES), BF16),
        compiler_params=_cparams(("arbitrary", "arbitrary")),
        name="mod_proj_rope" if n_rope else "mod_proj",
    )(*args)


GLA_BLOCK = 256


def _gla_local(items):
    nc = GLA_BLOCK // HG_CHUNK
    kks, cats = [], []
    for (_, _, z, lb, _, _, _) in items:
        f = lb + (1.0 - lb) * jax.nn.sigmoid(z)
        logf = jnp.log(f)
        kks.append(1.0 - f)
        hi = logf.astype(BF16)
        lo = (logf - hi.astype(F32)).astype(BF16)
        cats.append(jnp.concatenate([hi, lo], axis=1))
    parts = [jnp.dot(it[4], cat, preferred_element_type=F32) for it, cat in zip(items, cats)]
    q_decs, k_invs, k_ends, decs = [], [], [], []
    for (q, _, _, _, _, _, end_row), kk, part in zip(items, kks, parts):
        bcum = part[:, :LANES] + part[:, LANES:]
        b_end = bcum.reshape(nc, HG_CHUNK, LANES)[:, end_row:end_row + 1, :]
        b_end_rows = jnp.broadcast_to(b_end, (nc, HG_CHUNK, LANES)).reshape(GLA_BLOCK, LANES)
        q_decs.append((q * jnp.exp(bcum)).astype(BF16))
        k_invs.append((kk * jnp.exp(-bcum)).astype(BF16))
        k_ends.append((kk * jnp.exp(b_end_rows - bcum)).astype(BF16))
        decs.append(jnp.exp(b_end.reshape(nc, LANES)))
    scores = [lax.dot_general(qd, ki, (((1,), (1,)), ((), ())), preferred_element_type=F32)
              for qd, ki in zip(q_decs, k_invs)]
    scores = [jnp.where(it[5], a, 0.0).astype(BF16) for it, a in zip(items, scores)]
    outs = [jnp.dot(a, it[1], preferred_element_type=F32) for it, a in zip(items, scores)]
    return list(zip(outs, q_decs, k_ends, decs))


def _gla_kernel(q_ref, i_ref, g_ref, zf_ref, zb_ref, lb_ref, ng_ref, y_ref,
                of_ref, ob_ref, qd_ref, ke_ref, de_ref, st_ref, *, n_lat, n_ctx):
    c = HG_CHUNK
    blk = GLA_BLOCK
    nc = blk // c
    ltot = (n_lat + n_ctx) * c
    row = lax.broadcasted_iota(jnp.int32, (blk, blk), 0)
    col = lax.broadcasted_iota(jnp.int32, (blk, blk), 1)
    same = (row // c) == (col // c)
    masks = (same & (col <= row), same & (col >= row))
    tris = (masks[0].astype(BF16), masks[1].astype(BF16))
    z_refs = (zf_ref, zb_ref)
    o_refs = (of_ref, ob_ref)
    end_rows = (c - 1, 0)

    n_blk = ltot // blk
    per_step = 3 if n_blk % 3 == 0 else 1

    def local(t, carry):
        work = []
        for u in range(per_step):
            r0 = pl.multiple_of((t * per_step + u) * blk, blk)
            c0 = pl.multiple_of((t * per_step + u) * nc, nc)
            q = q_ref[0, 0, pl.ds(r0, blk), :].astype(F32)
            v = i_ref[0, 0, pl.ds(r0, blk), :]
            zs = [z_refs[d][0, 0, pl.ds(r0, blk), :].astype(F32) for d in range(2)]
            work.append((r0, c0, q, v, zs))
        res = _gla_local([(q, v, zs[d], lb_ref[0, d:d + 1, :], tris[d], masks[d], end_rows[d])
                          for (_, _, q, v, zs) in work for d in range(2)])
        for u, (r0, c0, _, _, _) in enumerate(work):
            for d in range(2):
                o, q_dec, k_end, dec = res[2 * u + d]
                o_refs[d][pl.ds(r0, blk), :] = o
                qd_ref[d, pl.ds(r0, blk), :] = q_dec
                ke_ref[d, pl.ds(r0, blk), :] = k_end
                de_ref[d, pl.ds(c0, nc), :] = dec
        return carry
    lax.fori_loop(0, n_blk // per_step, local, 0)

    st_ref[...] = jnp.zeros_like(st_ref)

    def segment(first, n):
        per = 4 if n % 4 == 0 else 1

        def body(jj, carry):
            todo = []
            for u in range(per):
                j = jj * per + u
                for d, ch in enumerate((first + j, first + n - 1 - j)):
                    r0 = pl.multiple_of(ch * c, c)
                    kv = lax.dot_general(i_ref[0, 0, pl.ds(r0, c), :], ke_ref[d, pl.ds(r0, c), :],
                                         (((0,), (0,)), ((), ())), preferred_element_type=F32)
                    todo.append((d, r0, kv, de_ref[d, pl.ds(ch, 1), :], qd_ref[d, pl.ds(r0, c), :]))
            st = [st_ref[0], st_ref[1]]
            inter = []
            for (d, r0, kv, dec, q_dec) in todo:
                inter.append((d, r0, lax.dot_general(q_dec, st[d].astype(BF16), (((1,), (1,)), ((), ())),
                                                     preferred_element_type=F32)))
                st[d] = st[d] * dec + kv
            for (d, r0, o) in inter:
                o_refs[d][pl.ds(r0, c), :] += o
            st_ref[0] = st[0]
            st_ref[1] = st[1]
            return carry
        lax.fori_loop(0, n // per, body, 0)

    segment(n_lat, n_ctx)
    segment(0, n_lat)

    ng = ng_ref[0]
    rt = 256

    def readout(t, carry):
        r0 = pl.multiple_of(t * rt, rt)
        o = of_ref[pl.ds(r0, rt), :] + ob_ref[pl.ds(r0, rt), :]
        ms = jnp.mean(o * o, axis=-1, keepdims=True)
        g = g_ref[0, 0, pl.ds(r0, rt), :].astype(F32)
        y = o * lax.rsqrt(ms + RMS_EPS) * ng * (g * jax.nn.sigmoid(g))
        y_ref[0, 0, pl.ds(r0, rt), :] = y.astype(BF16)
        return carry
    lax.fori_loop(0, ltot // rt, readout, 0)


def _gla(p, lb, ng, n_lat_rows):
    b, nh5, ltot, _ = p.shape
    h = nh5 // 5
    n_lat = n_lat_rows // HG_CHUNK
    n_ctx = (ltot - n_lat_rows) // HG_CHUNK

    def spec(k):
        return pl.BlockSpec((1, 1, ltot, LANES), lambda i, j, k=k: (i, k * h + j, 0, 0))

    return pl.pallas_call(
        functools.partial(_gla_kernel, n_lat=n_lat, n_ctx=n_ctx),
        grid=(b, h),
        in_specs=[spec(0), spec(1), spec(2), spec(3), spec(4),
                  pl.BlockSpec((1, 2, LANES), lambda i, j: (j, 0, 0)),
                  pl.BlockSpec((1, 1, LANES), lambda i, j: (j, 0, 0))],
        out_specs=pl.BlockSpec((1, 1, ltot, LANES), lambda i, j: (i, j, 0, 0)),
        out_shape=jax.ShapeDtypeStruct((b, h, ltot, LANES), BF16),
        scratch_shapes=[pltpu.VMEM((ltot, LANES), F32), pltpu.VMEM((ltot, LANES), F32),
                        pltpu.VMEM((2, ltot, LANES), BF16), pltpu.VMEM((2, ltot, LANES), BF16),
                        pltpu.VMEM((2, ltot // HG_CHUNK, LANES), F32),
                        pltpu.VMEM((2, LANES, LANES), F32)],
        compiler_params=_cparams(("arbitrary", "arbitrary")),
        name="hgrn2_gla",
    )(p, p, p, p, p, lb, ng)


def _attn_kernel(lam_ref, q_ref, k_ref, v_ref, ng_ref, y_ref, *, tk, out_scale):
    tq = q_ref.shape[2]
    lk = k_ref.shape[2]
    lam = lam_ref[0]
    q = q_ref[0, 0]
    lane = lax.broadcasted_iota(jnp.int32, (tq, LANES), 1)
    zero = jnp.zeros_like(q)
    qm = (jnp.where(lane < 64, q, zero), jnp.where(lane >= 64, q, zero))
    ones = jnp.ones((tk, LANES), BF16)

    def scores(ci):
        kc = k_ref[0, 0, ci * tk:(ci + 1) * tk, :]
        return [lax.dot_general(qm[c], kc, (((1,), (1,)), ((), ())), preferred_element_type=F32)
                for c in range(2)]

    n_chunks = lk // tk
    m = [jnp.full((tq, 1), -1e30, F32) for _ in range(2)]
    acc = [jnp.zeros((tq, 2 * LANES), F32) for _ in range(2)]
    s_next = scores(0)
    for ci in range(n_chunks):
        s_cur = s_next
        if ci + 1 < n_chunks:
            s_next = scores(ci + 1)
        vext = jnp.concatenate([v_ref[0, 0, ci * tk:(ci + 1) * tk, :], ones], axis=1)
        for c in range(2):
            m_new = jnp.maximum(m[c], jnp.max(s_cur[c], axis=-1, keepdims=True))
            p = jnp.exp2(s_cur[c] - m_new).astype(BF16)
            acc[c] = jnp.exp2(m[c] - m_new) * acc[c] + jnp.dot(p, vext, preferred_element_type=F32)
            m[c] = m_new
    a0, a1 = acc
    o = a0[:, :LANES] / a0[:, LANES:LANES + 1] - lam * (a1[:, :LANES] / a1[:, LANES:LANES + 1])
    ms = jnp.mean(o * o, axis=-1, keepdims=True)
    y_ref[0, 0] = (o * lax.rsqrt(ms + RMS_EPS) * ng_ref[...] * out_scale).astype(BF16)


def _attn(p, lam, ng, n_lat_rows, out_scale, tq=256, tk=768):
    b, nh3, ltot, _ = p.shape
    h = nh3 // 3
    assert ltot % tk == 0 and n_lat_rows % tq == 0
    return pl.pallas_call(
        functools.partial(_attn_kernel, tk=tk, out_scale=out_scale),
        grid=(b, h, n_lat_rows // tq),
        in_specs=[pl.BlockSpec(memory_space=pltpu.SMEM),
                  pl.BlockSpec((1, 1, tq, LANES), lambda i, j, t: (i, j, t, 0)),
                  pl.BlockSpec((1, 1, ltot, LANES), lambda i, j, t: (i, h + j, 0, 0)),
                  pl.BlockSpec((1, 1, ltot, LANES), lambda i, j, t: (i, 2 * h + j, 0, 0)),
                  pl.BlockSpec((1, LANES), lambda i, j, t: (0, 0))],
        out_specs=pl.BlockSpec((1, 1, tq, LANES), lambda i, j, t: (i, j, t, 0)),
        out_shape=jax.ShapeDtypeStruct((b, h, n_lat_rows, LANES), BF16),
        compiler_params=_cparams(("arbitrary", "arbitrary", "arbitrary")),
        name="diff_attn",
    )(lam, p, p, p, ng)


def _layer_norm(x, g, b):
    mu = jnp.mean(x, axis=-1, keepdims=True)
    xc = x - mu
    var = jnp.mean(xc * xc, axis=-1, keepdims=True)
    return xc * lax.rsqrt(var + LN_EPS) * g + b


def _post_kernel(y_ref, w_ref, x_ref, g1_ref, lng_ref, lnb_ref, sc2_ref, sh2_ref, wr_ref, br_ref,
                 xo_ref, h2_ref, lg_ref, *, alpha):
    nh = y_ref.shape[1]
    y = jnp.concatenate([y_ref[0, k] for k in range(nh)], axis=-1)
    m = jnp.dot(y, w_ref[...], preferred_element_type=F32)
    xl = _layer_norm(alpha * x_ref[0] + g1_ref[0] * m, lng_ref[...], lnb_ref[...])
    xo_ref[0] = xl
    h2 = xl * (1.0 + sc2_ref[0]) + sh2_ref[0]
    h2_ref[0] = h2
    lg_ref[0] = jnp.dot(h2, wr_ref[...], precision=HIGHEST, preferred_element_type=F32) + br_ref[...]


def _post(y, w_out, x, g1, lng, lnb, sc2, sh2, wr, br, n_lat_tiles, n_tiles, tm, alpha):
    b, nh, _, _ = y.shape
    d = x.shape[2]
    nb = g1.shape[0] - 1
    rows = n_tiles * tm

    def mod_map(i, j):
        return (jnp.where(j < n_lat_tiles, i, nb), 0, 0)

    tok = pl.BlockSpec((1, tm, d), lambda i, j: (i, j, 0))
    vec = pl.BlockSpec((1, d), lambda i, j: (0, 0))
    return pl.pallas_call(
        functools.partial(_post_kernel, alpha=alpha),
        grid=(b, n_tiles),
        in_specs=[pl.BlockSpec((1, nh, tm, LANES), lambda i, j: (i, 0, j, 0)),
                  pl.BlockSpec((d, d), lambda i, j: (0, 0)),
                  tok, pl.BlockSpec((1, 1, d), mod_map), vec, vec,
                  pl.BlockSpec((1, 1, d), mod_map), pl.BlockSpec((1, 1, d), mod_map),
                  pl.BlockSpec((d, LANES), lambda i, j: (0, 0)),
                  pl.BlockSpec((1, LANES), lambda i, j: (0, 0))],
        out_specs=[tok, tok, pl.BlockSpec((1, tm, LANES), lambda i, j: (i, j, 0))],
        out_shape=[jax.ShapeDtypeStruct((b, rows, d), F32),
                   jax.ShapeDtypeStruct((b, rows, d), F32),
                   jax.ShapeDtypeStruct((b, rows, LANES), F32)],
        compiler_params=_cparams(("arbitrary", "arbitrary")),
        name="out_proj_ln",
    )(y, w_out, x, g1, lng, lnb, sc2, sh2, wr, br)


def _router_kernel(lg_ref, idx_ref, gate_ref, rank_ref, cnt_ref, carry_ref):
    tm = lg_ref.shape[0]

    @pl.when(pl.program_id(0) == 0)
    def _():
        carry_ref[...] = jnp.zeros_like(carry_ref)

    work = lg_ref[...]
    lane = lax.broadcasted_iota(jnp.int32, (tm, LANES), 1)
    lane_f = lane.astype(F32)
    vals, onehots = [], []
    idx_out = jnp.zeros((tm, LANES), jnp.int32)
    for k in range(TOP_K):
        mx = jnp.max(work, axis=-1, keepdims=True)
        am = jnp.min(jnp.where(work == mx, lane_f, float(LANES)), axis=-1, keepdims=True)
        oh = lane_f == am
        vals.append(mx)
        onehots.append(oh)
        idx_out = jnp.where(lane == k, am.astype(jnp.int32), idx_out)
        work = jnp.where(oh, -3e38, work)
    es = [jnp.exp(v - vals[0]) for v in vals]
    denom = es[0] + es[1] + es[2] + es[3]
    gate_out = jnp.zeros((tm, LANES), F32)
    for k in range(TOP_K):
        gate_out = jnp.where(lane == k, es[k] / denom, gate_out)
    member = (onehots[0] | onehots[1] | onehots[2] | onehots[3])
    r = lax.broadcasted_iota(jnp.int32, (tm, tm), 0)
    c = lax.broadcasted_iota(jnp.int32, (tm, tm), 1)
    strict = (c < r).astype(BF16)
    prefix = jnp.dot(strict, member.astype(BF16), preferred_element_type=F32) + carry_ref[...]
    rank_out = jnp.zeros((tm, LANES), jnp.int32)
    for k in range(TOP_K):
        rk = jnp.sum(jnp.where(onehots[k], prefix, 0.0), axis=-1, keepdims=True)
        rank_out = jnp.where(lane == k, rk.astype(jnp.int32), rank_out)
    carry_ref[...] = carry_ref[...] + jnp.sum(member.astype(F32), axis=0, keepdims=True)
    idx_ref[...] = idx_out
    gate_ref[...] = gate_out
    rank_ref[...] = rank_out
    cnt_ref[...] = carry_ref[...]


def _router(logits, tm):
    t = logits.shape[0]
    tile = pl.BlockSpec((tm, LANES), lambda i: (i, 0))
    return pl.pallas_call(
        _router_kernel,
        grid=(t // tm,),
        in_specs=[tile],
        out_specs=[tile, tile, tile, pl.BlockSpec((1, LANES), lambda i: (0, 0))],
        out_shape=[jax.ShapeDtypeStruct((t, LANES), jnp.int32),
                   jax.ShapeDtypeStruct((t, LANES), F32),
                   jax.ShapeDtypeStruct((t, LANES), jnp.int32),
                   jax.ShapeDtypeStruct((1, LANES), F32)],
        scratch_shapes=[pltpu.VMEM((1, LANES), F32)],
        compiler_params=_cparams(("arbitrary",)),
        name="router_topk",
    )(logits)


def _dispatch_kernel(pos_ref, h_ref, xs_in_ref, xs_ref, sem):
    del xs_in_ref
    tm = h_ref.shape[0]

    def row_copy(r, p):
        return pltpu.make_async_copy(h_ref.at[pl.ds(r, 1)], xs_ref.at[pl.ds(p, 1)], sem)

    def issue(r, carry):
        for k in range(TOP_K):
            row_copy(r, pos_ref[r * TOP_K + k]).start(priority=k % 2)
        return carry
    lax.fori_loop(0, tm, issue, 0)

    def drain(r, carry):
        for k in range(TOP_K):
            row_copy(r, pos_ref[r * TOP_K + k]).wait()
        return carry
    lax.fori_loop(0, tm, drain, 0)


def _dispatch(h2, pos, n_rows, tm):
    t, d = h2.shape
    xs0 = jnp.zeros((n_rows, d), F32)
    return pl.pallas_call(
        _dispatch_kernel,
        grid=(t // tm,),
        in_specs=[pl.BlockSpec((tm * TOP_K,), lambda i: (i,), memory_space=pltpu.SMEM),
                  pl.BlockSpec((tm, d), lambda i: (i, 0)),
                  pl.BlockSpec(memory_space=pl.ANY)],
        out_specs=pl.BlockSpec(memory_space=pl.ANY),
        out_shape=jax.ShapeDtypeStruct((n_rows, d), F32),
        scratch_shapes=[pltpu.SemaphoreType.DMA(())],
        input_output_aliases={2: 0},
        compiler_params=_cparams(("arbitrary",)),
        name="moe_dispatch",
    )(pos, h2, xs0)


def _ffn_kernel(be_ref, nu_ref, x_ref, wgu_ref, bgu_ref, wdn_ref, bdn_ref, y_ref):
    de = wdn_ref.shape[1]

    @pl.when(pl.program_id(0) < nu_ref[0])
    def _():
        x = x_ref[...].astype(BF16)
        gu = jnp.dot(x, wgu_ref[0], preferred_element_type=F32) + bgu_ref[0]
        gate = jnp.minimum(gu[:, :de], SWIGLU_LIMIT)
        up = jnp.clip(gu[:, de:], -SWIGLU_LIMIT, SWIGLU_LIMIT)
        act = (up + 1.0) * gate * jax.nn.sigmoid(SWIGLU_ALPHA * gate)
        y_ref[...] = jnp.dot(act.astype(BF16), wdn_ref[0], preferred_element_type=F32) + bdn_ref[0]

    @pl.when(pl.program_id(0) >= nu_ref[0])
    def _():
        y_ref[...] = jnp.zeros_like(y_ref)


def _ffn(xs, block_expert, n_used, wgu, bgu, wdn, bdn, tb):
    p, d = xs.shape
    e, _, n2 = wgu.shape
    de = wdn.shape[1]
    grid_spec = pltpu.PrefetchScalarGridSpec(
        num_scalar_prefetch=2,
        grid=(p // tb,),
        in_specs=[pl.BlockSpec((tb, d), lambda i, be, nu: (i, 0)),
                  pl.BlockSpec((1, d, n2), lambda i, be, nu: (be[i], 0, 0)),
                  pl.BlockSpec((1, 1, n2), lambda i, be, nu: (be[i], 0, 0)),
                  pl.BlockSpec((1, de, d), lambda i, be, nu: (be[i], 0, 0)),
                  pl.BlockSpec((1, 1, d), lambda i, be, nu: (be[i], 0, 0))],
        out_specs=pl.BlockSpec((tb, d), lambda i, be, nu: (i, 0)),
    )
    return pl.pallas_call(
        _ffn_kernel,
        grid_spec=grid_spec,
        out_shape=jax.ShapeDtypeStruct((p, d), F32),
        compiler_params=_cparams(("arbitrary",)),
        name="moe_ffn",
    )(block_expert, n_used, xs, wgu, bgu.reshape(e, 1, n2), wdn, bdn.reshape(e, 1, d))


def _combine_kernel(pos_ref, gate_ref, x_ref, g2_ref, lng_ref, lnb_ref, ys_ref, o_ref, buf_ref, sem,
                    *, alpha):
    tm = x_ref.shape[1]

    def row_copy(r, k, p):
        return pltpu.make_async_copy(ys_ref.at[pl.ds(p, 1)], buf_ref.at[k, pl.ds(r, 1)], sem)

    def issue(r, carry):
        for k in range(TOP_K):
            row_copy(r, k, pos_ref[r * TOP_K + k]).start(priority=k % 2)
        return carry
    lax.fori_loop(0, tm, issue, 0)

    def drain(r, carry):
        for k in range(TOP_K):
            row_copy(r, k, pos_ref[r * TOP_K + k]).wait()
        return carry
    lax.fori_loop(0, tm, drain, 0)

    gate = gate_ref[...]
    f = gate[:, 0:1] * buf_ref[0]
    for k in range(1, TOP_K):
        f = f + gate[:, k:k + 1] * buf_ref[k]
    o_ref[0] = _layer_norm(alpha * x_ref[0] + g2_ref[0] * f, lng_ref[...], lnb_ref[...])


def _combine(pos, gates, x, g2, lng, lnb, ys, n_lat_tiles, n_tiles, tm, alpha):
    b, _, d = x.shape
    nb = g2.shape[0] - 1

    def mod_map(i, j):
        return (jnp.where(j < n_lat_tiles, i, nb), 0, 0)

    tok = pl.BlockSpec((1, tm, d), lambda i, j: (i, j, 0))
    vec = pl.BlockSpec((1, d), lambda i, j: (0, 0))
    return pl.pallas_call(
        functools.partial(_combine_kernel, alpha=alpha),
        grid=(b, n_tiles),
        in_specs=[pl.BlockSpec((tm * TOP_K,), lambda i, j: (i * n_tiles + j,), memory_space=pltpu.SMEM),
                  pl.BlockSpec((tm, LANES), lambda i, j: (i * n_tiles + j, 0)),
                  tok, pl.BlockSpec((1, 1, d), mod_map), vec, vec,
                  pl.BlockSpec(memory_space=pl.ANY)],
        out_specs=tok,
        out_shape=jax.ShapeDtypeStruct((b, n_tiles * tm, d), F32),
        scratch_shapes=[pltpu.VMEM((TOP_K, tm, d), F32), pltpu.SemaphoreType.DMA(())],
        compiler_params=_cparams(("arbitrary", "arbitrary")),
        name="moe_combine_ln",
    )(pos, gates, x, g2, lng, lnb, ys)


def _moe_layer(h2, logits, x_res, g2, lng, lnb, wgu, bgu, wdn, bdn, n_lat_tiles, n_tiles, tm, alpha):
    b, rows, d = h2.shape
    t = b * rows
    tb = 512
    idx, gates, rank, counts = _router(logits.reshape(t, LANES), tm)
    counts = counts[0, :N_EXPERTS].astype(jnp.int32)
    padded = (counts + tb - 1) // tb * tb
    pend = jnp.cumsum(padded)
    pstart = pend - padded
    idx4 = idx[:, :TOP_K]
    start_of = jnp.sum(jnp.where(idx4[:, :, None] == jnp.arange(N_EXPERTS)[None, None, :],
                                 pstart[None, None, :], 0), axis=-1)
    pos = (start_of + rank[:, :TOP_K]).reshape(t * TOP_K)
    n_rows = (t * TOP_K + tb - 1) // tb * tb + N_EXPERTS * tb
    n_blocks = n_rows // tb
    block_start = jnp.arange(n_blocks, dtype=jnp.int32) * tb
    block_expert = jnp.minimum(jnp.sum((pend[None, :] <= block_start[:, None]).astype(jnp.int32), axis=1),
                               N_EXPERTS - 1).astype(jnp.int32)
    n_used = (pend[-1:] // tb).astype(jnp.int32)
    xs = _dispatch(h2.reshape(t, d), pos, n_rows, tm)
    ys = _ffn(xs, block_expert, n_used, wgu, bgu, wdn, bdn, tb)
    return _combine(pos, gates, x_res, g2, lng, lnb, ys, n_lat_tiles, n_tiles, tm, alpha)


def _rope_tables(n_lat_rows, n_rows, scale):
    pos = jnp.arange(n_lat_rows)
    lane = jnp.arange(LANES)
    m = lane % 64
    n_freq = 16
    inv = ROPE_BASE ** (-(m % n_freq).astype(F32) / n_freq)
    p = jnp.where((m // 32)[None, :] == 0, (pos // GRID_W)[:, None], (pos % GRID_W)[:, None]).astype(F32)
    ang = p * inv[None, :]
    sign = jnp.where((m % 32) < n_freq, -1.0, 1.0)[None, :]
    cos = jnp.concatenate([jnp.cos(ang), jnp.ones((n_rows - n_lat_rows, LANES), F32)], axis=0)
    sin = jnp.concatenate([jnp.sin(ang) * sign, jnp.zeros((n_rows - n_lat_rows, LANES), F32)], axis=0)
    return cos * scale, sin * scale


def kernel(x, c, ctx, c_ctx, w_ada, b_ada, ln_g, ln_b, hg_w_in, hg_lb, hg_norm_g, hg_w_out, da_w_in, da_lam,
           da_norm_g, da_w_out, moe_w_router, moe_b_router, moe_w_gu, moe_b_gu, moe_w_dn, moe_b_dn):
    bsz, seq, d = x.shape
    lc = ctx.shape[1]
    depth = w_ada.shape[0]
    assert depth == 2 and hg_w_in.shape[0] == 1 and da_w_in.shape[0] == 1
    ltot = seq + lc
    tm = 256
    assert seq % tm == 0 and lc % tm == 0 and seq % GRID_W == 0 and d == DA_HEADS * LANES
    alpha = (2 * depth) ** 0.25
    n_lat_tiles = seq // tm
    n_all_tiles = ltot // tm
    h_heads = d // LANES

    n_mod = bsz + 1
    c_pad = jnp.zeros((16, d), F32).at[:bsz].set(c).at[bsz].set(c_ctx)
    mod = _ada(c_pad, w_ada, b_ada)[:, :n_mod]

    def mods(l):
        return [mod[l, :, k * d:(k + 1) * d].reshape(n_mod, 1, d) for k in range(6)]

    def router_params(l):
        wr = jnp.zeros((d, LANES), F32).at[:, :N_EXPERTS].set(moe_w_router[l])
        br = jnp.full((1, LANES), -1e30, F32).at[0, :N_EXPERTS].set(moe_b_router[l])
        return wr, br

    xall = jnp.concatenate([x, ctx], axis=1)

    sh1, sc1, g1, sh2, sc2, g2 = mods(0)
    p0 = _proj(xall, sc1, sh1, hg_w_in[0].astype(BF16), n_lat_tiles, tm)
    lb = jnp.cumsum(jax.nn.softmax(hg_lb.astype(F32), axis=1), axis=1)[:, 0]
    lb = lb.reshape(2, h_heads, LANES).transpose(1, 0, 2)
    y0 = _gla(p0, lb, hg_norm_g[0].reshape(h_heads, 1, LANES), seq)
    wr, br = router_params(0)
    x0, h20, lg0 = _post(y0, hg_w_out[0].astype(BF16), xall, g1, ln_g[0, 0].reshape(1, d),
                         ln_b[0, 0].reshape(1, d), sc2, sh2, wr, br, n_lat_tiles, n_all_tiles, tm, alpha)
    x1 = _moe_layer(h20, lg0, x0, g2, ln_g[0, 1].reshape(1, d), ln_b[0, 1].reshape(1, d),
                    moe_w_gu[0].astype(BF16), moe_b_gu[0], moe_w_dn[0].astype(BF16), moe_b_dn[0],
                    n_lat_tiles, n_all_tiles, tm, alpha)

    sh1, sc1, g1, sh2, sc2, g2 = mods(1)
    dh = d // DA_HEADS // 2
    cq, sq = _rope_tables(seq, ltot, dh ** -0.5 * math.log2(math.e))
    ck, sk = _rope_tables(seq, ltot, 1.0)
    p1 = _proj(x1, sc1, sh1, da_w_in[0].astype(BF16), n_lat_tiles, tm, rope_tabs=(cq, sq, ck, sk))
    lam_init = 0.8 - 0.6 * math.exp(-0.3 * 1)
    lp = da_lam[0].astype(F32)
    lam = (jnp.exp(jnp.sum(lp[0] * lp[1])) - jnp.exp(jnp.sum(lp[2] * lp[3])) + lam_init).reshape(1)
    y1 = _attn(p1, lam, da_norm_g[0].reshape(1, LANES), seq, 1.0 - lam_init)
    wr, br = router_params(1)
    x2, h21, lg1 = _post(y1, da_w_out[0].astype(BF16), x1, g1, ln_g[1, 0].reshape(1, d),
                         ln_b[1, 0].reshape(1, d), sc2, sh2, wr, br, n_lat_tiles, n_lat_tiles, tm, alpha)
    return _moe_layer(h21, lg1, x2, g2, ln_g[1, 1].reshape(1, d), ln_b[1, 1].reshape(1, d),
                      moe_w_gu[1].astype(BF16), moe_b_gu[1], moe_w_dn[1].astype(BF16), moe_b_dn[1],
                      n_lat_tiles, n_lat_tiles, tm, alpha)
```

```python
import functools
import math

import jax
import jax.numpy as jnp
from jax import lax
from jax.experimental import pallas as pl
from jax.experimental.pallas import tpu as pltpu

F32 = jnp.float32
BF16 = jnp.bfloat16
HIGHEST = lax.Precision.HIGHEST

LANES = 128
HG_CHUNK = 64
GRID_W = 64
ROPE_BASE = 10000.0
DA_HEADS = 8
N_EXPERTS = 32
TOP_K = 4
SEG_ALIGN = 8
SWIGLU_LIMIT = 7.0
SWIGLU_ALPHA = 1.702
LN_EPS = 1e-5
RMS_EPS = 1e-6
VMEM_LIMIT = 56 * 1024 * 1024


def _cparams(sem):
    return pltpu.CompilerParams(dimension_semantics=sem, vmem_limit_bytes=VMEM_LIMIT)


def _ada_kernel(c_ref, w_ref, b_ref, o_ref):
    c = c_ref[...]
    s = c * jax.nn.sigmoid(c)
    o_ref[0] = jnp.dot(s, w_ref[0], precision=HIGHEST, preferred_element_type=F32) + b_ref[0]


def _ada(c_pad, w_ada, b_ada):
    depth, d, n = w_ada.shape
    rows = c_pad.shape[0]
    tn = 1536
    return pl.pallas_call(
        _ada_kernel,
        grid=(depth, n // tn),
        in_specs=[pl.BlockSpec((rows, d), lambda l, j: (0, 0)),
                  pl.BlockSpec((1, d, tn), lambda l, j: (l, 0, j)),
                  pl.BlockSpec((1, 1, tn), lambda l, j: (l, 0, j))],
        out_specs=pl.BlockSpec((1, rows, tn), lambda l, j: (l, 0, j)),
        out_shape=jax.ShapeDtypeStruct((depth, rows, n), F32),
        compiler_params=_cparams(("arbitrary", "arbitrary")),
        name="ada_mod",
    )(c_pad, w_ada, b_ada.reshape(depth, 1, n))


def _rope(piece, cos, sin, lo_mask):
    rot = jnp.where(lo_mask, pltpu.roll(piece, LANES - 16, 1), pltpu.roll(piece, 16, 1))
    return piece * cos + rot * sin


def _proj_kernel(x_ref, sc_ref, sh_ref, w_ref, *rest, n_rope, cw):
    if n_rope:
        cq_ref, sq_ref, ck_ref, sk_ref, o_ref = rest
    else:
        (o_ref,) = rest
    n = w_ref.shape[1]
    h = (x_ref[0] * (1.0 + sc_ref[0]) + sh_ref[0]).astype(BF16)
    if n_rope:
        lane = lax.broadcasted_iota(jnp.int32, (x_ref.shape[1], LANES), 1)
        lo_mask = (lane % 32) < 16
    per = cw // LANES
    for j in range(n // cw):
        r = jnp.dot(h, w_ref[:, j * cw:(j + 1) * cw], preferred_element_type=F32)
        for g in range(per):
            hd = j * per + g
            piece = r[:, g * LANES:(g + 1) * LANES]
            if hd < n_rope:
                piece = _rope(piece, cq_ref[...], sq_ref[...], lo_mask)
            elif hd < 2 * n_rope:
                piece = _rope(piece, ck_ref[...], sk_ref[...], lo_mask)
            o_ref[0, hd] = piece.astype(BF16)


def _proj(x, sc, sh, w, n_lat_tiles, tm, rope_tabs=None):
    b, ltot, d = x.shape
    n = w.shape[1]
    nb = sc.shape[0] - 1
    n_rope = DA_HEADS if rope_tabs is not None else 0

    def mod_map(i, j):
        return (jnp.where(j < n_lat_tiles, i, nb), 0, 0)

    in_specs = [pl.BlockSpec((1, tm, d), lambda i, j: (i, j, 0)),
                pl.BlockSpec((1, 1, d), mod_map),
                pl.BlockSpec((1, 1, d), mod_map),
                pl.BlockSpec((d, n), lambda i, j: (0, 0))]
    args = [x, sc, sh, w]
    if rope_tabs is not None:
        in_specs += [pl.BlockSpec((tm, LANES), lambda i, j: (j, 0))] * 4
        args += list(rope_tabs)
    return pl.pallas_call(
        functools.partial(_proj_kernel, n_rope=n_rope, cw=512),
        grid=(b, ltot // tm),
        in_specs=in_specs,
        out_specs=pl.BlockSpec((1, n // LANES, tm, LANES), lambda i, j: (i, 0, j, 0)),
        out_shape=jax.ShapeDtypeStruct((b, n // LANES, ltot, LANES), BF16),
        compiler_params=_cparams(("arbitrary", "arbitrary")),
        name="mod_proj_rope" if n_rope else "mod_proj",
    )(*args)


GLA_BLOCK = 256


def _gla_local(items):
    nc = GLA_BLOCK // HG_CHUNK
    kks, cats = [], []
    for (_, _, z, lb, _, _, _) in items:
        f = lb + (1.0 - lb) * jax.nn.sigmoid(z)
        logf = jnp.log(f)
        kks.append(1.0 - f)
        hi = logf.astype(BF16)
        lo = (logf - hi.astype(F32)).astype(BF16)
        cats.append(jnp.concatenate([hi, lo], axis=1))
    parts = [jnp.dot(it[4], cat, preferred_element_type=F32) for it, cat in zip(items, cats)]
    q_decs, k_invs, k_ends, decs = [], [], [], []
    for (q, _, _, _, _, _, end_row), kk, part in zip(items, kks, parts):
        bcum = part[:, :LANES] + part[:, LANES:]
        b_end = bcum.reshape(nc, HG_CHUNK, LANES)[:, end_row:end_row + 1, :]
        b_end_rows = jnp.broadcast_to(b_end, (nc, HG_CHUNK, LANES)).reshape(GLA_BLOCK, LANES)
        q_decs.append((q * jnp.exp(bcum)).astype(BF16))
        k_invs.append((kk * jnp.exp(-bcum)).astype(BF16))
        k_ends.append((kk * jnp.exp(b_end_rows - bcum)).astype(BF16))
        decs.append(jnp.exp(b_end.reshape(nc, LANES)))
    scores = [lax.dot_general(qd, ki, (((1,), (1,)), ((), ())), preferred_element_type=F32)
              for qd, ki in zip(q_decs, k_invs)]
    scores = [jnp.where(it[5], a, 0.0).astype(BF16) for it, a in zip(items, scores)]
    outs = [jnp.dot(a, it[1], preferred_element_type=F32) for it, a in zip(items, scores)]
    return list(zip(outs, q_decs, k_ends, decs))


def _gla_kernel(q_ref, i_ref, g_ref, zf_ref, zb_ref, lb_ref, ng_ref, y_ref,
                of_ref, ob_ref, qd_ref, ke_ref, de_ref, st_ref, *, n_lat, n_ctx):
    c = HG_CHUNK
    blk = GLA_BLOCK
    nc = blk // c
    ltot = (n_lat + n_ctx) * c
    row = lax.broadcasted_iota(jnp.int32, (blk, blk), 0)
    col = lax.broadcasted_iota(jnp.int32, (blk, blk), 1)
    same = (row // c) == (col // c)
    masks = (same & (col <= row), same & (col >= row))
    tris = (masks[0].astype(BF16), masks[1].astype(BF16))
    z_refs = (zf_ref, zb_ref)
    o_refs = (of_ref, ob_ref)
    end_rows = (c - 1, 0)

    n_blk = ltot // blk
    per_step = 3 if n_blk % 3 == 0 else 1

    def local(t, carry):
        work = []
        for u in range(per_step):
            r0 = pl.multiple_of((t * per_step + u) * blk, blk)
            c0 = pl.multiple_of((t * per_step + u) * nc, nc)
            q = q_ref[0, 0, pl.ds(r0, blk), :].astype(F32)
            v = i_ref[0, 0, pl.ds(r0, blk), :]
            zs = [z_refs[d][0, 0, pl.ds(r0, blk), :].astype(F32) for d in range(2)]
            work.append((r0, c0, q, v, zs))
        res = _gla_local([(q, v, zs[d], lb_ref[0, d:d + 1, :], tris[d], masks[d], end_rows[d])
                          for (_, _, q, v, zs) in work for d in range(2)])
        for u, (r0, c0, _, _, _) in enumerate(work):
            for d in range(2):
                o, q_dec, k_end, dec = res[2 * u + d]
                o_refs[d][pl.ds(r0, blk), :] = o
                qd_ref[d, pl.ds(r0, blk), :] = q_dec
                ke_ref[d, pl.ds(r0, blk), :] = k_end
                de_ref[d, pl.ds(c0, nc), :] = dec
        return carry
    lax.fori_loop(0, n_blk // per_step, local, 0)

    st_ref[...] = jnp.zeros_like(st_ref)

    def segment(first, n):
        per = 4 if n % 4 == 0 else 1

        def body(jj, carry):
            todo = []
            for u in range(per):
                j = jj * per + u
                for d, ch in enumerate((first + j, first + n - 1 - j)):
                    r0 = pl.multiple_of(ch * c, c)
                    kv = lax.dot_general(i_ref[0, 0, pl.ds(r0, c), :], ke_ref[d, pl.ds(r0, c), :],
                                         (((0,), (0,)), ((), ())), preferred_element_type=F32)
                    todo.append((d, r0, kv, de_ref[d, pl.ds(ch, 1), :], qd_ref[d, pl.ds(r0, c), :]))
            st = [st_ref[0], st_ref[1]]
            inter = []
            for (d, r0, kv, dec, q_dec) in todo:
                inter.append((d, r0, lax.dot_general(q_dec, st[d].astype(BF16), (((1,), (1,)), ((), ())),
                                                     preferred_element_type=F32)))
                st[d] = st[d] * dec + kv
            for (d, r0, o) in inter:
                o_refs[d][pl.ds(r0, c), :] += o
            st_ref[0] = st[0]
            st_ref[1] = st[1]
            return carry
        lax.fori_loop(0, n // per, body, 0)

    segment(n_lat, n_ctx)
    segment(0, n_lat)

    ng = ng_ref[0]
    rt = 256

    def readout(t, carry):
        r0 = pl.multiple_of(t * rt, rt)
        o = of_ref[pl.ds(r0, rt), :] + ob_ref[pl.ds(r0, rt), :]
        ms = jnp.mean(o * o, axis=-1, keepdims=True)
        g = g_ref[0, 0, pl.ds(r0, rt), :].astype(F32)
        y = o * lax.rsqrt(ms + RMS_EPS) * ng * (g * jax.nn.sigmoid(g))
        y_ref[0, 0, pl.ds(r0, rt), :] = y.astype(BF16)
        return carry
    lax.fori_loop(0, ltot // rt, readout, 0)


def _gla(p, lb, ng, n_lat_rows):
    b, nh5, ltot, _ = p.shape
    h = nh5 // 5
    n_lat = n_lat_rows // HG_CHUNK
    n_ctx = (ltot - n_lat_rows) // HG_CHUNK

    def spec(k):
        return pl.BlockSpec((1, 1, ltot, LANES), lambda i, j, k=k: (i, k * h + j, 0, 0))

    return pl.pallas_call(
        functools.partial(_gla_kernel, n_lat=n_lat, n_ctx=n_ctx),
        grid=(b, h),
        in_specs=[spec(0), spec(1), spec(2), spec(3), spec(4),
                  pl.BlockSpec((1, 2, LANES), lambda i, j: (j, 0, 0)),
                  pl.BlockSpec((1, 1, LANES), lambda i, j: (j, 0, 0))],
        out_specs=pl.BlockSpec((1, 1, ltot, LANES), lambda i, j: (i, j, 0, 0)),
        out_shape=jax.ShapeDtypeStruct((b, h, ltot, LANES), BF16),
        scratch_shapes=[pltpu.VMEM((ltot, LANES), F32), pltpu.VMEM((ltot, LANES), F32),
                        pltpu.VMEM((2, ltot, LANES), BF16), pltpu.VMEM((2, ltot, LANES), BF16),
                        pltpu.VMEM((2, ltot // HG_CHUNK, LANES), F32),
                        pltpu.VMEM((2, LANES, LANES), F32)],
        compiler_params=_cparams(("arbitrary", "arbitrary")),
        name="hgrn2_gla",
    )(p, p, p, p, p, lb, ng)


def _attn_kernel(lam_ref, q_ref, k_ref, v_ref, ng_ref, y_ref, *, tk, out_scale):
    tq = q_ref.shape[2]
    lk = k_ref.shape[2]
    lam = lam_ref[0]
    q = q_ref[0, 0]
    lane = lax.broadcasted_iota(jnp.int32, (tq, LANES), 1)
    zero = jnp.zeros_like(q)
    qm = (jnp.where(lane < 64, q, zero), jnp.where(lane >= 64, q, zero))
    ones = jnp.ones((tk, LANES), BF16)

    def scores(ci):
        kc = k_ref[0, 0, ci * tk:(ci + 1) * tk, :]
        return [lax.dot_general(qm[c], kc, (((1,), (1,)), ((), ())), preferred_element_type=F32)
                for c in range(2)]

    n_chunks = lk // tk
    m = [jnp.full((tq, 1), -1e30, F32) for _ in range(2)]
    acc = [jnp.zeros((tq, 2 * LANES), F32) for _ in range(2)]
    s_next = scores(0)
    for ci in range(n_chunks):
        s_cur = s_next
        if ci + 1 < n_chunks:
            s_next = scores(ci + 1)
        vext = jnp.concatenate([v_ref[0, 0, ci * tk:(ci + 1) * tk, :], ones], axis=1)
        for c in range(2):
            m_new = jnp.maximum(m[c], jnp.max(s_cur[c], axis=-1, keepdims=True))
            p = jnp.exp2(s_cur[c] - m_new).astype(BF16)
            acc[c] = jnp.exp2(m[c] - m_new) * acc[c] + jnp.dot(p, vext, preferred_element_type=F32)
            m[c] = m_new
    a0, a1 = acc
    o = a0[:, :LANES] / a0[:, LANES:LANES + 1] - lam * (a1[:, :LANES] / a1[:, LANES:LANES + 1])
    ms = jnp.mean(o * o, axis=-1, keepdims=True)
    y_ref[0, 0] = (o * lax.rsqrt(ms + RMS_EPS) * ng_ref[...] * out_scale).astype(BF16)


def _attn(p, lam, ng, n_lat_rows, out_scale, tq=256, tk=768):
    b, nh3, ltot, _ = p.shape
    h = nh3 // 3
    assert ltot % tk == 0 and n_lat_rows % tq == 0
    return pl.pallas_call(
        functools.partial(_attn_kernel, tk=tk, out_scale=out_scale),
        grid=(b, h, n_lat_rows // tq),
        in_specs=[pl.BlockSpec(memory_space=pltpu.SMEM),
                  pl.BlockSpec((1, 1, tq, LANES), lambda i, j, t: (i, j, t, 0)),
                  pl.BlockSpec((1, 1, ltot, LANES), lambda i, j, t: (i, h + j, 0, 0)),
                  pl.BlockSpec((1, 1, ltot, LANES), lambda i, j, t: (i, 2 * h + j, 0, 0)),
                  pl.BlockSpec((1, LANES), lambda i, j, t: (0, 0))],
        out_specs=pl.BlockSpec((1, 1, tq, LANES), lambda i, j, t: (i, j, t, 0)),
        out_shape=jax.ShapeDtypeStruct((b, h, n_lat_rows, LANES), BF16),
        compiler_params=_cparams(("arbitrary", "arbitrary", "arbitrary")),
        name="diff_attn",
    )(lam, p, p, p, ng)


def _layer_norm(x, g, b):
    mu = jnp.mean(x, axis=-1, keepdims=True)
    xc = x - mu
    var = jnp.mean(xc * xc, axis=-1, keepdims=True)
    return xc * lax.rsqrt(var + LN_EPS) * g + b


def _post_kernel(y_ref, w_ref, x_ref, g1_ref, lng_ref, lnb_ref, sc2_ref, sh2_ref, wr_ref, br_ref,
                 xo_ref, h2_ref, lg_ref, *, alpha):
    nh = y_ref.shape[1]
    y = jnp.concatenate([y_ref[0, k] for k in range(nh)], axis=-1)
    m = jnp.dot(y, w_ref[...], preferred_element_type=F32)
    xl = _layer_norm(alpha * x_ref[0] + g1_ref[0] * m, lng_ref[...], lnb_ref[...])
    xo_ref[0] = xl
    h2 = xl * (1.0 + sc2_ref[0]) + sh2_ref[0]
    h2_ref[0] = h2
    lg_ref[0] = jnp.dot(h2, wr_ref[...], precision=HIGHEST, preferred_element_type=F32) + br_ref[...]


def _post(y, w_out, x, g1, lng, lnb, sc2, sh2, wr, br, n_lat_tiles, n_tiles, tm, alpha):
    b, nh, _, _ = y.shape
    d = x.shape[2]
    nb = g1.shape[0] - 1
    rows = n_tiles * tm

    def mod_map(i, j):
        return (jnp.where(j < n_lat_tiles, i, nb), 0, 0)

    tok = pl.BlockSpec((1, tm, d), lambda i, j: (i, j, 0))
    vec = pl.BlockSpec((1, d), lambda i, j: (0, 0))
    return pl.pallas_call(
        functools.partial(_post_kernel, alpha=alpha),
        grid=(b, n_tiles),
        in_specs=[pl.BlockSpec((1, nh, tm, LANES), lambda i, j: (i, 0, j, 0)),
                  pl.BlockSpec((d, d), lambda i, j: (0, 0)),
                  tok, pl.BlockSpec((1, 1, d), mod_map), vec, vec,
                  pl.BlockSpec((1, 1, d), mod_map), pl.BlockSpec((1, 1, d), mod_map),
                  pl.BlockSpec((d, LANES), lambda i, j: (0, 0)),
                  pl.BlockSpec((1, LANES), lambda i, j: (0, 0))],
        out_specs=[tok, tok, pl.BlockSpec((1, tm, LANES), lambda i, j: (i, j, 0))],
        out_shape=[jax.ShapeDtypeStruct((b, rows, d), F32),
                   jax.ShapeDtypeStruct((b, rows, d), F32),
                   jax.ShapeDtypeStruct((b, rows, LANES), F32)],
        compiler_params=_cparams(("arbitrary", "arbitrary")),
        name="out_proj_ln",
    )(y, w_out, x, g1, lng, lnb, sc2, sh2, wr, br)


def _router_kernel(lg_ref, idx_ref, gate_ref, lpos_ref, lpost_ref, tile_ref, cnt_ref, carry_ref):
    tm = lg_ref.shape[0]

    @pl.when(pl.program_id(0) == 0)
    def _():
        carry_ref[...] = jnp.zeros_like(carry_ref)

    work = lg_ref[...]
    lane = lax.broadcasted_iota(jnp.int32, (tm, LANES), 1)
    lane_f = lane.astype(F32)
    vals, onehots = [], []
    idx_out = jnp.zeros((tm, LANES), jnp.int32)
    for k in range(TOP_K):
        mx = jnp.max(work, axis=-1, keepdims=True)
        am = jnp.min(jnp.where(work == mx, lane_f, float(LANES)), axis=-1, keepdims=True)
        oh = lane_f == am
        vals.append(mx)
        onehots.append(oh)
        idx_out = jnp.where(lane == k, am.astype(jnp.int32), idx_out)
        work = jnp.where(oh, -3e38, work)
    es = [jnp.exp(v - vals[0]) for v in vals]
    denom = es[0] + es[1] + es[2] + es[3]
    gate_out = jnp.zeros((tm, LANES), F32)
    for k in range(TOP_K):
        gate_out = jnp.where(lane == k, es[k] / denom, gate_out)
    member = (onehots[0] | onehots[1] | onehots[2] | onehots[3])
    r = lax.broadcasted_iota(jnp.int32, (tm, tm), 0)
    c = lax.broadcasted_iota(jnp.int32, (tm, tm), 1)
    strict = (c < r).astype(BF16)
    before = jnp.dot(strict, member.astype(BF16), preferred_element_type=F32)
    cnt_tile = jnp.sum(member.astype(F32), axis=0, keepdims=True)
    cnt_tile = jnp.floor((cnt_tile + (SEG_ALIGN - 1)) * (1.0 / SEG_ALIGN)) * SEG_ALIGN
    ei = lax.broadcasted_iota(jnp.int32, (LANES, LANES), 0)
    ej = lax.broadcasted_iota(jnp.int32, (LANES, LANES), 1)
    seg_start = jnp.dot(jnp.broadcast_to(cnt_tile, (8, LANES)).astype(BF16), (ei < ej).astype(BF16),
                        preferred_element_type=F32)[0:1]
    slot = seg_start + before
    lpos_out = jnp.zeros((tm, LANES), jnp.int32)
    for k in range(TOP_K):
        lp = jnp.sum(jnp.where(onehots[k], slot, 0.0), axis=-1, keepdims=True)
        lpos_out = jnp.where(lane == k, lp.astype(jnp.int32), lpos_out)
    idx_ref[...] = idx_out
    gate_ref[...] = gate_out
    lpos_ref[...] = lpos_out
    lpost_ref[...] = jnp.transpose(lpos_out.astype(F32))[0:8, :].astype(jnp.int32)
    row8 = lax.broadcasted_iota(jnp.int32, (8, LANES), 0)
    tile_ref[...] = jnp.where(row8 == 0, carry_ref[...], jnp.where(row8 == 1, cnt_tile, 0.0))
    carry_ref[...] = carry_ref[...] + cnt_tile
    cnt_ref[...] = carry_ref[...]


def _router(logits, tm):
    t = logits.shape[0]
    tile = pl.BlockSpec((tm, LANES), lambda i: (i, 0))
    return pl.pallas_call(
        _router_kernel,
        grid=(t // tm,),
        in_specs=[tile],
        out_specs=[tile, tile, tile, pl.BlockSpec((8, tm), lambda i: (0, i)),
                   pl.BlockSpec((8, LANES), lambda i: (i, 0)), pl.BlockSpec((1, LANES), lambda i: (0, 0))],
        out_shape=[jax.ShapeDtypeStruct((t, LANES), jnp.int32),
                   jax.ShapeDtypeStruct((t, LANES), F32),
                   jax.ShapeDtypeStruct((t, LANES), jnp.int32),
                   jax.ShapeDtypeStruct((8, t), jnp.int32),
                   jax.ShapeDtypeStruct((t // tm * 8, LANES), F32),
                   jax.ShapeDtypeStruct((1, LANES), F32)],
        scratch_shapes=[pltpu.VMEM((1, LANES), F32)],
        compiler_params=_cparams(("arbitrary",)),
        name="router_topk",
    )(logits)


HI_MASK = 0xFFFF0000


def _pack_rows(x):
    half = x.shape[1] // 2
    bits = lax.bitcast_convert_type(x.astype(BF16).astype(F32), jnp.uint32)
    return (bits[:, :half] & jnp.uint32(HI_MASK)) | (bits[:, half:] >> 16)


def _unpack_rows(p):
    left = lax.bitcast_convert_type(p & jnp.uint32(HI_MASK), F32).astype(BF16)
    right = lax.bitcast_convert_type(p << 16, F32).astype(BF16)
    return left, right


def _copy_pieces(n, local_row, global_row, copy, max_rows):
    for b in range(SEG_ALIGN.bit_length() - 1, max_rows.bit_length()):
        size = 1 << b

        @pl.when((n & size) != 0)
        def _():
            lo = n & (size - 1)
            copy(pl.multiple_of(local_row + lo, SEG_ALIGN), pl.multiple_of(global_row + lo, SEG_ALIGN),
                 size).start()


def _segment_copies(base_ref, cnt_ref, copy, max_rows, n_slots, fill_row):
    def per_expert(e, off):
        _copy_pieces(cnt_ref[e], off, base_ref[e], copy, max_rows)
        return off + cnt_ref[e]
    used = lax.fori_loop(0, N_EXPERTS, per_expert, 0)
    _copy_pieces(n_slots - used, used, fill_row, copy, max_rows)


def _tile_slots(tm):
    return tm * TOP_K + N_EXPERTS * SEG_ALIGN


def _dispatch_kernel(base_ref, cnt_ref, lpost_ref, h_ref, xs_in_ref, xs_ref, cbuf_ref, sem, *, spare_row):
    del xs_in_ref
    tm = h_ref.shape[0]
    n_slots = _tile_slots(tm)
    slot = lax.broadcasted_iota(jnp.int32, (n_slots, tm), 0)
    lp = lpost_ref[...]
    sel = slot == lp[0:1, :]
    for k in range(1, TOP_K):
        sel = sel | (slot == lp[k:k + 1, :])
    rows = jnp.dot(sel.astype(BF16), h_ref[...].astype(BF16), preferred_element_type=F32)
    cbuf_ref[...] = _pack_rows(rows)

    def copy(src, dst, size):
        return pltpu.make_async_copy(cbuf_ref.at[pl.ds(src, size)], xs_ref.at[pl.ds(dst, size)], sem)
    _segment_copies(base_ref, cnt_ref, copy, tm, n_slots, spare_row)
    copy(0, 0, n_slots).wait()


def _dispatch(h2, base, cnt, lpost, n_rows, tm):
    t, d = h2.shape
    xs0 = jnp.zeros((n_rows + tm, d // 2), jnp.uint32)
    seg = pl.BlockSpec((LANES,), lambda i: (i,), memory_space=pltpu.SMEM)
    return pl.pallas_call(
        functools.partial(_dispatch_kernel, spare_row=n_rows),
        grid=(t // tm,),
        in_specs=[seg, seg,
                  pl.BlockSpec((8, tm), lambda i: (0, i)),
                  pl.BlockSpec((tm, d), lambda i: (i, 0)),
                  pl.BlockSpec(memory_space=pl.ANY)],
        out_specs=pl.BlockSpec(memory_space=pl.ANY),
        out_shape=jax.ShapeDtypeStruct((n_rows + tm, d // 2), jnp.uint32),
        scratch_shapes=[pltpu.VMEM((_tile_slots(tm), d // 2), jnp.uint32), pltpu.SemaphoreType.DMA(())],
        input_output_aliases={4: 0},
        compiler_params=_cparams(("arbitrary",)),
        name="moe_dispatch",
    )(base, cnt, lpost, h2, xs0)


def _ffn_kernel(be_ref, nu_ref, x_ref, wgu_ref, bgu_ref, wdn_ref, bdn_ref, y_ref):
    de = wdn_ref.shape[1]

    @pl.when(pl.program_id(0) < nu_ref[0])
    def _():
        half = x_ref.shape[1]
        x_left, x_right = _unpack_rows(x_ref[...])
        gu = (jnp.dot(x_left, wgu_ref[0, :half, :], preferred_element_type=F32)
              + jnp.dot(x_right, wgu_ref[0, half:, :], preferred_element_type=F32) + bgu_ref[0])
        gate = jnp.minimum(gu[:, :de], SWIGLU_LIMIT)
        up = jnp.clip(gu[:, de:], -SWIGLU_LIMIT, SWIGLU_LIMIT)
        act = (up + 1.0) * gate * jax.nn.sigmoid(SWIGLU_ALPHA * gate)
        y = jnp.dot(act.astype(BF16), wdn_ref[0], preferred_element_type=F32) + bdn_ref[0]
        y_ref[...] = _pack_rows(y)

    @pl.when(pl.program_id(0) >= nu_ref[0])
    def _():
        y_ref[...] = jnp.zeros_like(y_ref)


def _ffn(xs, n_rows, block_expert, n_used, wgu, bgu, wdn, bdn, tb):
    p, half = n_rows, xs.shape[1]
    d = 2 * half
    e, _, n2 = wgu.shape
    de = wdn.shape[1]
    grid_spec = pltpu.PrefetchScalarGridSpec(
        num_scalar_prefetch=2,
        grid=(p // tb,),
        in_specs=[pl.BlockSpec((tb, half), lambda i, be, nu: (i, 0)),
                  pl.BlockSpec((1, d, n2), lambda i, be, nu: (be[i], 0, 0)),
                  pl.BlockSpec((1, 1, n2), lambda i, be, nu: (be[i], 0, 0)),
                  pl.BlockSpec((1, de, d), lambda i, be, nu: (be[i], 0, 0)),
                  pl.BlockSpec((1, 1, d), lambda i, be, nu: (be[i], 0, 0))],
        out_specs=pl.BlockSpec((tb, half), lambda i, be, nu: (i, 0)),
    )
    return pl.pallas_call(
        _ffn_kernel,
        grid_spec=grid_spec,
        out_shape=jax.ShapeDtypeStruct((p, half), jnp.uint32),
        compiler_params=_cparams(("arbitrary",)),
        name="moe_ffn",
    )(block_expert, n_used, xs, wgu, bgu.reshape(e, 1, n2), wdn, bdn.reshape(e, 1, d))


def _combine_kernel(base_ref, cnt_ref, lpos_ref, gate_ref, x_ref, g2_ref, lng_ref, lnb_ref, ys_ref, o_ref,
                    gbuf_ref, sem, *, alpha):
    tm = x_ref.shape[1]
    n_slots = _tile_slots(tm)

    def copy(dst, src, size):
        return pltpu.make_async_copy(ys_ref.at[pl.ds(src, size)], gbuf_ref.at[pl.ds(dst, size)], sem)
    _segment_copies(base_ref, cnt_ref, copy, tm, n_slots, 0)

    lane = lax.broadcasted_iota(jnp.int32, (tm, n_slots), 1)
    lpos = lpos_ref[...]
    gate = gate_ref[...]
    w = jnp.zeros((tm, n_slots), F32)
    for k in range(TOP_K):
        w = jnp.where(lane == lpos[:, k:k + 1], gate[:, k:k + 1], w)
    w_hi = w.astype(BF16)
    w_lo = (w - w_hi.astype(F32)).astype(BF16)

    copy(0, 0, n_slots).wait()
    y_left, y_right = _unpack_rows(gbuf_ref[...])
    f = jnp.concatenate(
        [jnp.dot(w_hi, y, preferred_element_type=F32) + jnp.dot(w_lo, y, preferred_element_type=F32)
         for y in (y_left, y_right)], axis=1)
    o_ref[0] = _layer_norm(alpha * x_ref[0] + g2_ref[0] * f, lng_ref[...], lnb_ref[...])


def _combine(base, cnt, lpos, gates, x, g2, lng, lnb, ys, n_lat_tiles, n_tiles, tm, alpha):
    b, _, d = x.shape
    nb = g2.shape[0] - 1

    def mod_map(i, j):
        return (jnp.where(j < n_lat_tiles, i, nb), 0, 0)

    tok = pl.BlockSpec((1, tm, d), lambda i, j: (i, j, 0))
    vec = pl.BlockSpec((1, d), lambda i, j: (0, 0))
    seg = pl.BlockSpec((LANES,), lambda i, j: (i * n_tiles + j,), memory_space=pltpu.SMEM)
    per_tok = pl.BlockSpec((tm, LANES), lambda i, j: (i * n_tiles + j, 0))
    return pl.pallas_call(
        functools.partial(_combine_kernel, alpha=alpha),
        grid=(b, n_tiles),
        in_specs=[seg, seg, per_tok, per_tok,
                  tok, pl.BlockSpec((1, 1, d), mod_map), vec, vec,
                  pl.BlockSpec(memory_space=pl.ANY)],
        out_specs=tok,
        out_shape=jax.ShapeDtypeStruct((b, n_tiles * tm, d), F32),
        scratch_shapes=[pltpu.VMEM((_tile_slots(tm), d // 2), jnp.uint32), pltpu.SemaphoreType.DMA(())],
        compiler_params=_cparams(("arbitrary", "arbitrary")),
        name="moe_combine_ln",
    )(base, cnt, lpos, gates, x, g2, lng, lnb, ys)


def _moe_layer(h2, logits, x_res, g2, lng, lnb, wgu, bgu, wdn, bdn, n_lat_tiles, n_tiles, tm, alpha):
    b, rows, d = h2.shape
    t = b * rows
    tb = 512
    _, gates, lpos, lpost, tiles, counts = _router(logits.reshape(t, LANES), tm)
    counts = counts[0, :N_EXPERTS].astype(jnp.int32)
    padded = (counts + tb - 1) // tb * tb
    pend = jnp.cumsum(padded)
    pstart = pend - padded
    tiles = tiles.reshape(t // tm, 8, LANES)[:, :2, :].astype(jnp.int32)
    seg_base = (jnp.pad(pstart, (0, LANES - N_EXPERTS))[None, :] + tiles[:, 0]).reshape(-1)
    seg_cnt = tiles[:, 1].reshape(-1)
    n_rows = t * TOP_K + (t // tm) * N_EXPERTS * (SEG_ALIGN - 1)
    n_rows = (n_rows + tb - 1) // tb * tb + N_EXPERTS * tb
    n_blocks = n_rows // tb
    block_start = jnp.arange(n_blocks, dtype=jnp.int32) * tb
    block_expert = jnp.minimum(jnp.sum((pend[None, :] <= block_start[:, None]).astype(jnp.int32), axis=1),
                               N_EXPERTS - 1).astype(jnp.int32)
    n_used = (pend[-1:] // tb).astype(jnp.int32)
    xs = _dispatch(h2.reshape(t, d), seg_base, seg_cnt, lpost, n_rows, tm)
    ys = _ffn(xs, n_rows, block_expert, n_used, wgu, bgu, wdn, bdn, tb)
    return _combine(seg_base, seg_cnt, lpos, gates, x_res, g2, lng, lnb, ys, n_lat_tiles, n_tiles, tm, alpha)


def _rope_tables(n_lat_rows, n_rows, scale):
    pos = jnp.arange(n_lat_rows)
    lane = jnp.arange(LANES)
    m = lane % 64
    n_freq = 16
    inv = ROPE_BASE ** (-(m % n_freq).astype(F32) / n_freq)
    p = jnp.where((m // 32)[None, :] == 0, (pos // GRID_W)[:, None], (pos % GRID_W)[:, None]).astype(F32)
    ang = p * inv[None, :]
    sign = jnp.where((m % 32) < n_freq, -1.0, 1.0)[None, :]
    cos = jnp.concatenate([jnp.cos(ang), jnp.ones((n_rows - n_lat_rows, LANES), F32)], axis=0)
    sin = jnp.concatenate([jnp.sin(ang) * sign, jnp.zeros((n_rows - n_lat_rows, LANES), F32)], axis=0)
    return cos * scale, sin * scale


def kernel(x, c, ctx, c_ctx, w_ada, b_ada, ln_g, ln_b, hg_w_in, hg_lb, hg_norm_g, hg_w_out, da_w_in, da_lam,
           da_norm_g, da_w_out, moe_w_router, moe_b_router, moe_w_gu, moe_b_gu, moe_w_dn, moe_b_dn):
    bsz, seq, d = x.shape
    lc = ctx.shape[1]
    depth = w_ada.shape[0]
    assert depth == 2 and hg_w_in.shape[0] == 1 and da_w_in.shape[0] == 1
    ltot = seq + lc
    tm = 256
    assert seq % tm == 0 and lc % tm == 0 and seq % GRID_W == 0 and d == DA_HEADS * LANES
    alpha = (2 * depth) ** 0.25
    n_lat_tiles = seq // tm
    n_all_tiles = ltot // tm
    h_heads = d // LANES

    n_mod = bsz + 1
    c_pad = jnp.zeros((16, d), F32).at[:bsz].set(c).at[bsz].set(c_ctx)
    mod = _ada(c_pad, w_ada, b_ada)[:, :n_mod]

    def mods(l):
        return [mod[l, :, k * d:(k + 1) * d].reshape(n_mod, 1, d) for k in range(6)]

    def router_params(l):
        wr = jnp.zeros((d, LANES), F32).at[:, :N_EXPERTS].set(moe_w_router[l])
        br = jnp.full((1, LANES), -1e30, F32).at[0, :N_EXPERTS].set(moe_b_router[l])
        return wr, br

    xall = jnp.concatenate([x, ctx], axis=1)

    sh1, sc1, g1, sh2, sc2, g2 = mods(0)
    p0 = _proj(xall, sc1, sh1, hg_w_in[0].astype(BF16), n_lat_tiles, tm)
    lb = jnp.cumsum(jax.nn.softmax(hg_lb.astype(F32), axis=1), axis=1)[:, 0]
    lb = lb.reshape(2, h_heads, LANES).transpose(1, 0, 2)
    y0 = _gla(p0, lb, hg_norm_g[0].reshape(h_heads, 1, LANES), seq)
    wr, br = router_params(0)
    x0, h20, lg0 = _post(y0, hg_w_out[0].astype(BF16), xall, g1, ln_g[0, 0].reshape(1, d),
                         ln_b[0, 0].reshape(1, d), sc2, sh2, wr, br, n_lat_tiles, n_all_tiles, tm, alpha)
    x1 = _moe_layer(h20, lg0, x0, g2, ln_g[0, 1].reshape(1, d), ln_b[0, 1].reshape(1, d),
                    moe_w_gu[0].astype(BF16), moe_b_gu[0], moe_w_dn[0].astype(BF16), moe_b_dn[0],
                    n_lat_tiles, n_all_tiles, tm, alpha)

    sh1, sc1, g1, sh2, sc2, g2 = mods(1)
    dh = d // DA_HEADS // 2
    cq, sq = _rope_tables(seq, ltot, dh ** -0.5 * math.log2(math.e))
    ck, sk = _rope_tables(seq, ltot, 1.0)
    p1 = _proj(x1, sc1, sh1, da_w_in[0].astype(BF16), n_lat_tiles, tm, rope_tabs=(cq, sq, ck, sk))
    lam_init = 0.8 - 0.6 * math.exp(-0.3 * 1)
    lp = da_lam[0].astype(F32)
    lam = (jnp.exp(jnp.sum(lp[0] * lp[1])) - jnp.exp(jnp.sum(lp[2] * lp[3])) + lam_init).reshape(1)
    y1 = _attn(p1, lam, da_norm_g[0].reshape(1, LANES), seq, 1.0 - lam_init)
    wr, br = router_params(1)
    x2, h21, lg1 = _post(y1, da_w_out[0].astype(BF16), x1, g1, ln_g[1, 0].reshape(1, d),
                         ln_b[1, 0].reshape(1, d), sc2, sh2, wr, br, n_lat_tiles, n_lat_tiles, tm, alpha)
    return _moe_layer(h21, lg1, x2, g2, ln_g[1, 1].reshape(1, d), ln_b[1, 1].reshape(1, d),
                      moe_w_gu[1].astype(BF16), moe_b_gu[1], moe_w_dn[1].astype(BF16), moe_b_dn[1],
                      n_lat_tiles, n_lat_tiles, tm, alpha)
```

```python
import functools
import math

import jax
import jax.numpy as jnp
from jax import lax
from jax.experimental import pallas as pl
from jax.experimental.pallas import tpu as pltpu

F32 = jnp.float32
BF16 = jnp.bfloat16
HIGHEST = lax.Precision.HIGHEST

LANES = 128
HG_CHUNK = 64
GRID_W = 64
ROPE_BASE = 10000.0
DA_HEADS = 8
N_EXPERTS = 32
TOP_K = 4
SEG_ALIGN = 8
SWIGLU_LIMIT = 7.0
SWIGLU_ALPHA = 1.702
LN_EPS = 1e-5
RMS_EPS = 1e-6
VMEM_LIMIT = 56 * 1024 * 1024


def _cparams(sem):
    return pltpu.CompilerParams(dimension_semantics=sem, vmem_limit_bytes=VMEM_LIMIT)


def _ada_kernel(c_ref, w_ref, b_ref, o_ref):
    c = c_ref[...]
    s = c * jax.nn.sigmoid(c)
    o_ref[0] = jnp.dot(s, w_ref[0], precision=HIGHEST, preferred_element_type=F32) + b_ref[0]


def _ada(c_pad, w_ada, b_ada):
    depth, d, n = w_ada.shape
    rows = c_pad.shape[0]
    tn = 1536
    return pl.pallas_call(
        _ada_kernel,
        grid=(depth, n // tn),
        in_specs=[pl.BlockSpec((rows, d), lambda l, j: (0, 0)),
                  pl.BlockSpec((1, d, tn), lambda l, j: (l, 0, j)),
                  pl.BlockSpec((1, 1, tn), lambda l, j: (l, 0, j))],
        out_specs=pl.BlockSpec((1, rows, tn), lambda l, j: (l, 0, j)),
        out_shape=jax.ShapeDtypeStruct((depth, rows, n), F32),
        compiler_params=_cparams(("arbitrary", "arbitrary")),
        name="ada_mod",
    )(c_pad, w_ada, b_ada.reshape(depth, 1, n))


def _rope(piece, cos, sin, lo_mask):
    rot = jnp.where(lo_mask, pltpu.roll(piece, LANES - 16, 1), pltpu.roll(piece, 16, 1))
    return piece * cos + rot * sin


def _proj_kernel(x_ref, sc_ref, sh_ref, w_ref, *rest, n_rope, cw):
    if n_rope:
        cq_ref, sq_ref, ck_ref, sk_ref, o_ref = rest
    else:
        (o_ref,) = rest
    n = w_ref.shape[1]
    h = (x_ref[0] * (1.0 + sc_ref[0]) + sh_ref[0]).astype(BF16)
    if n_rope:
        lane = lax.broadcasted_iota(jnp.int32, (x_ref.shape[1], LANES), 1)
        lo_mask = (lane % 32) < 16
    per = cw // LANES
    for j in range(n // cw):
        r = jnp.dot(h, w_ref[:, j * cw:(j + 1) * cw], preferred_element_type=F32)
        for g in range(per):
            hd = j * per + g
            piece = r[:, g * LANES:(g + 1) * LANES]
            if hd < n_rope:
                piece = _rope(piece, cq_ref[...], sq_ref[...], lo_mask)
            elif hd < 2 * n_rope:
                piece = _rope(piece, ck_ref[...], sk_ref[...], lo_mask)
            o_ref[0, hd] = piece.astype(BF16)


def _proj(x, sc, sh, w, n_lat_tiles, tm, rope_tabs=None):
    b, ltot, d = x.shape
    n = w.shape[1]
    nb = sc.shape[0] - 1
    n_rope = DA_HEADS if rope_tabs is not None else 0

    def mod_map(i, j):
        return (jnp.where(j < n_lat_tiles, i, nb), 0, 0)

    in_specs = [pl.BlockSpec((1, tm, d), lambda i, j: (i, j, 0)),
                pl.BlockSpec((1, 1, d), mod_map),
                pl.BlockSpec((1, 1, d), mod_map),
                pl.BlockSpec((d, n), lambda i, j: (0, 0))]
    args = [x, sc, sh, w]
    if rope_tabs is not None:
        in_specs += [pl.BlockSpec((tm, LANES), lambda i, j: (j, 0))] * 4
        args += list(rope_tabs)
    return pl.pallas_call(
        functools.partial(_proj_kernel, n_rope=n_rope, cw=512),
        grid=(b, ltot // tm),
        in_specs=in_specs,
        out_specs=pl.BlockSpec((1, n // LANES, tm, LANES), lambda i, j: (i, 0, j, 0)),
        out_shape=jax.ShapeDtypeStruct((b, n // LANES, ltot, LANES), BF16),
        compiler_params=_cparams(("arbitrary", "arbitrary")),
        name="mod_proj_rope" if n_rope else "mod_proj",
    )(*args)


GLA_BLOCK = 256


def _gla_local(items):
    nc = GLA_BLOCK // HG_CHUNK
    kks, cats = [], []
    for (_, _, z, lb, _, _, _) in items:
        f = lb + (1.0 - lb) * jax.nn.sigmoid(z)
        logf = jnp.log(f)
        kks.append(1.0 - f)
        hi = logf.astype(BF16)
        lo = (logf - hi.astype(F32)).astype(BF16)
        cats.append(jnp.concatenate([hi, lo], axis=1))
    parts = [jnp.dot(it[4], cat, preferred_element_type=F32) for it, cat in zip(items, cats)]
    q_decs, k_invs, k_ends, decs = [], [], [], []
    for (q, _, _, _, _, _, end_row), kk, part in zip(items, kks, parts):
        bcum = part[:, :LANES] + part[:, LANES:]
        b_end = bcum.reshape(nc, HG_CHUNK, LANES)[:, end_row:end_row + 1, :]
        b_end_rows = jnp.broadcast_to(b_end, (nc, HG_CHUNK, LANES)).reshape(GLA_BLOCK, LANES)
        q_decs.append((q * jnp.exp(bcum)).astype(BF16))
        k_invs.append((kk * jnp.exp(-bcum)).astype(BF16))
        k_ends.append((kk * jnp.exp(b_end_rows - bcum)).astype(BF16))
        decs.append(jnp.exp(b_end.reshape(nc, LANES)))
    scores = [lax.dot_general(qd, ki, (((1,), (1,)), ((), ())), preferred_element_type=F32)
              for qd, ki in zip(q_decs, k_invs)]
    scores = [jnp.where(it[5], a, 0.0).astype(BF16) for it, a in zip(items, scores)]
    outs = [jnp.dot(a, it[1], preferred_element_type=F32) for it, a in zip(items, scores)]
    return list(zip(outs, q_decs, k_ends, decs))


def _gla_kernel(q_ref, i_ref, g_ref, zf_ref, zb_ref, lb_ref, ng_ref, y_ref,
                of_ref, ob_ref, qd_ref, ke_ref, de_ref, st_ref, *, n_lat, n_ctx):
    c = HG_CHUNK
    blk = GLA_BLOCK
    nc = blk // c
    ltot = (n_lat + n_ctx) * c
    row = lax.broadcasted_iota(jnp.int32, (blk, blk), 0)
    col = lax.broadcasted_iota(jnp.int32, (blk, blk), 1)
    same = (row // c) == (col // c)
    masks = (same & (col <= row), same & (col >= row))
    tris = (masks[0].astype(BF16), masks[1].astype(BF16))
    z_refs = (zf_ref, zb_ref)
    o_refs = (of_ref, ob_ref)
    end_rows = (c - 1, 0)

    n_blk = ltot // blk
    per_step = 3 if n_blk % 3 == 0 else 1

    def local(t, carry):
        work = []
        for u in range(per_step):
            r0 = pl.multiple_of((t * per_step + u) * blk, blk)
            c0 = pl.multiple_of((t * per_step + u) * nc, nc)
            q = q_ref[0, 0, pl.ds(r0, blk), :].astype(F32)
            v = i_ref[0, 0, pl.ds(r0, blk), :]
            zs = [z_refs[d][0, 0, pl.ds(r0, blk), :].astype(F32) for d in range(2)]
            work.append((r0, c0, q, v, zs))
        res = _gla_local([(q, v, zs[d], lb_ref[0, d:d + 1, :], tris[d], masks[d], end_rows[d])
                          for (_, _, q, v, zs) in work for d in range(2)])
        for u, (r0, c0, _, _, _) in enumerate(work):
            for d in range(2):
                o, q_dec, k_end, dec = res[2 * u + d]
                o_refs[d][pl.ds(r0, blk), :] = o
                qd_ref[d, pl.ds(r0, blk), :] = q_dec
                ke_ref[d, pl.ds(r0, blk), :] = k_end
                de_ref[d, pl.ds(c0, nc), :] = dec
        return carry
    lax.fori_loop(0, n_blk // per_step, local, 0)

    st_ref[...] = jnp.zeros_like(st_ref)

    def segment(first, n):
        per = 8 if n % 8 == 0 else (4 if n % 4 == 0 else 1)

        def body(jj, carry):
            todo = []
            for u in range(per):
                j = jj * per + u
                for d, ch in enumerate((first + j, first + n - 1 - j)):
                    r0 = pl.multiple_of(ch * c, c)
                    kv = lax.dot_general(i_ref[0, 0, pl.ds(r0, c), :], ke_ref[d, pl.ds(r0, c), :],
                                         (((0,), (0,)), ((), ())), preferred_element_type=F32)
                    todo.append((d, r0, kv, de_ref[d, pl.ds(ch, 1), :], qd_ref[d, pl.ds(r0, c), :]))
            st = [st_ref[0], st_ref[1]]
            inter = []
            for (d, r0, kv, dec, q_dec) in todo:
                inter.append((d, r0, lax.dot_general(q_dec, st[d].astype(BF16), (((1,), (1,)), ((), ())),
                                                     preferred_element_type=F32)))
                st[d] = st[d] * dec + kv
            for (d, r0, o) in inter:
                o_refs[d][pl.ds(r0, c), :] += o
            st_ref[0] = st[0]
            st_ref[1] = st[1]
            return carry
        lax.fori_loop(0, n // per, body, 0)

    segment(n_lat, n_ctx)
    segment(0, n_lat)

    ng = ng_ref[0]
    rt = 256

    def readout(t, carry):
        r0 = pl.multiple_of(t * rt, rt)
        o = of_ref[pl.ds(r0, rt), :] + ob_ref[pl.ds(r0, rt), :]
        ms = jnp.mean(o * o, axis=-1, keepdims=True)
        g = g_ref[0, 0, pl.ds(r0, rt), :].astype(F32)
        y = o * lax.rsqrt(ms + RMS_EPS) * ng * (g * jax.nn.sigmoid(g))
        y_ref[0, 0, pl.ds(r0, rt), :] = y.astype(BF16)
        return carry
    lax.fori_loop(0, ltot // rt, readout, 0)


def _gla(p, lb, ng, n_lat_rows):
    b, nh5, ltot, _ = p.shape
    h = nh5 // 5
    n_lat = n_lat_rows // HG_CHUNK
    n_ctx = (ltot - n_lat_rows) // HG_CHUNK

    def spec(k):
        return pl.BlockSpec((1, 1, ltot, LANES), lambda i, j, k=k: (i, k * h + j, 0, 0))

    return pl.pallas_call(
        functools.partial(_gla_kernel, n_lat=n_lat, n_ctx=n_ctx),
        grid=(b, h),
        in_specs=[spec(0), spec(1), spec(2), spec(3), spec(4),
                  pl.BlockSpec((1, 2, LANES), lambda i, j: (j, 0, 0)),
                  pl.BlockSpec((1, 1, LANES), lambda i, j: (j, 0, 0))],
        out_specs=pl.BlockSpec((1, 1, ltot, LANES), lambda i, j: (i, j, 0, 0)),
        out_shape=jax.ShapeDtypeStruct((b, h, ltot, LANES), BF16),
        scratch_shapes=[pltpu.VMEM((ltot, LANES), F32), pltpu.VMEM((ltot, LANES), F32),
                        pltpu.VMEM((2, ltot, LANES), BF16), pltpu.VMEM((2, ltot, LANES), BF16),
                        pltpu.VMEM((2, ltot // HG_CHUNK, LANES), F32),
                        pltpu.VMEM((2, LANES, LANES), F32)],
        compiler_params=_cparams(("arbitrary", "arbitrary")),
        name="hgrn2_gla",
    )(p, p, p, p, p, lb, ng)


SUM_ROWS = 16


def _attn_kernel(lam_ref, q_ref, k_ref, v_ref, ng_ref, y_ref, vt_ref, *, tk, out_scale):
    tq = q_ref.shape[2]
    lk = k_ref.shape[2]
    lam = lam_ref[0]

    @pl.when(pl.program_id(2) == 0)
    def _():
        blk = 256
        for cb in range(lk // blk):
            vt_ref[0:LANES, cb * blk:(cb + 1) * blk] = jnp.transpose(
                v_ref[0, 0, cb * blk:(cb + 1) * blk, :].astype(F32)).astype(BF16)
        vt_ref[LANES:LANES + SUM_ROWS, :] = jnp.ones((SUM_ROWS, lk), BF16)

    q = q_ref[0, 0]
    lane = lax.broadcasted_iota(jnp.int32, (tq, LANES), 1)
    zero = jnp.zeros_like(q)
    qm = (jnp.where(lane < 64, q, zero), jnp.where(lane >= 64, q, zero))

    def scores(ci, c):
        kc = k_ref[0, 0, ci * tk:(ci + 1) * tk, :]
        return lax.dot_general(kc, qm[c], (((1,), (1,)), ((), ())), preferred_element_type=F32)

    n_chunks = lk // tk
    m = [jnp.full((1, tq), -1e30, F32) for _ in range(2)]
    acc = [jnp.zeros((LANES + SUM_ROWS, tq), F32) for _ in range(2)]
    s_next = [scores(0, 0), scores(0, 1)]
    for ci in range(n_chunks):
        vt = vt_ref[:, ci * tk:(ci + 1) * tk]
        for c in range(2):
            s_cur = s_next[c]
            m_new = jnp.maximum(m[c], jnp.max(s_cur, axis=0, keepdims=True))
            if ci + 1 < n_chunks:
                s_next[c] = scores(ci + 1, c)
            p = jnp.exp2(s_cur - m_new).astype(BF16)
            acc[c] = jnp.exp2(m[c] - m_new) * acc[c] + jnp.dot(vt, p, preferred_element_type=F32)
            m[c] = m_new
    a0, a1 = acc
    o = a0[:LANES] / a0[LANES:LANES + 1] - lam * (a1[:LANES] / a1[LANES:LANES + 1])
    ms = jnp.mean(o * o, axis=0, keepdims=True)
    y = o * lax.rsqrt(ms + RMS_EPS) * ng_ref[...] * out_scale
    y_ref[0, 0] = jnp.transpose(y).astype(BF16)


def _attn(p, lam, ng, n_lat_rows, out_scale, tq=256, tk=768):
    b, nh3, ltot, _ = p.shape
    h = nh3 // 3
    assert ltot % tk == 0 and n_lat_rows % tq == 0 and ltot % 256 == 0
    return pl.pallas_call(
        functools.partial(_attn_kernel, tk=tk, out_scale=out_scale),
        grid=(b, h, n_lat_rows // tq),
        in_specs=[pl.BlockSpec(memory_space=pltpu.SMEM),
                  pl.BlockSpec((1, 1, tq, LANES), lambda i, j, t: (i, j, t, 0)),
                  pl.BlockSpec((1, 1, ltot, LANES), lambda i, j, t: (i, h + j, 0, 0)),
                  pl.BlockSpec((1, 1, ltot, LANES), lambda i, j, t: (i, 2 * h + j, 0, 0)),
                  pl.BlockSpec((LANES, 1), lambda i, j, t: (0, 0))],
        out_specs=pl.BlockSpec((1, 1, tq, LANES), lambda i, j, t: (i, j, t, 0)),
        out_shape=jax.ShapeDtypeStruct((b, h, n_lat_rows, LANES), BF16),
        scratch_shapes=[pltpu.VMEM((LANES + SUM_ROWS, ltot), BF16)],
        compiler_params=_cparams(("arbitrary", "arbitrary", "arbitrary")),
        name="diff_attn",
    )(lam, p, p, p, ng)


def _layer_norm(x, g, b):
    mu = jnp.mean(x, axis=-1, keepdims=True)
    xc = x - mu
    var = jnp.mean(xc * xc, axis=-1, keepdims=True)
    return xc * lax.rsqrt(var + LN_EPS) * g + b


def _post_kernel(y_ref, w_ref, x_ref, g1_ref, lng_ref, lnb_ref, sc2_ref, sh2_ref, wr_ref, br_ref,
                 xo_ref, h2_ref, lg_ref, *, alpha):
    nh = y_ref.shape[1]
    y = jnp.concatenate([y_ref[0, k] for k in range(nh)], axis=-1)
    m = jnp.dot(y, w_ref[...], preferred_element_type=F32)
    xl = _layer_norm(alpha * x_ref[0] + g1_ref[0] * m, lng_ref[...], lnb_ref[...])
    xo_ref[0] = xl
    h2 = xl * (1.0 + sc2_ref[0]) + sh2_ref[0]
    h_hi = h2.astype(BF16)
    h2_ref[0] = h_hi
    h_lo = (h2 - h_hi.astype(F32)).astype(BF16)
    lg_ref[0] = (jnp.dot(h_hi, wr_ref[0], preferred_element_type=F32)
                 + jnp.dot(h_lo, wr_ref[0], preferred_element_type=F32)
                 + jnp.dot(h_hi, wr_ref[1], preferred_element_type=F32) + br_ref[...])


def _post(y, w_out, x, g1, lng, lnb, sc2, sh2, wr, br, n_lat_tiles, n_tiles, tm, alpha):
    b, nh, _, _ = y.shape
    d = x.shape[2]
    nb = g1.shape[0] - 1
    rows = n_tiles * tm

    def mod_map(i, j):
        return (jnp.where(j < n_lat_tiles, i, nb), 0, 0)

    tok = pl.BlockSpec((1, tm, d), lambda i, j: (i, j, 0))
    vec = pl.BlockSpec((1, d), lambda i, j: (0, 0))
    return pl.pallas_call(
        functools.partial(_post_kernel, alpha=alpha),
        grid=(b, n_tiles),
        in_specs=[pl.BlockSpec((1, nh, tm, LANES), lambda i, j: (i, 0, j, 0)),
                  pl.BlockSpec((d, d), lambda i, j: (0, 0)),
                  tok, pl.BlockSpec((1, 1, d), mod_map), vec, vec,
                  pl.BlockSpec((1, 1, d), mod_map), pl.BlockSpec((1, 1, d), mod_map),
                  pl.BlockSpec((2, d, LANES), lambda i, j: (0, 0, 0)),
                  pl.BlockSpec((1, LANES), lambda i, j: (0, 0))],
        out_specs=[tok, tok, pl.BlockSpec((1, tm, LANES), lambda i, j: (i, j, 0))],
        out_shape=[jax.ShapeDtypeStruct((b, rows, d), F32),
                   jax.ShapeDtypeStruct((b, rows, d), BF16),
                   jax.ShapeDtypeStruct((b, rows, LANES), F32)],
        compiler_params=_cparams(("arbitrary", "arbitrary")),
        name="out_proj_ln",
    )(y, w_out, x, g1, lng, lnb, sc2, sh2, wr, br)


def _router_kernel(lg_ref, idx_ref, gate_ref, lpos_ref, lpost_ref, tile_ref, cnt_ref, carry_ref):
    tm = lg_ref.shape[0]

    @pl.when(pl.program_id(0) == 0)
    def _():
        carry_ref[...] = jnp.zeros_like(carry_ref)

    work = lg_ref[...]
    lane = lax.broadcasted_iota(jnp.int32, (tm, LANES), 1)
    lane_f = lane.astype(F32)
    vals, onehots = [], []
    idx_out = jnp.zeros((tm, LANES), jnp.int32)
    for k in range(TOP_K):
        mx = jnp.max(work, axis=-1, keepdims=True)
        am = jnp.min(jnp.where(work == mx, lane_f, float(LANES)), axis=-1, keepdims=True)
        oh = lane_f == am
        vals.append(mx)
        onehots.append(oh)
        idx_out = jnp.where(lane == k, am.astype(jnp.int32), idx_out)
        work = jnp.where(oh, -3e38, work)
    es = [jnp.exp(v - vals[0]) for v in vals]
    denom = es[0] + es[1] + es[2] + es[3]
    gate_out = jnp.zeros((tm, LANES), F32)
    for k in range(TOP_K):
        gate_out = jnp.where(lane == k, es[k] / denom, gate_out)
    member = (onehots[0] | onehots[1] | onehots[2] | onehots[3])
    r = lax.broadcasted_iota(jnp.int32, (tm, tm), 0)
    c = lax.broadcasted_iota(jnp.int32, (tm, tm), 1)
    strict = (c < r).astype(BF16)
    before = jnp.dot(strict, member.astype(BF16), preferred_element_type=F32)
    cnt_tile = jnp.sum(member.astype(F32), axis=0, keepdims=True)
    cnt_tile = jnp.floor((cnt_tile + (SEG_ALIGN - 1)) * (1.0 / SEG_ALIGN)) * SEG_ALIGN
    ei = lax.broadcasted_iota(jnp.int32, (LANES, LANES), 0)
    ej = lax.broadcasted_iota(jnp.int32, (LANES, LANES), 1)
    seg_start = jnp.dot(jnp.broadcast_to(cnt_tile, (8, LANES)).astype(BF16), (ei < ej).astype(BF16),
                        preferred_element_type=F32)[0:1]
    slot = seg_start + before
    lpos_out = jnp.zeros((tm, LANES), jnp.int32)
    for k in range(TOP_K):
        lp = jnp.sum(jnp.where(onehots[k], slot, 0.0), axis=-1, keepdims=True)
        lpos_out = jnp.where(lane == k, lp.astype(jnp.int32), lpos_out)
    idx_ref[...] = idx_out
    gate_ref[...] = gate_out
    lpos_ref[...] = lpos_out
    lpost_ref[...] = jnp.transpose(lpos_out.astype(F32))[0:8, :].astype(jnp.int32)
    row8 = lax.broadcasted_iota(jnp.int32, (8, LANES), 0)
    tile_ref[...] = jnp.where(row8 == 0, carry_ref[...], jnp.where(row8 == 1, cnt_tile, 0.0))
    carry_ref[...] = carry_ref[...] + cnt_tile
    cnt_ref[...] = carry_ref[...]


def _router(logits, tm):
    t = logits.shape[0]
    tile = pl.BlockSpec((tm, LANES), lambda i: (i, 0))
    return pl.pallas_call(
        _router_kernel,
        grid=(t // tm,),
        in_specs=[tile],
        out_specs=[tile, tile, tile, pl.BlockSpec((8, tm), lambda i: (0, i)),
                   pl.BlockSpec((8, LANES), lambda i: (i, 0)), pl.BlockSpec((1, LANES), lambda i: (0, 0))],
        out_shape=[jax.ShapeDtypeStruct((t, LANES), jnp.int32),
                   jax.ShapeDtypeStruct((t, LANES), F32),
                   jax.ShapeDtypeStruct((t, LANES), jnp.int32),
                   jax.ShapeDtypeStruct((8, t), jnp.int32),
                   jax.ShapeDtypeStruct((t // tm * 8, LANES), F32),
                   jax.ShapeDtypeStruct((1, LANES), F32)],
        scratch_shapes=[pltpu.VMEM((1, LANES), F32)],
        compiler_params=_cparams(("arbitrary",)),
        name="router_topk",
    )(logits)


HI_MASK = 0xFFFF0000


def _pack_rows(x):
    half = x.shape[1] // 2
    bits = lax.bitcast_convert_type(x.astype(BF16).astype(F32), jnp.uint32)
    return (bits[:, :half] & jnp.uint32(HI_MASK)) | (bits[:, half:] >> 16)


def _unpack_rows(p):
    left = lax.bitcast_convert_type(p & jnp.uint32(HI_MASK), F32).astype(BF16)
    right = lax.bitcast_convert_type(p << 16, F32).astype(BF16)
    return left, right


def _copy_pieces(n, local_row, global_row, copy, max_rows):
    for b in range(SEG_ALIGN.bit_length() - 1, max_rows.bit_length()):
        size = 1 << b

        @pl.when((n & size) != 0)
        def _():
            lo = n & (size - 1)
            copy(pl.multiple_of(local_row + lo, SEG_ALIGN), pl.multiple_of(global_row + lo, SEG_ALIGN),
                 size).start()


def _segment_copies(base_ref, cnt_ref, copy, max_rows, n_slots, fill_row):
    def per_expert(e, off):
        _copy_pieces(cnt_ref[e], off, base_ref[e], copy, max_rows)
        return off + cnt_ref[e]
    used = lax.fori_loop(0, N_EXPERTS, per_expert, 0)
    _copy_pieces(n_slots - used, used, fill_row, copy, max_rows)


def _tile_slots(tm):
    return tm * TOP_K + N_EXPERTS * SEG_ALIGN


def _dispatch_kernel(base_ref, cnt_ref, lpost_ref, h_ref, xs_in_ref, xs_ref, cbuf_ref, sem, *, spare_row):
    del xs_in_ref
    tm = h_ref.shape[0]
    n_slots = _tile_slots(tm)
    slot = lax.broadcasted_iota(jnp.int32, (n_slots, tm), 0)
    lp = lpost_ref[...]
    sel = slot == lp[0:1, :]
    for k in range(1, TOP_K):
        sel = sel | (slot == lp[k:k + 1, :])
    rows = jnp.dot(sel.astype(BF16), h_ref[...].astype(BF16), preferred_element_type=F32)
    cbuf_ref[...] = _pack_rows(rows)

    def copy(src, dst, size):
        return pltpu.make_async_copy(cbuf_ref.at[pl.ds(src, size)], xs_ref.at[pl.ds(dst, size)], sem)
    _segment_copies(base_ref, cnt_ref, copy, tm, n_slots, spare_row)
    copy(0, 0, n_slots).wait()


def _dispatch(h2, base, cnt, lpost, n_rows, tm):
    t, d = h2.shape
    xs0 = jnp.zeros((n_rows + tm, d // 2), jnp.uint32)
    seg = pl.BlockSpec((LANES,), lambda i: (i,), memory_space=pltpu.SMEM)
    return pl.pallas_call(
        functools.partial(_dispatch_kernel, spare_row=n_rows),
        grid=(t // tm,),
        in_specs=[seg, seg,
                  pl.BlockSpec((8, tm), lambda i: (0, i)),
                  pl.BlockSpec((tm, d), lambda i: (i, 0)),
                  pl.BlockSpec(memory_space=pl.ANY)],
        out_specs=pl.BlockSpec(memory_space=pl.ANY),
        out_shape=jax.ShapeDtypeStruct((n_rows + tm, d // 2), jnp.uint32),
        scratch_shapes=[pltpu.VMEM((_tile_slots(tm), d // 2), jnp.uint32), pltpu.SemaphoreType.DMA(())],
        input_output_aliases={4: 0},
        compiler_params=_cparams(("arbitrary",)),
        name="moe_dispatch",
    )(base, cnt, lpost, h2, xs0)


def _ffn_kernel(be_ref, nu_ref, x_ref, wgu_ref, bgu_ref, wdn_ref, bdn_ref, y_ref):
    de = wdn_ref.shape[1]

    @pl.when(pl.program_id(0) < nu_ref[0])
    def _():
        half = x_ref.shape[1]
        x_left, x_right = _unpack_rows(x_ref[...])
        gu = (jnp.dot(x_left, wgu_ref[0, :half, :], preferred_element_type=F32)
              + jnp.dot(x_right, wgu_ref[0, half:, :], preferred_element_type=F32) + bgu_ref[0])
        gate = jnp.minimum(gu[:, :de], SWIGLU_LIMIT)
        up = jnp.clip(gu[:, de:], -SWIGLU_LIMIT, SWIGLU_LIMIT)
        act = (up + 1.0) * gate * jax.nn.sigmoid(SWIGLU_ALPHA * gate)
        y = jnp.dot(act.astype(BF16), wdn_ref[0], preferred_element_type=F32) + bdn_ref[0]
        y_ref[...] = _pack_rows(y)

    @pl.when(pl.program_id(0) >= nu_ref[0])
    def _():
        y_ref[...] = jnp.zeros_like(y_ref)


def _ffn(xs, n_rows, block_expert, n_used, wgu, bgu, wdn, bdn, tb):
    p, half = n_rows, xs.shape[1]
    d = 2 * half
    e, _, n2 = wgu.shape
    de = wdn.shape[1]
    grid_spec = pltpu.PrefetchScalarGridSpec(
        num_scalar_prefetch=2,
        grid=(p // tb,),
        in_specs=[pl.BlockSpec((tb, half), lambda i, be, nu: (i, 0)),
                  pl.BlockSpec((1, d, n2), lambda i, be, nu: (be[i], 0, 0)),
                  pl.BlockSpec((1, 1, n2), lambda i, be, nu: (be[i], 0, 0)),
                  pl.BlockSpec((1, de, d), lambda i, be, nu: (be[i], 0, 0)),
                  pl.BlockSpec((1, 1, d), lambda i, be, nu: (be[i], 0, 0))],
        out_specs=pl.BlockSpec((tb, half), lambda i, be, nu: (i, 0)),
    )
    return pl.pallas_call(
        _ffn_kernel,
        grid_spec=grid_spec,
        out_shape=jax.ShapeDtypeStruct((p, half), jnp.uint32),
        compiler_params=_cparams(("arbitrary",)),
        name="moe_ffn",
    )(block_expert, n_used, xs, wgu, bgu.reshape(e, 1, n2), wdn, bdn.reshape(e, 1, d))


def _combine_kernel(base_ref, cnt_ref, lpos_ref, gate_ref, x_ref, g2_ref, lng_ref, lnb_ref, ys_ref, o_ref,
                    gbuf_ref, sem, *, alpha):
    tm = x_ref.shape[1]
    n_slots = _tile_slots(tm)

    def copy(dst, src, size):
        return pltpu.make_async_copy(ys_ref.at[pl.ds(src, size)], gbuf_ref.at[pl.ds(dst, size)], sem)
    _segment_copies(base_ref, cnt_ref, copy, tm, n_slots, 0)

    lane = lax.broadcasted_iota(jnp.int32, (tm, n_slots), 1)
    lpos = lpos_ref[...]
    gate = gate_ref[...]
    w = jnp.zeros((tm, n_slots), F32)
    for k in range(TOP_K):
        w = jnp.where(lane == lpos[:, k:k + 1], gate[:, k:k + 1], w)
    w_hi = w.astype(BF16)
    w_lo = (w - w_hi.astype(F32)).astype(BF16)

    copy(0, 0, n_slots).wait()
    y_left, y_right = _unpack_rows(gbuf_ref[...])
    f = jnp.concatenate(
        [jnp.dot(w_hi, y, preferred_element_type=F32) + jnp.dot(w_lo, y, preferred_element_type=F32)
         for y in (y_left, y_right)], axis=1)
    o_ref[0] = _layer_norm(alpha * x_ref[0] + g2_ref[0] * f, lng_ref[...], lnb_ref[...])


def _combine(base, cnt, lpos, gates, x, g2, lng, lnb, ys, n_lat_tiles, n_tiles, tm, alpha):
    b, _, d = x.shape
    nb = g2.shape[0] - 1

    def mod_map(i, j):
        return (jnp.where(j < n_lat_tiles, i, nb), 0, 0)

    tok = pl.BlockSpec((1, tm, d), lambda i, j: (i, j, 0))
    vec = pl.BlockSpec((1, d), lambda i, j: (0, 0))
    seg = pl.BlockSpec((LANES,), lambda i, j: (i * n_tiles + j,), memory_space=pltpu.SMEM)
    per_tok = pl.BlockSpec((tm, LANES), lambda i, j: (i * n_tiles + j, 0))
    return pl.pallas_call(
        functools.partial(_combine_kernel, alpha=alpha),
        grid=(b, n_tiles),
        in_specs=[seg, seg, per_tok, per_tok,
                  tok, pl.BlockSpec((1, 1, d), mod_map), vec, vec,
                  pl.BlockSpec(memory_space=pl.ANY)],
        out_specs=tok,
        out_shape=jax.ShapeDtypeStruct((b, n_tiles * tm, d), F32),
        scratch_shapes=[pltpu.VMEM((_tile_slots(tm), d // 2), jnp.uint32), pltpu.SemaphoreType.DMA(())],
        compiler_params=_cparams(("arbitrary", "arbitrary")),
        name="moe_combine_ln",
    )(base, cnt, lpos, gates, x, g2, lng, lnb, ys)


def _moe_layer(h2, logits, x_res, g2, lng, lnb, wgu, bgu, wdn, bdn, n_lat_tiles, n_tiles, tm, alpha):
    b, rows, d = h2.shape
    t = b * rows
    tb = 512
    _, gates, lpos, lpost, tiles, counts = _router(logits.reshape(t, LANES), tm)
    counts = counts[0, :N_EXPERTS].astype(jnp.int32)
    padded = (counts + tb - 1) // tb * tb
    pend = jnp.cumsum(padded)
    pstart = pend - padded
    tiles = tiles.reshape(t // tm, 8, LANES)[:, :2, :].astype(jnp.int32)
    seg_base = (jnp.pad(pstart, (0, LANES - N_EXPERTS))[None, :] + tiles[:, 0]).reshape(-1)
    seg_cnt = tiles[:, 1].reshape(-1)
    n_rows = t * TOP_K + (t // tm) * N_EXPERTS * (SEG_ALIGN - 1)
    n_rows = (n_rows + tb - 1) // tb * tb + N_EXPERTS * tb
    n_blocks = n_rows // tb
    block_start = jnp.arange(n_blocks, dtype=jnp.int32) * tb
    block_expert = jnp.minimum(jnp.sum((pend[None, :] <= block_start[:, None]).astype(jnp.int32), axis=1),
                               N_EXPERTS - 1).astype(jnp.int32)
    n_used = (pend[-1:] // tb).astype(jnp.int32)
    xs = _dispatch(h2.reshape(t, d), seg_base, seg_cnt, lpost, n_rows, tm)
    ys = _ffn(xs, n_rows, block_expert, n_used, wgu, bgu, wdn, bdn, tb)
    return _combine(seg_base, seg_cnt, lpos, gates, x_res, g2, lng, lnb, ys, n_lat_tiles, n_tiles, tm, alpha)


def _rope_tables(n_lat_rows, n_rows, scale):
    pos = jnp.arange(n_lat_rows)
    lane = jnp.arange(LANES)
    m = lane % 64
    n_freq = 16
    inv = ROPE_BASE ** (-(m % n_freq).astype(F32) / n_freq)
    p = jnp.where((m // 32)[None, :] == 0, (pos // GRID_W)[:, None], (pos % GRID_W)[:, None]).astype(F32)
    ang = p * inv[None, :]
    sign = jnp.where((m % 32) < n_freq, -1.0, 1.0)[None, :]
    cos = jnp.concatenate([jnp.cos(ang), jnp.ones((n_rows - n_lat_rows, LANES), F32)], axis=0)
    sin = jnp.concatenate([jnp.sin(ang) * sign, jnp.zeros((n_rows - n_lat_rows, LANES), F32)], axis=0)
    return cos * scale, sin * scale


def kernel(x, c, ctx, c_ctx, w_ada, b_ada, ln_g, ln_b, hg_w_in, hg_lb, hg_norm_g, hg_w_out, da_w_in, da_lam,
           da_norm_g, da_w_out, moe_w_router, moe_b_router, moe_w_gu, moe_b_gu, moe_w_dn, moe_b_dn):
    bsz, seq, d = x.shape
    lc = ctx.shape[1]
    depth = w_ada.shape[0]
    assert depth == 2 and hg_w_in.shape[0] == 1 and da_w_in.shape[0] == 1
    ltot = seq + lc
    tm = 256
    assert seq % tm == 0 and lc % tm == 0 and seq % GRID_W == 0 and d == DA_HEADS * LANES
    alpha = (2 * depth) ** 0.25
    n_lat_tiles = seq // tm
    n_all_tiles = ltot // tm
    h_heads = d // LANES

    n_mod = bsz + 1
    c_pad = jnp.zeros((16, d), F32).at[:bsz].set(c).at[bsz].set(c_ctx)
    mod = _ada(c_pad, w_ada, b_ada)[:, :n_mod]

    def mods(l):
        return [mod[l, :, k * d:(k + 1) * d].reshape(n_mod, 1, d) for k in range(6)]

    def router_params(l):
        wr = jnp.zeros((d, LANES), F32).at[:, :N_EXPERTS].set(moe_w_router[l])
        wr_hi = wr.astype(BF16)
        wr_lo = (wr - wr_hi.astype(F32)).astype(BF16)
        br = jnp.full((1, LANES), -1e30, F32).at[0, :N_EXPERTS].set(moe_b_router[l])
        return jnp.stack([wr_hi, wr_lo]), br

    xall = jnp.concatenate([x, ctx], axis=1)

    sh1, sc1, g1, sh2, sc2, g2 = mods(0)
    p0 = _proj(xall, sc1, sh1, hg_w_in[0].astype(BF16), n_lat_tiles, tm)
    lb = jnp.cumsum(jax.nn.softmax(hg_lb.astype(F32), axis=1), axis=1)[:, 0]
    lb = lb.reshape(2, h_heads, LANES).transpose(1, 0, 2)
    y0 = _gla(p0, lb, hg_norm_g[0].reshape(h_heads, 1, LANES), seq)
    wr, br = router_params(0)
    x0, h20, lg0 = _post(y0, hg_w_out[0].astype(BF16), xall, g1, ln_g[0, 0].reshape(1, d),
                         ln_b[0, 0].reshape(1, d), sc2, sh2, wr, br, n_lat_tiles, n_all_tiles, tm, alpha)
    x1 = _moe_layer(h20, lg0, x0, g2, ln_g[0, 1].reshape(1, d), ln_b[0, 1].reshape(1, d),
                    moe_w_gu[0].astype(BF16), moe_b_gu[0], moe_w_dn[0].astype(BF16), moe_b_dn[0],
                    n_lat_tiles, n_all_tiles, tm, alpha)

    sh1, sc1, g1, sh2, sc2, g2 = mods(1)
    dh = d // DA_HEADS // 2
    cq, sq = _rope_tables(seq, ltot, dh ** -0.5 * math.log2(math.e))
    ck, sk = _rope_tables(seq, ltot, 1.0)
    p1 = _proj(x1, sc1, sh1, da_w_in[0].astype(BF16), n_lat_tiles, tm, rope_tabs=(cq, sq, ck, sk))
    lam_init = 0.8 - 0.6 * math.exp(-0.3 * 1)
    lp = da_lam[0].astype(F32)
    lam = (jnp.exp(jnp.sum(lp[0] * lp[1])) - jnp.exp(jnp.sum(lp[2] * lp[3])) + lam_init).reshape(1)
    y1 = _attn(p1, lam, da_norm_g[0].reshape(LANES, 1), seq, 1.0 - lam_init)
    wr, br = router_params(1)
    x2, h21, lg1 = _post(y1, da_w_out[0].astype(BF16), x1, g1, ln_g[1, 0].reshape(1, d),
                         ln_b[1, 0].reshape(1, d), sc2, sh2, wr, br, n_lat_tiles, n_lat_tiles, tm, alpha)
    return _moe_layer(h21, lg1, x2, g2, ln_g[1, 1].reshape(1, d), ln_b[1, 1].reshape(1, d),
                      moe_w_gu[1].astype(BF16), moe_b_gu[1], moe_w_dn[1].astype(BF16), moe_b_dn[1],
                      n_lat_tiles, n_lat_tiles, tm, alpha)
```

```python
import functools
import math

import jax
import jax.numpy as jnp
from jax import lax
from jax.experimental import pallas as pl
from jax.experimental.pallas import tpu as pltpu

F32 = jnp.float32
BF16 = jnp.bfloat16
HIGHEST = lax.Precision.HIGHEST

LANES = 128
HG_CHUNK = 64
GRID_W = 64
ROPE_BASE = 10000.0
DA_HEADS = 8
N_EXPERTS = 32
TOP_K = 4
SEG_ALIGN = 8
SWIGLU_LIMIT = 7.0
SWIGLU_ALPHA = 1.702
LN_EPS = 1e-5
RMS_EPS = 1e-6
VMEM_LIMIT = 56 * 1024 * 1024


def _cparams(sem):
    return pltpu.CompilerParams(dimension_semantics=sem, vmem_limit_bytes=VMEM_LIMIT)


def _ada_kernel(c_ref, w_ref, b_ref, o_ref):
    c = c_ref[...]
    s = c * jax.nn.sigmoid(c)
    o_ref[0] = jnp.dot(s, w_ref[0], precision=HIGHEST, preferred_element_type=F32) + b_ref[0]


def _ada(c_pad, w_ada, b_ada):
    depth, d, n = w_ada.shape
    rows = c_pad.shape[0]
    tn = 1536
    return pl.pallas_call(
        _ada_kernel,
        grid=(depth, n // tn),
        in_specs=[pl.BlockSpec((rows, d), lambda l, j: (0, 0)),
                  pl.BlockSpec((1, d, tn), lambda l, j: (l, 0, j)),
                  pl.BlockSpec((1, 1, tn), lambda l, j: (l, 0, j))],
        out_specs=pl.BlockSpec((1, rows, tn), lambda l, j: (l, 0, j)),
        out_shape=jax.ShapeDtypeStruct((depth, rows, n), F32),
        compiler_params=_cparams(("arbitrary", "arbitrary")),
        name="ada_mod",
    )(c_pad, w_ada, b_ada.reshape(depth, 1, n))


def _rope(piece, cos, sin, lo_mask):
    rot = jnp.where(lo_mask, pltpu.roll(piece, LANES - 16, 1), pltpu.roll(piece, 16, 1))
    return piece * cos + rot * sin


def _proj_kernel(x_ref, sc_ref, sh_ref, w_ref, *rest, n_rope, cw):
    if n_rope:
        cq_ref, sq_ref, ck_ref, sk_ref, o_ref = rest
    else:
        (o_ref,) = rest
    n = w_ref.shape[1]
    h = (x_ref[0] * (1.0 + sc_ref[0]) + sh_ref[0]).astype(BF16)
    if n_rope:
        lane = lax.broadcasted_iota(jnp.int32, (x_ref.shape[1], LANES), 1)
        lo_mask = (lane % 32) < 16
    per = cw // LANES
    for j in range(n // cw):
        r = jnp.dot(h, w_ref[:, j * cw:(j + 1) * cw], preferred_element_type=F32)
        for g in range(per):
            hd = j * per + g
            piece = r[:, g * LANES:(g + 1) * LANES]
            if hd < n_rope:
                piece = _rope(piece, cq_ref[...], sq_ref[...], lo_mask)
            elif hd < 2 * n_rope:
                piece = _rope(piece, ck_ref[...], sk_ref[...], lo_mask)
            o_ref[0, hd] = piece.astype(BF16)


def _proj(x, sc, sh, w, n_lat_tiles, tm, rope_tabs=None):
    b, ltot, d = x.shape
    n = w.shape[1]
    nb = sc.shape[0] - 1
    n_rope = DA_HEADS if rope_tabs is not None else 0

    def mod_map(i, j):
        return (jnp.where(j < n_lat_tiles, i, nb), 0, 0)

    in_specs = [pl.BlockSpec((1, tm, d), lambda i, j: (i, j, 0)),
                pl.BlockSpec((1, 1, d), mod_map),
                pl.BlockSpec((1, 1, d), mod_map),
                pl.BlockSpec((d, n), lambda i, j: (0, 0))]
    args = [x, sc, sh, w]
    if rope_tabs is not None:
        in_specs += [pl.BlockSpec((tm, LANES), lambda i, j: (j, 0))] * 4
        args += list(rope_tabs)
    return pl.pallas_call(
        functools.partial(_proj_kernel, n_rope=n_rope, cw=512),
        grid=(b, ltot // tm),
        in_specs=in_specs,
        out_specs=pl.BlockSpec((1, n // LANES, tm, LANES), lambda i, j: (i, 0, j, 0)),
        out_shape=jax.ShapeDtypeStruct((b, n // LANES, ltot, LANES), BF16),
        compiler_params=_cparams(("arbitrary", "arbitrary")),
        name="mod_proj_rope" if n_rope else "mod_proj",
    )(*args)


GLA_BLOCK = 256


def _gla_local(items):
    nc = GLA_BLOCK // HG_CHUNK
    kks, cats = [], []
    for (_, _, z, lb, _, _, _) in items:
        f = lb + (1.0 - lb) * jax.nn.sigmoid(z)
        logf = jnp.log(f)
        kks.append(1.0 - f)
        hi = logf.astype(BF16)
        lo = (logf - hi.astype(F32)).astype(BF16)
        cats.append(jnp.concatenate([hi, lo], axis=1))
    parts = [jnp.dot(it[4], cat, preferred_element_type=F32) for it, cat in zip(items, cats)]
    q_decs, k_invs, k_ends, decs = [], [], [], []
    for (q, _, _, _, _, _, end_row), kk, part in zip(items, kks, parts):
        bcum = part[:, :LANES] + part[:, LANES:]
        b_end = bcum.reshape(nc, HG_CHUNK, LANES)[:, end_row:end_row + 1, :]
        b_end_rows = jnp.broadcast_to(b_end, (nc, HG_CHUNK, LANES)).reshape(GLA_BLOCK, LANES)
        q_decs.append((q * jnp.exp(bcum)).astype(BF16))
        k_invs.append((kk * jnp.exp(-bcum)).astype(BF16))
        k_ends.append((kk * jnp.exp(b_end_rows - bcum)).astype(BF16))
        decs.append(jnp.exp(b_end.reshape(nc, LANES)))
    scores = [lax.dot_general(qd, ki, (((1,), (1,)), ((), ())), preferred_element_type=F32)
              for qd, ki in zip(q_decs, k_invs)]
    scores = [jnp.where(it[5], a, 0.0).astype(BF16) for it, a in zip(items, scores)]
    outs = [jnp.dot(a, it[1], preferred_element_type=F32) for it, a in zip(items, scores)]
    return list(zip(outs, q_decs, k_ends, decs))


def _gla_kernel(q_ref, i_ref, g_ref, zf_ref, zb_ref, lb_ref, ng_ref, y_ref,
                of_ref, ob_ref, qd_ref, ke_ref, de_ref, st_ref, *, n_lat, n_ctx):
    c = HG_CHUNK
    blk = GLA_BLOCK
    nc = blk // c
    ltot = (n_lat + n_ctx) * c
    row = lax.broadcasted_iota(jnp.int32, (blk, blk), 0)
    col = lax.broadcasted_iota(jnp.int32, (blk, blk), 1)
    same = (row // c) == (col // c)
    masks = (same & (col <= row), same & (col >= row))
    tris = (masks[0].astype(BF16), masks[1].astype(BF16))
    z_refs = (zf_ref, zb_ref)
    o_refs = (of_ref, ob_ref)
    end_rows = (c - 1, 0)

    n_blk = ltot // blk
    per_step = 3 if n_blk % 3 == 0 else 1

    def local(t, carry):
        work = []
        for u in range(per_step):
            r0 = pl.multiple_of((t * per_step + u) * blk, blk)
            c0 = pl.multiple_of((t * per_step + u) * nc, nc)
            q = q_ref[0, 0, pl.ds(r0, blk), :].astype(F32)
            v = i_ref[0, 0, pl.ds(r0, blk), :]
            zs = [z_refs[d][0, 0, pl.ds(r0, blk), :].astype(F32) for d in range(2)]
            work.append((r0, c0, q, v, zs))
        res = _gla_local([(q, v, zs[d], lb_ref[0, d:d + 1, :], tris[d], masks[d], end_rows[d])
                          for (_, _, q, v, zs) in work for d in range(2)])
        for u, (r0, c0, _, _, _) in enumerate(work):
            for d in range(2):
                o, q_dec, k_end, dec = res[2 * u + d]
                o_refs[d][pl.ds(r0, blk), :] = o
                qd_ref[d, pl.ds(r0, blk), :] = q_dec
                ke_ref[d, pl.ds(r0, blk), :] = k_end
                de_ref[d, pl.ds(c0, nc), :] = dec
        return carry
    lax.fori_loop(0, n_blk // per_step, local, 0)

    st_ref[...] = jnp.zeros_like(st_ref)

    def segment(first, n):
        per = 8 if n % 8 == 0 else (4 if n % 4 == 0 else 1)

        def body(jj, carry):
            todo = []
            for u in range(per):
                j = jj * per + u
                for d, ch in enumerate((first + j, first + n - 1 - j)):
                    r0 = pl.multiple_of(ch * c, c)
                    kv = lax.dot_general(i_ref[0, 0, pl.ds(r0, c), :], ke_ref[d, pl.ds(r0, c), :],
                                         (((0,), (0,)), ((), ())), preferred_element_type=F32)
                    todo.append((d, r0, kv, de_ref[d, pl.ds(ch, 1), :], qd_ref[d, pl.ds(r0, c), :]))
            st = [st_ref[0], st_ref[1]]
            inter = []
            for (d, r0, kv, dec, q_dec) in todo:
                inter.append((d, r0, lax.dot_general(q_dec, st[d].astype(BF16), (((1,), (1,)), ((), ())),
                                                     preferred_element_type=F32)))
                st[d] = st[d] * dec + kv
            for (d, r0, o) in inter:
                o_refs[d][pl.ds(r0, c), :] += o
            st_ref[0] = st[0]
            st_ref[1] = st[1]
            return carry
        lax.fori_loop(0, n // per, body, 0)

    segment(n_lat, n_ctx)
    segment(0, n_lat)

    ng = ng_ref[0]
    rt = 256

    def readout(t, carry):
        r0 = pl.multiple_of(t * rt, rt)
        o = of_ref[pl.ds(r0, rt), :] + ob_ref[pl.ds(r0, rt), :]
        ms = jnp.mean(o * o, axis=-1, keepdims=True)
        g = g_ref[0, 0, pl.ds(r0, rt), :].astype(F32)
        y = o * lax.rsqrt(ms + RMS_EPS) * ng * (g * jax.nn.sigmoid(g))
        y_ref[0, 0, pl.ds(r0, rt), :] = y.astype(BF16)
        return carry
    lax.fori_loop(0, ltot // rt, readout, 0)


def _gla(p, lb, ng, n_lat_rows):
    b, nh5, ltot, _ = p.shape
    h = nh5 // 5
    n_lat = n_lat_rows // HG_CHUNK
    n_ctx = (ltot - n_lat_rows) // HG_CHUNK

    def spec(k):
        return pl.BlockSpec((1, 1, ltot, LANES), lambda i, j, k=k: (i, k * h + j, 0, 0))

    return pl.pallas_call(
        functools.partial(_gla_kernel, n_lat=n_lat, n_ctx=n_ctx),
        grid=(b, h),
        in_specs=[spec(0), spec(1), spec(2), spec(3), spec(4),
                  pl.BlockSpec((1, 2, LANES), lambda i, j: (j, 0, 0)),
                  pl.BlockSpec((1, 1, LANES), lambda i, j: (j, 0, 0))],
        out_specs=pl.BlockSpec((1, 1, ltot, LANES), lambda i, j: (i, j, 0, 0)),
        out_shape=jax.ShapeDtypeStruct((b, h, ltot, LANES), BF16),
        scratch_shapes=[pltpu.VMEM((ltot, LANES), F32), pltpu.VMEM((ltot, LANES), F32),
                        pltpu.VMEM((2, ltot, LANES), BF16), pltpu.VMEM((2, ltot, LANES), BF16),
                        pltpu.VMEM((2, ltot // HG_CHUNK, LANES), F32),
                        pltpu.VMEM((2, LANES, LANES), F32)],
        compiler_params=_cparams(("arbitrary", "arbitrary")),
        name="hgrn2_gla",
    )(p, p, p, p, p, lb, ng)


SUM_ROWS = 16


def _attn_kernel(lam_ref, q_ref, k_ref, v_ref, ng_ref, y_ref, vt_ref, *, tk, out_scale):
    tq = q_ref.shape[2]
    lk = k_ref.shape[2]
    lam = lam_ref[0]

    @pl.when(pl.program_id(2) == 0)
    def _():
        blk = 256
        for cb in range(lk // blk):
            vt_ref[0:LANES, cb * blk:(cb + 1) * blk] = jnp.transpose(
                v_ref[0, 0, cb * blk:(cb + 1) * blk, :].astype(F32)).astype(BF16)
        vt_ref[LANES:LANES + SUM_ROWS, :] = jnp.ones((SUM_ROWS, lk), BF16)

    q = q_ref[0, 0]
    lane = lax.broadcasted_iota(jnp.int32, (tq, LANES), 1)
    zero = jnp.zeros_like(q)
    qm = (jnp.where(lane < 64, q, zero), jnp.where(lane >= 64, q, zero))

    def scores(ci, c):
        kc = k_ref[0, 0, ci * tk:(ci + 1) * tk, :]
        return lax.dot_general(kc, qm[c], (((1,), (1,)), ((), ())), preferred_element_type=F32)

    n_chunks = lk // tk
    m = [jnp.full((1, tq), -1e30, F32) for _ in range(2)]
    acc = [jnp.zeros((LANES + SUM_ROWS, tq), F32) for _ in range(2)]
    s_next = [scores(0, 0), scores(0, 1)]
    for ci in range(n_chunks):
        vt = vt_ref[:, ci * tk:(ci + 1) * tk]
        for c in range(2):
            s_cur = s_next[c]
            m_new = jnp.maximum(m[c], jnp.max(s_cur, axis=0, keepdims=True))
            if ci + 1 < n_chunks:
                s_next[c] = scores(ci + 1, c)
            p = jnp.exp2(s_cur - m_new).astype(BF16)
            acc[c] = jnp.exp2(m[c] - m_new) * acc[c] + jnp.dot(vt, p, preferred_element_type=F32)
            m[c] = m_new
    a0, a1 = acc
    o = a0[:LANES] / a0[LANES:LANES + 1] - lam * (a1[:LANES] / a1[LANES:LANES + 1])
    ms = jnp.mean(o * o, axis=0, keepdims=True)
    y = o * lax.rsqrt(ms + RMS_EPS) * ng_ref[...] * out_scale
    y_ref[0, 0] = jnp.transpose(y).astype(BF16)


def _attn(p, lam, ng, n_lat_rows, out_scale, tq=256, tk=768):
    b, nh3, ltot, _ = p.shape
    h = nh3 // 3
    assert ltot % tk == 0 and n_lat_rows % tq == 0 and ltot % 256 == 0
    return pl.pallas_call(
        functools.partial(_attn_kernel, tk=tk, out_scale=out_scale),
        grid=(b, h, n_lat_rows // tq),
        in_specs=[pl.BlockSpec(memory_space=pltpu.SMEM),
                  pl.BlockSpec((1, 1, tq, LANES), lambda i, j, t: (i, j, t, 0)),
                  pl.BlockSpec((1, 1, ltot, LANES), lambda i, j, t: (i, h + j, 0, 0)),
                  pl.BlockSpec((1, 1, ltot, LANES), lambda i, j, t: (i, 2 * h + j, 0, 0)),
                  pl.BlockSpec((LANES, 1), lambda i, j, t: (0, 0))],
        out_specs=pl.BlockSpec((1, 1, tq, LANES), lambda i, j, t: (i, j, t, 0)),
        out_shape=jax.ShapeDtypeStruct((b, h, n_lat_rows, LANES), BF16),
        scratch_shapes=[pltpu.VMEM((LANES + SUM_ROWS, ltot), BF16)],
        compiler_params=_cparams(("arbitrary", "arbitrary", "arbitrary")),
        name="diff_attn",
    )(lam, p, p, p, ng)


def _layer_norm(x, g, b):
    mu = jnp.mean(x, axis=-1, keepdims=True)
    xc = x - mu
    var = jnp.mean(xc * xc, axis=-1, keepdims=True)
    return xc * lax.rsqrt(var + LN_EPS) * g + b


def _post_kernel(y_ref, w_ref, x_ref, g1_ref, lng_ref, lnb_ref, sc2_ref, sh2_ref, wr_ref, br_ref,
                 xo_ref, h2_ref, lg_ref, *, alpha):
    nh = y_ref.shape[1]
    y = jnp.concatenate([y_ref[0, k] for k in range(nh)], axis=-1)
    m = jnp.dot(y, w_ref[...], preferred_element_type=F32)
    xl = _layer_norm(alpha * x_ref[0] + g1_ref[0] * m, lng_ref[...], lnb_ref[...])
    xo_ref[0] = xl
    h2 = xl * (1.0 + sc2_ref[0]) + sh2_ref[0]
    h_hi = h2.astype(BF16)
    h2_ref[0] = h_hi
    h_lo = (h2 - h_hi.astype(F32)).astype(BF16)
    lg_ref[0] = (jnp.dot(h_hi, wr_ref[0], preferred_element_type=F32)
                 + jnp.dot(h_lo, wr_ref[0], preferred_element_type=F32)
                 + jnp.dot(h_hi, wr_ref[1], preferred_element_type=F32) + br_ref[...])


def _post(y, w_out, x, g1, lng, lnb, sc2, sh2, wr, br, n_lat_tiles, n_tiles, tm, alpha):
    b, nh, _, _ = y.shape
    d = x.shape[2]
    nb = g1.shape[0] - 1
    rows = n_tiles * tm

    def mod_map(i, j):
        return (jnp.where(j < n_lat_tiles, i, nb), 0, 0)

    tok = pl.BlockSpec((1, tm, d), lambda i, j: (i, j, 0))
    vec = pl.BlockSpec((1, d), lambda i, j: (0, 0))
    return pl.pallas_call(
        functools.partial(_post_kernel, alpha=alpha),
        grid=(b, n_tiles),
        in_specs=[pl.BlockSpec((1, nh, tm, LANES), lambda i, j: (i, 0, j, 0)),
                  pl.BlockSpec((d, d), lambda i, j: (0, 0)),
                  tok, pl.BlockSpec((1, 1, d), mod_map), vec, vec,
                  pl.BlockSpec((1, 1, d), mod_map), pl.BlockSpec((1, 1, d), mod_map),
                  pl.BlockSpec((2, d, LANES), lambda i, j: (0, 0, 0)),
                  pl.BlockSpec((1, LANES), lambda i, j: (0, 0))],
        out_specs=[tok, tok, pl.BlockSpec((1, tm, LANES), lambda i, j: (i, j, 0))],
        out_shape=[jax.ShapeDtypeStruct((b, rows, d), F32),
                   jax.ShapeDtypeStruct((b, rows, d), BF16),
                   jax.ShapeDtypeStruct((b, rows, LANES), F32)],
        compiler_params=_cparams(("arbitrary", "arbitrary")),
        name="out_proj_ln",
    )(y, w_out, x, g1, lng, lnb, sc2, sh2, wr, br)


def _router_kernel(lg_ref, idx_ref, gate_ref, lpos_ref, lpost_ref, tile_ref, cnt_ref, carry_ref):
    tm = lg_ref.shape[0]

    @pl.when(pl.program_id(0) == 0)
    def _():
        carry_ref[...] = jnp.zeros_like(carry_ref)

    work = lg_ref[...]
    lane = lax.broadcasted_iota(jnp.int32, (tm, LANES), 1)
    lane_f = lane.astype(F32)
    vals, onehots = [], []
    idx_out = jnp.zeros((tm, LANES), jnp.int32)
    for k in range(TOP_K):
        mx = jnp.max(work, axis=-1, keepdims=True)
        am = jnp.min(jnp.where(work == mx, lane_f, float(LANES)), axis=-1, keepdims=True)
        oh = lane_f == am
        vals.append(mx)
        onehots.append(oh)
        idx_out = jnp.where(lane == k, am.astype(jnp.int32), idx_out)
        work = jnp.where(oh, -3e38, work)
    es = [jnp.exp(v - vals[0]) for v in vals]
    denom = es[0] + es[1] + es[2] + es[3]
    gate_out = jnp.zeros((tm, LANES), F32)
    for k in range(TOP_K):
        gate_out = jnp.where(lane == k, es[k] / denom, gate_out)
    member = (onehots[0] | onehots[1] | onehots[2] | onehots[3])
    r = lax.broadcasted_iota(jnp.int32, (tm, tm), 0)
    c = lax.broadcasted_iota(jnp.int32, (tm, tm), 1)
    strict = (c < r).astype(BF16)
    before = jnp.dot(strict, member.astype(BF16), preferred_element_type=F32)
    cnt_tile = jnp.sum(member.astype(F32), axis=0, keepdims=True)
    cnt_tile = jnp.floor((cnt_tile + (SEG_ALIGN - 1)) * (1.0 / SEG_ALIGN)) * SEG_ALIGN
    ei = lax.broadcasted_iota(jnp.int32, (LANES, LANES), 0)
    ej = lax.broadcasted_iota(jnp.int32, (LANES, LANES), 1)
    seg_start = jnp.dot(jnp.broadcast_to(cnt_tile, (8, LANES)).astype(BF16), (ei < ej).astype(BF16),
                        preferred_element_type=F32)[0:1]
    slot = seg_start + before
    lpos_out = jnp.zeros((tm, LANES), jnp.int32)
    for k in range(TOP_K):
        lp = jnp.sum(jnp.where(onehots[k], slot, 0.0), axis=-1, keepdims=True)
        lpos_out = jnp.where(lane == k, lp.astype(jnp.int32), lpos_out)
    idx_ref[...] = idx_out
    gate_ref[...] = gate_out
    lpos_ref[...] = lpos_out
    lpost_ref[...] = jnp.transpose(lpos_out.astype(F32))[0:8, :].astype(jnp.int32)
    row8 = lax.broadcasted_iota(jnp.int32, (8, LANES), 0)
    tile_ref[...] = jnp.where(row8 == 0, carry_ref[...], jnp.where(row8 == 1, cnt_tile, 0.0))
    carry_ref[...] = carry_ref[...] + cnt_tile
    cnt_ref[...] = carry_ref[...]


def _router(logits, tm):
    t = logits.shape[0]
    tile = pl.BlockSpec((tm, LANES), lambda i: (i, 0))
    return pl.pallas_call(
        _router_kernel,
        grid=(t // tm,),
        in_specs=[tile],
        out_specs=[tile, tile, tile, pl.BlockSpec((8, tm), lambda i: (0, i)),
                   pl.BlockSpec((8, LANES), lambda i: (i, 0)), pl.BlockSpec((1, LANES), lambda i: (0, 0))],
        out_shape=[jax.ShapeDtypeStruct((t, LANES), jnp.int32),
                   jax.ShapeDtypeStruct((t, LANES), F32),
                   jax.ShapeDtypeStruct((t, LANES), jnp.int32),
                   jax.ShapeDtypeStruct((8, t), jnp.int32),
                   jax.ShapeDtypeStruct((t // tm * 8, LANES), F32),
                   jax.ShapeDtypeStruct((1, LANES), F32)],
        scratch_shapes=[pltpu.VMEM((1, LANES), F32)],
        compiler_params=_cparams(("arbitrary",)),
        name="router_topk",
    )(logits)


HI_MASK = 0xFFFF0000


def _pack_rows(x):
    half = x.shape[1] // 2
    bits = lax.bitcast_convert_type(x.astype(BF16).astype(F32), jnp.uint32)
    return (bits[:, :half] & jnp.uint32(HI_MASK)) | (bits[:, half:] >> 16)


def _unpack_rows(p):
    left = lax.bitcast_convert_type(p & jnp.uint32(HI_MASK), F32).astype(BF16)
    right = lax.bitcast_convert_type(p << 16, F32).astype(BF16)
    return left, right


def _copy_pieces(n, local_row, global_row, copy, max_rows, wait=False):
    for b in range(SEG_ALIGN.bit_length() - 1, max_rows.bit_length()):
        size = 1 << b

        @pl.when((n & size) != 0)
        def _():
            lo = n & (size - 1)
            piece = copy(pl.multiple_of(local_row + lo, SEG_ALIGN), pl.multiple_of(global_row + lo, SEG_ALIGN),
                         size)
            if wait:
                piece.wait()
            else:
                piece.start()


def _segment_copies(base_ref, cnt_ref, copy, max_rows, n_slots, fill_row):
    def per_expert(e, off):
        _copy_pieces(cnt_ref[e], off, base_ref[e], copy, max_rows)
        return off + cnt_ref[e]
    used = lax.fori_loop(0, N_EXPERTS, per_expert, 0)
    _copy_pieces(n_slots - used, used, fill_row, copy, max_rows)


def _tile_slots(tm):
    return tm * TOP_K + N_EXPERTS * SEG_ALIGN


def _dispatch_kernel(base_ref, cnt_ref, tail_ref, tail_len_ref, lpost_ref, h_ref, xs_ref,
                     cbuf_ref, zbuf_ref, sem, zsem, *, spare_row):
    tm = h_ref.shape[0]

    @pl.when(pl.program_id(0) == pl.num_programs(0) - 1)
    def _():
        zbuf_ref[...] = jnp.zeros_like(zbuf_ref)

        def zero_copy(src, dst, size):
            return pltpu.make_async_copy(zbuf_ref.at[pl.ds(src, size)], xs_ref.at[pl.ds(dst, size)], zsem)
        for wait in (False, True):
            def per_expert(e, carry, wait=wait):
                _copy_pieces(tail_len_ref[e], 0, tail_ref[e], zero_copy, zbuf_ref.shape[0], wait=wait)
                return carry
            lax.fori_loop(0, N_EXPERTS, per_expert, 0)

    n_slots = _tile_slots(tm)
    slot = lax.broadcasted_iota(jnp.int32, (n_slots, tm), 0)
    lp = lpost_ref[...]
    sel = slot == lp[0:1, :]
    for k in range(1, TOP_K):
        sel = sel | (slot == lp[k:k + 1, :])
    rows = jnp.dot(sel.astype(BF16), h_ref[...].astype(BF16), preferred_element_type=F32)
    cbuf_ref[...] = _pack_rows(rows)

    def copy(src, dst, size):
        return pltpu.make_async_copy(cbuf_ref.at[pl.ds(src, size)], xs_ref.at[pl.ds(dst, size)], sem)
    _segment_copies(base_ref, cnt_ref, copy, tm, n_slots, spare_row)
    copy(0, 0, n_slots).wait()


def _dispatch(h2, base, cnt, tail, tail_len, lpost, n_rows, tm, tb):
    t, d = h2.shape
    seg = pl.BlockSpec((LANES,), lambda i: (i,), memory_space=pltpu.SMEM)
    whole = pl.BlockSpec(memory_space=pltpu.SMEM)
    return pl.pallas_call(
        functools.partial(_dispatch_kernel, spare_row=n_rows),
        grid=(t // tm,),
        in_specs=[seg, seg, whole, whole,
                  pl.BlockSpec((8, tm), lambda i: (0, i)),
                  pl.BlockSpec((tm, d), lambda i: (i, 0))],
        out_specs=pl.BlockSpec(memory_space=pl.ANY),
        out_shape=jax.ShapeDtypeStruct((n_rows + tm, d // 2), jnp.uint32),
        scratch_shapes=[pltpu.VMEM((_tile_slots(tm), d // 2), jnp.uint32),
                        pltpu.VMEM((tb, d // 2), jnp.uint32),
                        pltpu.SemaphoreType.DMA(()), pltpu.SemaphoreType.DMA(())],
        compiler_params=_cparams(("arbitrary",)),
        name="moe_dispatch",
    )(base, cnt, tail, tail_len, lpost, h2)


def _ffn_kernel(be_ref, nu_ref, x_ref, wgu_ref, bgu_ref, wdn_ref, bdn_ref, y_ref, wgu_bf_ref, wdn_bf_ref):
    de = wdn_ref.shape[2]
    i = pl.program_id(0)
    used = i < nu_ref[0]

    @pl.when(used & ((i == 0) | (be_ref[i] != be_ref[jnp.maximum(i - 1, 0)])))
    def _():
        wgu_bf_ref[...] = wgu_ref[0, 0].astype(BF16)
        wdn_bf_ref[...] = wdn_ref[0, 0].astype(BF16)

    @pl.when(used)
    def _():
        half = x_ref.shape[1]
        x_left, x_right = _unpack_rows(x_ref[...])
        gu = (jnp.dot(x_left, wgu_bf_ref[:half, :], preferred_element_type=F32)
              + jnp.dot(x_right, wgu_bf_ref[half:, :], preferred_element_type=F32) + bgu_ref[0, 0])
        gate = jnp.minimum(gu[:, :de], SWIGLU_LIMIT)
        up = jnp.clip(gu[:, de:], -SWIGLU_LIMIT, SWIGLU_LIMIT)
        act = (up + 1.0) * gate * jax.nn.sigmoid(SWIGLU_ALPHA * gate)
        y = jnp.dot(act.astype(BF16), wdn_bf_ref[...], preferred_element_type=F32) + bdn_ref[0, 0]
        y_ref[...] = _pack_rows(y)

    @pl.when(jnp.logical_not(used))
    def _():
        y_ref[...] = jnp.zeros_like(y_ref)


def _ffn(xs, n_rows, block_expert, n_used, layer, wgu, bgu, wdn, bdn, tb):
    p, half = n_rows, xs.shape[1]
    d = 2 * half
    nl, e, _, n2 = wgu.shape
    de = wdn.shape[2]
    grid_spec = pltpu.PrefetchScalarGridSpec(
        num_scalar_prefetch=2,
        grid=(p // tb,),
        in_specs=[pl.BlockSpec((tb, half), lambda i, be, nu: (i, 0)),
                  pl.BlockSpec((1, 1, d, n2), lambda i, be, nu: (layer, be[i], 0, 0)),
                  pl.BlockSpec((1, 1, 1, n2), lambda i, be, nu: (layer, be[i], 0, 0)),
                  pl.BlockSpec((1, 1, de, d), lambda i, be, nu: (layer, be[i], 0, 0)),
                  pl.BlockSpec((1, 1, 1, d), lambda i, be, nu: (layer, be[i], 0, 0))],
        out_specs=pl.BlockSpec((tb, half), lambda i, be, nu: (i, 0)),
        scratch_shapes=[pltpu.VMEM((d, n2), BF16), pltpu.VMEM((de, d), BF16)],
    )
    return pl.pallas_call(
        _ffn_kernel,
        grid_spec=grid_spec,
        out_shape=jax.ShapeDtypeStruct((p, half), jnp.uint32),
        compiler_params=_cparams(("arbitrary",)),
        name="moe_ffn",
    )(block_expert, n_used, xs, wgu, bgu.reshape(nl, e, 1, n2), wdn, bdn.reshape(nl, e, 1, d))


def _combine_kernel(base_ref, cnt_ref, lpos_ref, gate_ref, x_ref, g2_ref, lng_ref, lnb_ref, ys_ref, o_ref,
                    gbuf_ref, sem, *, alpha):
    tm = x_ref.shape[1]
    n_slots = _tile_slots(tm)

    def copy(dst, src, size):
        return pltpu.make_async_copy(ys_ref.at[pl.ds(src, size)], gbuf_ref.at[pl.ds(dst, size)], sem)
    _segment_copies(base_ref, cnt_ref, copy, tm, n_slots, 0)

    lane = lax.broadcasted_iota(jnp.int32, (tm, n_slots), 1)
    lpos = lpos_ref[...]
    gate = gate_ref[...]
    w = jnp.zeros((tm, n_slots), F32)
    for k in range(TOP_K):
        w = jnp.where(lane == lpos[:, k:k + 1], gate[:, k:k + 1], w)
    w_hi = w.astype(BF16)
    w_lo = (w - w_hi.astype(F32)).astype(BF16)

    copy(0, 0, n_slots).wait()
    y_left, y_right = _unpack_rows(gbuf_ref[...])
    f = jnp.concatenate(
        [jnp.dot(w_hi, y, preferred_element_type=F32) + jnp.dot(w_lo, y, preferred_element_type=F32)
         for y in (y_left, y_right)], axis=1)
    o_ref[0] = _layer_norm(alpha * x_ref[0] + g2_ref[0] * f, lng_ref[...], lnb_ref[...])


def _combine(base, cnt, lpos, gates, x, g2, lng, lnb, ys, n_lat_tiles, n_tiles, tm, alpha):
    b, _, d = x.shape
    nb = g2.shape[0] - 1

    def mod_map(i, j):
        return (jnp.where(j < n_lat_tiles, i, nb), 0, 0)

    tok = pl.BlockSpec((1, tm, d), lambda i, j: (i, j, 0))
    vec = pl.BlockSpec((1, d), lambda i, j: (0, 0))
    seg = pl.BlockSpec((LANES,), lambda i, j: (i * n_tiles + j,), memory_space=pltpu.SMEM)
    per_tok = pl.BlockSpec((tm, LANES), lambda i, j: (i * n_tiles + j, 0))
    return pl.pallas_call(
        functools.partial(_combine_kernel, alpha=alpha),
        grid=(b, n_tiles),
        in_specs=[seg, seg, per_tok, per_tok,
                  tok, pl.BlockSpec((1, 1, d), mod_map), vec, vec,
                  pl.BlockSpec(memory_space=pl.ANY)],
        out_specs=tok,
        out_shape=jax.ShapeDtypeStruct((b, n_tiles * tm, d), F32),
        scratch_shapes=[pltpu.VMEM((_tile_slots(tm), d // 2), jnp.uint32), pltpu.SemaphoreType.DMA(())],
        compiler_params=_cparams(("arbitrary", "arbitrary")),
        name="moe_combine_ln",
    )(base, cnt, lpos, gates, x, g2, lng, lnb, ys)


def _moe_layer(h2, logits, x_res, g2, lng, lnb, layer, wgu, bgu, wdn, bdn, n_lat_tiles, n_tiles, tm, alpha):
    b, rows, d = h2.shape
    t = b * rows
    tb = 512
    _, gates, lpos, lpost, tiles, counts = _router(logits.reshape(t, LANES), tm)
    counts = counts[0, :N_EXPERTS].astype(jnp.int32)
    padded = (counts + tb - 1) // tb * tb
    pend = jnp.cumsum(padded)
    pstart = pend - padded
    tiles = tiles.reshape(t // tm, 8, LANES)[:, :2, :].astype(jnp.int32)
    seg_base = (jnp.pad(pstart, (0, LANES - N_EXPERTS))[None, :] + tiles[:, 0]).reshape(-1)
    seg_cnt = tiles[:, 1].reshape(-1)
    n_rows = t * TOP_K + (t // tm) * N_EXPERTS * (SEG_ALIGN - 1)
    n_rows = (n_rows + tb - 1) // tb * tb + N_EXPERTS * tb
    n_blocks = n_rows // tb
    block_start = jnp.arange(n_blocks, dtype=jnp.int32) * tb
    block_expert = jnp.minimum(jnp.sum((pend[None, :] <= block_start[:, None]).astype(jnp.int32), axis=1),
                               N_EXPERTS - 1).astype(jnp.int32)
    n_used = (pend[-1:] // tb).astype(jnp.int32)
    tail = jnp.pad(pstart + counts, (0, LANES - N_EXPERTS))
    tail_len = jnp.pad(padded - counts, (0, LANES - N_EXPERTS))
    xs = _dispatch(h2.reshape(t, d), seg_base, seg_cnt, tail, tail_len, lpost, n_rows, tm, tb)
    ys = _ffn(xs, n_rows, block_expert, n_used, layer, wgu, bgu, wdn, bdn, tb)
    return _combine(seg_base, seg_cnt, lpos, gates, x_res, g2, lng, lnb, ys, n_lat_tiles, n_tiles, tm, alpha)


def _rope_tables(n_lat_rows, n_rows, scale):
    pos = jnp.arange(n_lat_rows)
    lane = jnp.arange(LANES)
    m = lane % 64
    n_freq = 16
    inv = ROPE_BASE ** (-(m % n_freq).astype(F32) / n_freq)
    p = jnp.where((m // 32)[None, :] == 0, (pos // GRID_W)[:, None], (pos % GRID_W)[:, None]).astype(F32)
    ang = p * inv[None, :]
    sign = jnp.where((m % 32) < n_freq, -1.0, 1.0)[None, :]
    cos = jnp.concatenate([jnp.cos(ang), jnp.ones((n_rows - n_lat_rows, LANES), F32)], axis=0)
    sin = jnp.concatenate([jnp.sin(ang) * sign, jnp.zeros((n_rows - n_lat_rows, LANES), F32)], axis=0)
    return cos * scale, sin * scale


def kernel(x, c, ctx, c_ctx, w_ada, b_ada, ln_g, ln_b, hg_w_in, hg_lb, hg_norm_g, hg_w_out, da_w_in, da_lam,
           da_norm_g, da_w_out, moe_w_router, moe_b_router, moe_w_gu, moe_b_gu, moe_w_dn, moe_b_dn):
    bsz, seq, d = x.shape
    lc = ctx.shape[1]
    depth = w_ada.shape[0]
    assert depth == 2 and hg_w_in.shape[0] == 1 and da_w_in.shape[0] == 1
    ltot = seq + lc
    tm = 256
    assert seq % tm == 0 and lc % tm == 0 and seq % GRID_W == 0 and d == DA_HEADS * LANES
    alpha = (2 * depth) ** 0.25
    n_lat_tiles = seq // tm
    n_all_tiles = ltot // tm
    h_heads = d // LANES

    n_mod = bsz + 1
    c_pad = jnp.zeros((16, d), F32).at[:bsz].set(c).at[bsz].set(c_ctx)
    mod = _ada(c_pad, w_ada, b_ada)[:, :n_mod]

    def mods(l):
        return [mod[l, :, k * d:(k + 1) * d].reshape(n_mod, 1, d) for k in range(6)]

    def router_params(l):
        wr = jnp.zeros((d, LANES), F32).at[:, :N_EXPERTS].set(moe_w_router[l])
        wr_hi = wr.astype(BF16)
        wr_lo = (wr - wr_hi.astype(F32)).astype(BF16)
        br = jnp.full((1, LANES), -1e30, F32).at[0, :N_EXPERTS].set(moe_b_router[l])
        return jnp.stack([wr_hi, wr_lo]), br

    xall = jnp.concatenate([x, ctx], axis=1)

    sh1, sc1, g1, sh2, sc2, g2 = mods(0)
    p0 = _proj(xall, sc1, sh1, hg_w_in[0].astype(BF16), n_lat_tiles, tm)
    lb = jnp.cumsum(jax.nn.softmax(hg_lb.astype(F32), axis=1), axis=1)[:, 0]
    lb = lb.reshape(2, h_heads, LANES).transpose(1, 0, 2)
    y0 = _gla(p0, lb, hg_norm_g[0].reshape(h_heads, 1, LANES), seq)
    wr, br = router_params(0)
    x0, h20, lg0 = _post(y0, hg_w_out[0].astype(BF16), xall, g1, ln_g[0, 0].reshape(1, d),
                         ln_b[0, 0].reshape(1, d), sc2, sh2, wr, br, n_lat_tiles, n_all_tiles, tm, alpha)
    x1 = _moe_layer(h20, lg0, x0, g2, ln_g[0, 1].reshape(1, d), ln_b[0, 1].reshape(1, d),
                    0, moe_w_gu, moe_b_gu, moe_w_dn, moe_b_dn, n_lat_tiles, n_all_tiles, tm, alpha)

    sh1, sc1, g1, sh2, sc2, g2 = mods(1)
    dh = d // DA_HEADS // 2
    cq, sq = _rope_tables(seq, ltot, dh ** -0.5 * math.log2(math.e))
    ck, sk = _rope_tables(seq, ltot, 1.0)
    p1 = _proj(x1, sc1, sh1, da_w_in[0].astype(BF16), n_lat_tiles, tm, rope_tabs=(cq, sq, ck, sk))
    lam_init = 0.8 - 0.6 * math.exp(-0.3 * 1)
    lp = da_lam[0].astype(F32)
    lam = (jnp.exp(jnp.sum(lp[0] * lp[1])) - jnp.exp(jnp.sum(lp[2] * lp[3])) + lam_init).reshape(1)
    y1 = _attn(p1, lam, da_norm_g[0].reshape(LANES, 1), seq, 1.0 - lam_init)
    wr, br = router_params(1)
    x2, h21, lg1 = _post(y1, da_w_out[0].astype(BF16), x1, g1, ln_g[1, 0].reshape(1, d),
                         ln_b[1, 0].reshape(1, d), sc2, sh2, wr, br, n_lat_tiles, n_lat_tiles, tm, alpha)
    return _moe_layer(h21, lg1, x2, g2, ln_g[1, 1].reshape(1, d), ln_b[1, 1].reshape(1, d),
                      1, moe_w_gu, moe_b_gu, moe_w_dn, moe_b_dn, n_lat_tiles, n_lat_tiles, tm, alpha)
```

```python
import functools
import math

import jax
import jax.numpy as jnp
from jax import lax
from jax.experimental import pallas as pl
from jax.experimental.pallas import tpu as pltpu

F32 = jnp.float32
BF16 = jnp.bfloat16
HIGHEST = lax.Precision.HIGHEST

LANES = 128
HG_CHUNK = 64
GRID_W = 64
ROPE_BASE = 10000.0
DA_HEADS = 8
N_EXPERTS = 32
TOP_K = 4
SEG_ALIGN = 8
SWIGLU_LIMIT = 7.0
SWIGLU_ALPHA = 1.702
LN_EPS = 1e-5
RMS_EPS = 1e-6
VMEM_LIMIT = 56 * 1024 * 1024


def _cparams(sem):
    return pltpu.CompilerParams(dimension_semantics=sem, vmem_limit_bytes=VMEM_LIMIT)


def _ada_kernel(c_ref, w_ref, b_ref, o_ref):
    c = c_ref[...]
    s = c * jax.nn.sigmoid(c)
    o_ref[0] = jnp.dot(s, w_ref[0], precision=HIGHEST, preferred_element_type=F32) + b_ref[0]


def _ada(c_pad, w_ada, b_ada):
    depth, d, n = w_ada.shape
    rows = c_pad.shape[0]
    tn = 1536
    return pl.pallas_call(
        _ada_kernel,
        grid=(depth, n // tn),
        in_specs=[pl.BlockSpec((rows, d), lambda l, j: (0, 0)),
                  pl.BlockSpec((1, d, tn), lambda l, j: (l, 0, j)),
                  pl.BlockSpec((1, 1, tn), lambda l, j: (l, 0, j))],
        out_specs=pl.BlockSpec((1, rows, tn), lambda l, j: (l, 0, j)),
        out_shape=jax.ShapeDtypeStruct((depth, rows, n), F32),
        compiler_params=_cparams(("arbitrary", "arbitrary")),
        name="ada_mod",
    )(c_pad, w_ada, b_ada.reshape(depth, 1, n))


def _rope(piece, cos, sin, lo_mask):
    rot = jnp.where(lo_mask, pltpu.roll(piece, LANES - 16, 1), pltpu.roll(piece, 16, 1))
    return piece * cos + rot * sin


def _proj_kernel(x_ref, sc_ref, sh_ref, w_ref, *rest, n_rope, cw):
    if n_rope:
        cq_ref, sq_ref, ck_ref, sk_ref, o_ref = rest
    else:
        (o_ref,) = rest
    n = w_ref.shape[1]
    h = (x_ref[0] * (1.0 + sc_ref[0]) + sh_ref[0]).astype(BF16)
    if n_rope:
        lane = lax.broadcasted_iota(jnp.int32, (x_ref.shape[1], LANES), 1)
        lo_mask = (lane % 32) < 16
    per = cw // LANES
    for j in range(n // cw):
        r = jnp.dot(h, w_ref[:, j * cw:(j + 1) * cw], preferred_element_type=F32)
        for g in range(per):
            hd = j * per + g
            piece = r[:, g * LANES:(g + 1) * LANES]
            if hd < n_rope:
                piece = _rope(piece, cq_ref[...], sq_ref[...], lo_mask)
            elif hd < 2 * n_rope:
                piece = _rope(piece, ck_ref[...], sk_ref[...], lo_mask)
            o_ref[0, hd] = piece.astype(BF16)


def _proj(x, sc, sh, w, n_lat_tiles, tm, rope_tabs=None):
    b, ltot, d = x.shape
    n = w.shape[1]
    nb = sc.shape[0] - 1
    n_rope = DA_HEADS if rope_tabs is not None else 0

    def mod_map(i, j):
        return (jnp.where(j < n_lat_tiles, i, nb), 0, 0)

    in_specs = [pl.BlockSpec((1, tm, d), lambda i, j: (i, j, 0)),
                pl.BlockSpec((1, 1, d), mod_map),
                pl.BlockSpec((1, 1, d), mod_map),
                pl.BlockSpec((d, n), lambda i, j: (0, 0))]
    args = [x, sc, sh, w]
    if rope_tabs is not None:
        in_specs += [pl.BlockSpec((tm, LANES), lambda i, j: (j, 0))] * 4
        args += list(rope_tabs)
    return pl.pallas_call(
        functools.partial(_proj_kernel, n_rope=n_rope, cw=512),
        grid=(b, ltot // tm),
        in_specs=in_specs,
        out_specs=pl.BlockSpec((1, n // LANES, tm, LANES), lambda i, j: (i, 0, j, 0)),
        out_shape=jax.ShapeDtypeStruct((b, n // LANES, ltot, LANES), BF16),
        compiler_params=_cparams(("arbitrary", "arbitrary")),
        name="mod_proj_rope" if n_rope else "mod_proj",
    )(*args)


GLA_BLOCK = 256


def _gla_local(items):
    nc = GLA_BLOCK // HG_CHUNK
    kks, cats = [], []
    for (_, _, z, lb, _, _, _) in items:
        f = lb + (1.0 - lb) * jax.nn.sigmoid(z)
        logf = jnp.log(f)
        kks.append(1.0 - f)
        hi = logf.astype(BF16)
        lo = (logf - hi.astype(F32)).astype(BF16)
        cats.append(jnp.concatenate([hi, lo], axis=1))
    parts = [jnp.dot(it[4], cat, preferred_element_type=F32) for it, cat in zip(items, cats)]
    q_decs, k_invs, k_ends, decs = [], [], [], []
    for (q, _, _, _, _, _, end_row), kk, part in zip(items, kks, parts):
        bcum = part[:, :LANES] + part[:, LANES:]
        b_end = bcum.reshape(nc, HG_CHUNK, LANES)[:, end_row:end_row + 1, :]
        b_end_rows = jnp.broadcast_to(b_end, (nc, HG_CHUNK, LANES)).reshape(GLA_BLOCK, LANES)
        q_decs.append((q * jnp.exp(bcum)).astype(BF16))
        k_invs.append((kk * jnp.exp(-bcum)).astype(BF16))
        k_ends.append((kk * jnp.exp(b_end_rows - bcum)).astype(BF16))
        decs.append(jnp.exp(b_end.reshape(nc, LANES)))
    scores = [lax.dot_general(qd, ki, (((1,), (1,)), ((), ())), preferred_element_type=F32)
              for qd, ki in zip(q_decs, k_invs)]
    scores = [jnp.where(it[5], a, 0.0).astype(BF16) for it, a in zip(items, scores)]
    outs = [jnp.dot(a, it[1], preferred_element_type=F32) for it, a in zip(items, scores)]
    return list(zip(outs, q_decs, k_ends, decs))


def _gla_kernel(q_ref, i_ref, g_ref, zf_ref, zb_ref, lb_ref, ng_ref, y_ref,
                of_ref, ob_ref, qd_ref, ke_ref, de_ref, st_ref, *, n_lat, n_ctx):
    c = HG_CHUNK
    blk = GLA_BLOCK
    nc = blk // c
    ltot = (n_lat + n_ctx) * c
    row = lax.broadcasted_iota(jnp.int32, (blk, blk), 0)
    col = lax.broadcasted_iota(jnp.int32, (blk, blk), 1)
    same = (row // c) == (col // c)
    masks = (same & (col <= row), same & (col >= row))
    tris = (masks[0].astype(BF16), masks[1].astype(BF16))
    z_refs = (zf_ref, zb_ref)
    o_refs = (of_ref, ob_ref)
    end_rows = (c - 1, 0)

    n_blk = ltot // blk
    per_step = 3 if n_blk % 3 == 0 else 1

    def local(t, carry):
        work = []
        for u in range(per_step):
            r0 = pl.multiple_of((t * per_step + u) * blk, blk)
            c0 = pl.multiple_of((t * per_step + u) * nc, nc)
            q = q_ref[0, 0, pl.ds(r0, blk), :].astype(F32)
            v = i_ref[0, 0, pl.ds(r0, blk), :]
            zs = [z_refs[d][0, 0, pl.ds(r0, blk), :].astype(F32) for d in range(2)]
            work.append((r0, c0, q, v, zs))
        res = _gla_local([(q, v, zs[d], lb_ref[0, d:d + 1, :], tris[d], masks[d], end_rows[d])
                          for (_, _, q, v, zs) in work for d in range(2)])
        for u, (r0, c0, _, _, _) in enumerate(work):
            for d in range(2):
                o, q_dec, k_end, dec = res[2 * u + d]
                o_refs[d][pl.ds(r0, blk), :] = o
                qd_ref[d, pl.ds(r0, blk), :] = q_dec
                ke_ref[d, pl.ds(r0, blk), :] = k_end
                de_ref[d, pl.ds(c0, nc), :] = dec
        return carry
    lax.fori_loop(0, n_blk // per_step, local, 0)

    st_ref[...] = jnp.zeros_like(st_ref)

    def segment(first, n):
        per = 8 if n % 8 == 0 else (4 if n % 4 == 0 else 1)

        def body(jj, carry):
            todo = []
            for u in range(per):
                j = jj * per + u
                for d, ch in enumerate((first + j, first + n - 1 - j)):
                    r0 = pl.multiple_of(ch * c, c)
                    kv = lax.dot_general(i_ref[0, 0, pl.ds(r0, c), :], ke_ref[d, pl.ds(r0, c), :],
                                         (((0,), (0,)), ((), ())), preferred_element_type=F32)
                    todo.append((d, r0, kv, de_ref[d, pl.ds(ch, 1), :], qd_ref[d, pl.ds(r0, c), :]))
            st = [st_ref[0], st_ref[1]]
            inter = []
            for (d, r0, kv, dec, q_dec) in todo:
                inter.append((d, r0, lax.dot_general(q_dec, st[d].astype(BF16), (((1,), (1,)), ((), ())),
                                                     preferred_element_type=F32)))
                st[d] = st[d] * dec + kv
            for (d, r0, o) in inter:
                o_refs[d][pl.ds(r0, c), :] += o
            st_ref[0] = st[0]
            st_ref[1] = st[1]
            return carry
        lax.fori_loop(0, n // per, body, 0)

    segment(n_lat, n_ctx)
    segment(0, n_lat)

    ng = ng_ref[0]
    rt = 256

    def readout(t, carry):
        r0 = pl.multiple_of(t * rt, rt)
        o = of_ref[pl.ds(r0, rt), :] + ob_ref[pl.ds(r0, rt), :]
        ms = jnp.mean(o * o, axis=-1, keepdims=True)
        g = g_ref[0, 0, pl.ds(r0, rt), :].astype(F32)
        y = o * lax.rsqrt(ms + RMS_EPS) * ng * (g * jax.nn.sigmoid(g))
        y_ref[0, 0, pl.ds(r0, rt), :] = y.astype(BF16)
        return carry
    lax.fori_loop(0, ltot // rt, readout, 0)


def _gla(p, lb, ng, n_lat_rows):
    b, nh5, ltot, _ = p.shape
    h = nh5 // 5
    n_lat = n_lat_rows // HG_CHUNK
    n_ctx = (ltot - n_lat_rows) // HG_CHUNK

    def spec(k):
        return pl.BlockSpec((1, 1, ltot, LANES), lambda i, j, k=k: (i, k * h + j, 0, 0))

    return pl.pallas_call(
        functools.partial(_gla_kernel, n_lat=n_lat, n_ctx=n_ctx),
        grid=(b, h),
        in_specs=[spec(0), spec(1), spec(2), spec(3), spec(4),
                  pl.BlockSpec((1, 2, LANES), lambda i, j: (j, 0, 0)),
                  pl.BlockSpec((1, 1, LANES), lambda i, j: (j, 0, 0))],
        out_specs=pl.BlockSpec((1, 1, ltot, LANES), lambda i, j: (i, j, 0, 0)),
        out_shape=jax.ShapeDtypeStruct((b, h, ltot, LANES), BF16),
        scratch_shapes=[pltpu.VMEM((ltot, LANES), F32), pltpu.VMEM((ltot, LANES), F32),
                        pltpu.VMEM((2, ltot, LANES), BF16), pltpu.VMEM((2, ltot, LANES), BF16),
                        pltpu.VMEM((2, ltot // HG_CHUNK, LANES), F32),
                        pltpu.VMEM((2, LANES, LANES), F32)],
        compiler_params=_cparams(("arbitrary", "arbitrary")),
        name="hgrn2_gla",
    )(p, p, p, p, p, lb, ng)


SUM_ROWS = 16


def _attn_kernel(lam_ref, q_ref, k_ref, v_ref, ng_ref, y_ref, vt_ref, *, tk, out_scale):
    tq = q_ref.shape[2]
    lk = k_ref.shape[2]
    lam = lam_ref[0]

    @pl.when(pl.program_id(2) == 0)
    def _():
        blk = 256
        for cb in range(lk // blk):
            vt_ref[0:LANES, cb * blk:(cb + 1) * blk] = jnp.transpose(
                v_ref[0, 0, cb * blk:(cb + 1) * blk, :].astype(F32)).astype(BF16)
        vt_ref[LANES:LANES + SUM_ROWS, :] = jnp.ones((SUM_ROWS, lk), BF16)

    q = q_ref[0, 0]
    lane = lax.broadcasted_iota(jnp.int32, (tq, LANES), 1)
    zero = jnp.zeros_like(q)
    qm = (jnp.where(lane < 64, q, zero), jnp.where(lane >= 64, q, zero))

    def scores(ci, c):
        kc = k_ref[0, 0, ci * tk:(ci + 1) * tk, :]
        return lax.dot_general(kc, qm[c], (((1,), (1,)), ((), ())), preferred_element_type=F32)

    n_chunks = lk // tk
    m = [jnp.full((1, tq), -1e30, F32) for _ in range(2)]
    acc = [jnp.zeros((LANES + SUM_ROWS, tq), F32) for _ in range(2)]
    s_next = [scores(0, 0), scores(0, 1)]
    for ci in range(n_chunks):
        vt = vt_ref[:, ci * tk:(ci + 1) * tk]
        for c in range(2):
            s_cur = s_next[c]
            m_new = jnp.maximum(m[c], jnp.max(s_cur, axis=0, keepdims=True))
            if ci + 1 < n_chunks:
                s_next[c] = scores(ci + 1, c)
            p = jnp.exp2(s_cur - m_new).astype(BF16)
            acc[c] = jnp.exp2(m[c] - m_new) * acc[c] + jnp.dot(vt, p, preferred_element_type=F32)
            m[c] = m_new
    a0, a1 = acc
    o = a0[:LANES] / a0[LANES:LANES + 1] - lam * (a1[:LANES] / a1[LANES:LANES + 1])
    ms = jnp.mean(o * o, axis=0, keepdims=True)
    y = o * lax.rsqrt(ms + RMS_EPS) * ng_ref[...] * out_scale
    y_ref[0, 0] = jnp.transpose(y).astype(BF16)


def _attn(p, lam, ng, n_lat_rows, out_scale, tq=512, tk=768):
    b, nh3, ltot, _ = p.shape
    h = nh3 // 3
    assert ltot % tk == 0 and n_lat_rows % tq == 0 and ltot % 256 == 0
    return pl.pallas_call(
        functools.partial(_attn_kernel, tk=tk, out_scale=out_scale),
        grid=(b, h, n_lat_rows // tq),
        in_specs=[pl.BlockSpec(memory_space=pltpu.SMEM),
                  pl.BlockSpec((1, 1, tq, LANES), lambda i, j, t: (i, j, t, 0)),
                  pl.BlockSpec((1, 1, ltot, LANES), lambda i, j, t: (i, h + j, 0, 0)),
                  pl.BlockSpec((1, 1, ltot, LANES), lambda i, j, t: (i, 2 * h + j, 0, 0)),
                  pl.BlockSpec((LANES, 1), lambda i, j, t: (0, 0))],
        out_specs=pl.BlockSpec((1, 1, tq, LANES), lambda i, j, t: (i, j, t, 0)),
        out_shape=jax.ShapeDtypeStruct((b, h, n_lat_rows, LANES), BF16),
        scratch_shapes=[pltpu.VMEM((LANES + SUM_ROWS, ltot), BF16)],
        compiler_params=_cparams(("arbitrary", "arbitrary", "arbitrary")),
        name="diff_attn",
    )(lam, p, p, p, ng)


def _layer_norm(x, g, b):
    mu = jnp.mean(x, axis=-1, keepdims=True)
    xc = x - mu
    var = jnp.mean(xc * xc, axis=-1, keepdims=True)
    return xc * lax.rsqrt(var + LN_EPS) * g + b


def _post_kernel(y_ref, w_ref, x_ref, g1_ref, lng_ref, lnb_ref, sc2_ref, sh2_ref, wr_ref, br_ref,
                 xo_ref, h2_ref, lg_ref, *, alpha):
    nh = y_ref.shape[1]
    y = jnp.concatenate([y_ref[0, k] for k in range(nh)], axis=-1)
    m = jnp.dot(y, w_ref[...], preferred_element_type=F32)
    xl = _layer_norm(alpha * x_ref[0] + g1_ref[0] * m, lng_ref[...], lnb_ref[...])
    xo_ref[0] = xl
    h2 = xl * (1.0 + sc2_ref[0]) + sh2_ref[0]
    h_hi = h2.astype(BF16)
    h2_ref[0] = h_hi
    h_lo = (h2 - h_hi.astype(F32)).astype(BF16)
    lg_ref[0] = (jnp.dot(h_hi, wr_ref[0], preferred_element_type=F32)
                 + jnp.dot(h_lo, wr_ref[0], preferred_element_type=F32)
                 + jnp.dot(h_hi, wr_ref[1], preferred_element_type=F32) + br_ref[...])


def _post(y, w_out, x, g1, lng, lnb, sc2, sh2, wr, br, n_lat_tiles, n_tiles, tm, alpha):
    b, nh, _, _ = y.shape
    d = x.shape[2]
    nb = g1.shape[0] - 1
    rows = n_tiles * tm

    def mod_map(i, j):
        return (jnp.where(j < n_lat_tiles, i, nb), 0, 0)

    tok = pl.BlockSpec((1, tm, d), lambda i, j: (i, j, 0))
    vec = pl.BlockSpec((1, d), lambda i, j: (0, 0))
    return pl.pallas_call(
        functools.partial(_post_kernel, alpha=alpha),
        grid=(b, n_tiles),
        in_specs=[pl.BlockSpec((1, nh, tm, LANES), lambda i, j: (i, 0, j, 0)),
                  pl.BlockSpec((d, d), lambda i, j: (0, 0)),
                  tok, pl.BlockSpec((1, 1, d), mod_map), vec, vec,
                  pl.BlockSpec((1, 1, d), mod_map), pl.BlockSpec((1, 1, d), mod_map),
                  pl.BlockSpec((2, d, LANES), lambda i, j: (0, 0, 0)),
                  pl.BlockSpec((1, LANES), lambda i, j: (0, 0))],
        out_specs=[tok, tok, pl.BlockSpec((1, tm, LANES), lambda i, j: (i, j, 0))],
        out_shape=[jax.ShapeDtypeStruct((b, rows, d), F32),
                   jax.ShapeDtypeStruct((b, rows, d), BF16),
                   jax.ShapeDtypeStruct((b, rows, LANES), F32)],
        compiler_params=_cparams(("arbitrary", "arbitrary")),
        name="out_proj_ln",
    )(y, w_out, x, g1, lng, lnb, sc2, sh2, wr, br)


def _router_kernel(lg_ref, idx_ref, gate_ref, lpos_ref, lpost_ref, tile_ref, cnt_ref, carry_ref):
    tm = lg_ref.shape[0]

    @pl.when(pl.program_id(0) == 0)
    def _():
        carry_ref[...] = jnp.zeros_like(carry_ref)

    work = lg_ref[...]
    lane = lax.broadcasted_iota(jnp.int32, (tm, LANES), 1)
    lane_f = lane.astype(F32)
    vals, onehots = [], []
    idx_out = jnp.zeros((tm, LANES), jnp.int32)
    for k in range(TOP_K):
        mx = jnp.max(work, axis=-1, keepdims=True)
        am = jnp.min(jnp.where(work == mx, lane_f, float(LANES)), axis=-1, keepdims=True)
        oh = lane_f == am
        vals.append(mx)
        onehots.append(oh)
        idx_out = jnp.where(lane == k, am.astype(jnp.int32), idx_out)
        work = jnp.where(oh, -3e38, work)
    es = [jnp.exp(v - vals[0]) for v in vals]
    denom = es[0] + es[1] + es[2] + es[3]
    gate_out = jnp.zeros((tm, LANES), F32)
    for k in range(TOP_K):
        gate_out = jnp.where(lane == k, es[k] / denom, gate_out)
    member = (onehots[0] | onehots[1] | onehots[2] | onehots[3])
    r = lax.broadcasted_iota(jnp.int32, (tm, tm), 0)
    c = lax.broadcasted_iota(jnp.int32, (tm, tm), 1)
    strict = (c < r).astype(BF16)
    before = jnp.dot(strict, member.astype(BF16), preferred_element_type=F32)
    cnt_tile = jnp.sum(member.astype(F32), axis=0, keepdims=True)
    cnt_tile = jnp.floor((cnt_tile + (SEG_ALIGN - 1)) * (1.0 / SEG_ALIGN)) * SEG_ALIGN
    ei = lax.broadcasted_iota(jnp.int32, (LANES, LANES), 0)
    ej = lax.broadcasted_iota(jnp.int32, (LANES, LANES), 1)
    seg_start = jnp.dot(jnp.broadcast_to(cnt_tile, (8, LANES)).astype(BF16), (ei < ej).astype(BF16),
                        preferred_element_type=F32)[0:1]
    slot = seg_start + before
    lpos_out = jnp.zeros((tm, LANES), jnp.int32)
    for k in range(TOP_K):
        lp = jnp.sum(jnp.where(onehots[k], slot, 0.0), axis=-1, keepdims=True)
        lpos_out = jnp.where(lane == k, lp.astype(jnp.int32), lpos_out)
    idx_ref[...] = idx_out
    gate_ref[...] = gate_out
    lpos_ref[...] = lpos_out
    lpost_ref[...] = jnp.transpose(lpos_out.astype(F32))[0:8, :].astype(jnp.int32)
    row8 = lax.broadcasted_iota(jnp.int32, (8, LANES), 0)
    tile_ref[...] = jnp.where(row8 == 0, carry_ref[...], jnp.where(row8 == 1, cnt_tile, 0.0))
    carry_ref[...] = carry_ref[...] + cnt_tile
    cnt_ref[...] = carry_ref[...]


def _router(logits, tm):
    t = logits.shape[0]
    tile = pl.BlockSpec((tm, LANES), lambda i: (i, 0))
    return pl.pallas_call(
        _router_kernel,
        grid=(t // tm,),
        in_specs=[tile],
        out_specs=[tile, tile, tile, pl.BlockSpec((8, tm), lambda i: (0, i)),
                   pl.BlockSpec((8, LANES), lambda i: (i, 0)), pl.BlockSpec((1, LANES), lambda i: (0, 0))],
        out_shape=[jax.ShapeDtypeStruct((t, LANES), jnp.int32),
                   jax.ShapeDtypeStruct((t, LANES), F32),
                   jax.ShapeDtypeStruct((t, LANES), jnp.int32),
                   jax.ShapeDtypeStruct((8, t), jnp.int32),
                   jax.ShapeDtypeStruct((t // tm * 8, LANES), F32),
                   jax.ShapeDtypeStruct((1, LANES), F32)],
        scratch_shapes=[pltpu.VMEM((1, LANES), F32)],
        compiler_params=_cparams(("arbitrary",)),
        name="router_topk",
    )(logits)


HI_MASK = 0xFFFF0000


def _pack_rows(x):
    half = x.shape[1] // 2
    bits = lax.bitcast_convert_type(x.astype(BF16).astype(F32), jnp.uint32)
    return (bits[:, :half] & jnp.uint32(HI_MASK)) | (bits[:, half:] >> 16)


def _unpack_rows(p):
    left = lax.bitcast_convert_type(p & jnp.uint32(HI_MASK), F32).astype(BF16)
    right = lax.bitcast_convert_type(p << 16, F32).astype(BF16)
    return left, right


def _copy_pieces(n, local_row, global_row, copy, max_rows, wait=False):
    for b in range(SEG_ALIGN.bit_length() - 1, max_rows.bit_length()):
        size = 1 << b

        @pl.when((n & size) != 0)
        def _():
            lo = n & (size - 1)
            piece = copy(pl.multiple_of(local_row + lo, SEG_ALIGN), pl.multiple_of(global_row + lo, SEG_ALIGN),
                         size)
            if wait:
                piece.wait()
            else:
                piece.start()


def _segment_copies(base_ref, cnt_ref, copy, max_rows, n_slots, fill_row):
    def per_expert(e, off):
        _copy_pieces(cnt_ref[e], off, base_ref[e], copy, max_rows)
        return off + cnt_ref[e]
    used = lax.fori_loop(0, N_EXPERTS, per_expert, 0)
    _copy_pieces(n_slots - used, used, fill_row, copy, max_rows)


def _tile_slots(tm):
    return tm * TOP_K + N_EXPERTS * SEG_ALIGN


def _dispatch_kernel(base_ref, cnt_ref, tail_ref, tail_len_ref, lpost_ref, h_ref, xs_ref,
                     cbuf_ref, zbuf_ref, sem, zsem, *, spare_row):
    tm = h_ref.shape[0]

    @pl.when(pl.program_id(0) == pl.num_programs(0) - 1)
    def _():
        zbuf_ref[...] = jnp.zeros_like(zbuf_ref)

        def zero_copy(src, dst, size):
            return pltpu.make_async_copy(zbuf_ref.at[pl.ds(src, size)], xs_ref.at[pl.ds(dst, size)], zsem)
        for wait in (False, True):
            def per_expert(e, carry, wait=wait):
                _copy_pieces(tail_len_ref[e], 0, tail_ref[e], zero_copy, zbuf_ref.shape[0], wait=wait)
                return carry
            lax.fori_loop(0, N_EXPERTS, per_expert, 0)

    n_slots = _tile_slots(tm)
    slot = lax.broadcasted_iota(jnp.int32, (n_slots, tm), 0)
    lp = lpost_ref[...]
    sel = slot == lp[0:1, :]
    for k in range(1, TOP_K):
        sel = sel | (slot == lp[k:k + 1, :])
    rows = jnp.dot(sel.astype(BF16), h_ref[...].astype(BF16), preferred_element_type=F32)
    step = pl.program_id(0)
    cur = step % 2
    cbuf_ref[cur] = _pack_rows(rows)

    def copy_from(buf):
        def copy(src, dst, size):
            return pltpu.make_async_copy(cbuf_ref.at[buf, pl.ds(src, size)], xs_ref.at[pl.ds(dst, size)],
                                         sem.at[buf])
        return copy
    _segment_copies(base_ref, cnt_ref, copy_from(cur), tm, n_slots, spare_row + cur * tm)

    @pl.when(step > 0)
    def _():
        copy_from(1 - cur)(0, 0, n_slots).wait()

    @pl.when(step == pl.num_programs(0) - 1)
    def _():
        copy_from(cur)(0, 0, n_slots).wait()


def _dispatch(h2, base, cnt, tail, tail_len, lpost, n_rows, tm, tb):
    t, d = h2.shape
    seg = pl.BlockSpec((LANES,), lambda i: (i,), memory_space=pltpu.SMEM)
    whole = pl.BlockSpec(memory_space=pltpu.SMEM)
    return pl.pallas_call(
        functools.partial(_dispatch_kernel, spare_row=n_rows),
        grid=(t // tm,),
        in_specs=[seg, seg, whole, whole,
                  pl.BlockSpec((8, tm), lambda i: (0, i)),
                  pl.BlockSpec((tm, d), lambda i: (i, 0))],
        out_specs=pl.BlockSpec(memory_space=pl.ANY),
        out_shape=jax.ShapeDtypeStruct((n_rows + 2 * tm, d // 2), jnp.uint32),
        scratch_shapes=[pltpu.VMEM((2, _tile_slots(tm), d // 2), jnp.uint32),
                        pltpu.VMEM((tb, d // 2), jnp.uint32),
                        pltpu.SemaphoreType.DMA((2,)), pltpu.SemaphoreType.DMA(())],
        compiler_params=_cparams(("arbitrary",)),
        name="moe_dispatch",
    )(base, cnt, tail, tail_len, lpost, h2)


def _ffn_kernel(be_ref, nu_ref, x_ref, wgu_ref, bgu_ref, wdn_ref, bdn_ref, y_ref, wgu_bf_ref, wdn_bf_ref):
    de = wdn_ref.shape[2]
    i = pl.program_id(0)
    used = i < nu_ref[0]

    @pl.when(used & ((i == 0) | (be_ref[i] != be_ref[jnp.maximum(i - 1, 0)])))
    def _():
        wgu_bf_ref[...] = wgu_ref[0, 0].astype(BF16)
        wdn_bf_ref[...] = wdn_ref[0, 0].astype(BF16)

    @pl.when(used)
    def _():
        half = x_ref.shape[1]
        x_left, x_right = _unpack_rows(x_ref[...])
        gu = (jnp.dot(x_left, wgu_bf_ref[:half, :], preferred_element_type=F32)
              + jnp.dot(x_right, wgu_bf_ref[half:, :], preferred_element_type=F32) + bgu_ref[0, 0])
        gate = jnp.minimum(gu[:, :de], SWIGLU_LIMIT)
        up = jnp.clip(gu[:, de:], -SWIGLU_LIMIT, SWIGLU_LIMIT)
        act = (up + 1.0) * gate * jax.nn.sigmoid(SWIGLU_ALPHA * gate)
        y = jnp.dot(act.astype(BF16), wdn_bf_ref[...], preferred_element_type=F32) + bdn_ref[0, 0]
        y_ref[...] = _pack_rows(y)

    @pl.when(jnp.logical_not(used))
    def _():
        y_ref[...] = jnp.zeros_like(y_ref)


def _ffn(xs, n_rows, block_expert, n_used, layer, wgu, bgu, wdn, bdn, tb):
    p, half = n_rows, xs.shape[1]
    d = 2 * half
    nl, e, _, n2 = wgu.shape
    de = wdn.shape[2]
    grid_spec = pltpu.PrefetchScalarGridSpec(
        num_scalar_prefetch=2,
        grid=(p // tb,),
        in_specs=[pl.BlockSpec((tb, half), lambda i, be, nu: (i, 0)),
                  pl.BlockSpec((1, 1, d, n2), lambda i, be, nu: (layer, be[i], 0, 0)),
                  pl.BlockSpec((1, 1, 1, n2), lambda i, be, nu: (layer, be[i], 0, 0)),
                  pl.BlockSpec((1, 1, de, d), lambda i, be, nu: (layer, be[i], 0, 0)),
                  pl.BlockSpec((1, 1, 1, d), lambda i, be, nu: (layer, be[i], 0, 0))],
        out_specs=pl.BlockSpec((tb, half), lambda i, be, nu: (i, 0)),
        scratch_shapes=[pltpu.VMEM((d, n2), BF16), pltpu.VMEM((de, d), BF16)],
    )
    return pl.pallas_call(
        _ffn_kernel,
        grid_spec=grid_spec,
        out_shape=jax.ShapeDtypeStruct((p, half), jnp.uint32),
        compiler_params=_cparams(("arbitrary",)),
        name="moe_ffn",
    )(block_expert, n_used, xs, wgu, bgu.reshape(nl, e, 1, n2), wdn, bdn.reshape(nl, e, 1, d))


def _combine_kernel(base_ref, cnt_ref, base_next_ref, cnt_next_ref, lpos_ref, gate_ref, x_ref, g2_ref,
                    lng_ref, lnb_ref, ys_ref, o_ref, gbuf_ref, sem, *, alpha):
    tm = x_ref.shape[1]
    n_slots = _tile_slots(tm)
    step = pl.program_id(0) * pl.num_programs(1) + pl.program_id(1)
    n_steps = pl.num_programs(0) * pl.num_programs(1)
    cur = step % 2

    def gather(bases, cnts, buf):
        def copy(dst, src, size):
            return pltpu.make_async_copy(ys_ref.at[pl.ds(src, size)], gbuf_ref.at[buf, pl.ds(dst, size)],
                                         sem.at[buf])
        _segment_copies(bases, cnts, copy, tm, n_slots, 0)
        return copy

    @pl.when(step == 0)
    def _():
        gather(base_ref, cnt_ref, cur)

    @pl.when(step + 1 < n_steps)
    def _():
        gather(base_next_ref, cnt_next_ref, 1 - cur)

    def copy(dst, src, size):
        return pltpu.make_async_copy(ys_ref.at[pl.ds(src, size)], gbuf_ref.at[cur, pl.ds(dst, size)],
                                     sem.at[cur])

    lane = lax.broadcasted_iota(jnp.int32, (tm, n_slots), 1)
    lpos = lpos_ref[...]
    gate = gate_ref[...]
    w = jnp.zeros((tm, n_slots), F32)
    for k in range(TOP_K):
        w = jnp.where(lane == lpos[:, k:k + 1], gate[:, k:k + 1], w)
    w_hi = w.astype(BF16)
    w_lo = (w - w_hi.astype(F32)).astype(BF16)

    copy(0, 0, n_slots).wait()
    y_left, y_right = _unpack_rows(gbuf_ref[cur])
    f = jnp.concatenate(
        [jnp.dot(w_hi, y, preferred_element_type=F32) + jnp.dot(w_lo, y, preferred_element_type=F32)
         for y in (y_left, y_right)], axis=1)
    o_ref[0] = _layer_norm(alpha * x_ref[0] + g2_ref[0] * f, lng_ref[...], lnb_ref[...])


def _combine(base, cnt, lpos, gates, x, g2, lng, lnb, ys, n_lat_tiles, n_tiles, tm, alpha):
    b, _, d = x.shape
    nb = g2.shape[0] - 1

    def mod_map(i, j):
        return (jnp.where(j < n_lat_tiles, i, nb), 0, 0)

    tok = pl.BlockSpec((1, tm, d), lambda i, j: (i, j, 0))
    vec = pl.BlockSpec((1, d), lambda i, j: (0, 0))
    last = b * n_tiles - 1
    seg = pl.BlockSpec((LANES,), lambda i, j: (i * n_tiles + j,), memory_space=pltpu.SMEM)
    seg_next = pl.BlockSpec((LANES,), lambda i, j: (jnp.minimum(i * n_tiles + j + 1, last),),
                            memory_space=pltpu.SMEM)
    per_tok = pl.BlockSpec((tm, LANES), lambda i, j: (i * n_tiles + j, 0))
    return pl.pallas_call(
        functools.partial(_combine_kernel, alpha=alpha),
        grid=(b, n_tiles),
        in_specs=[seg, seg, seg_next, seg_next, per_tok, per_tok,
                  tok, pl.BlockSpec((1, 1, d), mod_map), vec, vec,
                  pl.BlockSpec(memory_space=pl.ANY)],
        out_specs=tok,
        out_shape=jax.ShapeDtypeStruct((b, n_tiles * tm, d), F32),
        scratch_shapes=[pltpu.VMEM((2, _tile_slots(tm), d // 2), jnp.uint32), pltpu.SemaphoreType.DMA((2,))],
        compiler_params=_cparams(("arbitrary", "arbitrary")),
        name="moe_combine_ln",
    )(base, cnt, base, cnt, lpos, gates, x, g2, lng, lnb, ys)


def _moe_layer(h2, logits, x_res, g2, lng, lnb, layer, wgu, bgu, wdn, bdn, n_lat_tiles, n_tiles, tm, alpha):
    b, rows, d = h2.shape
    t = b * rows
    tb = 512
    _, gates, lpos, lpost, tiles, counts = _router(logits.reshape(t, LANES), tm)
    counts = counts[0, :N_EXPERTS].astype(jnp.int32)
    padded = (counts + tb - 1) // tb * tb
    pend = jnp.cumsum(padded)
    pstart = pend - padded
    tiles = tiles.reshape(t // tm, 8, LANES)[:, :2, :].astype(jnp.int32)
    seg_base = (jnp.pad(pstart, (0, LANES - N_EXPERTS))[None, :] + tiles[:, 0]).reshape(-1)
    seg_cnt = tiles[:, 1].reshape(-1)
    n_rows = t * TOP_K + (t // tm) * N_EXPERTS * (SEG_ALIGN - 1)
    n_rows = (n_rows + tb - 1) // tb * tb + N_EXPERTS * tb
    n_blocks = n_rows // tb
    block_start = jnp.arange(n_blocks, dtype=jnp.int32) * tb
    block_expert = jnp.minimum(jnp.sum((pend[None, :] <= block_start[:, None]).astype(jnp.int32), axis=1),
                               N_EXPERTS - 1).astype(jnp.int32)
    n_used = (pend[-1:] // tb).astype(jnp.int32)
    tail = jnp.pad(pstart + counts, (0, LANES - N_EXPERTS))
    tail_len = jnp.pad(padded - counts, (0, LANES - N_EXPERTS))
    xs = _dispatch(h2.reshape(t, d), seg_base, seg_cnt, tail, tail_len, lpost, n_rows, tm, tb)
    ys = _ffn(xs, n_rows, block_expert, n_used, layer, wgu, bgu, wdn, bdn, tb)
    return _combine(seg_base, seg_cnt, lpos, gates, x_res, g2, lng, lnb, ys, n_lat_tiles, n_tiles, tm, alpha)


def _rope_tables(n_lat_rows, n_rows, scale):
    pos = jnp.arange(n_lat_rows)
    lane = jnp.arange(LANES)
    m = lane % 64
    n_freq = 16
    inv = ROPE_BASE ** (-(m % n_freq).astype(F32) / n_freq)
    p = jnp.where((m // 32)[None, :] == 0, (pos // GRID_W)[:, None], (pos % GRID_W)[:, None]).astype(F32)
    ang = p * inv[None, :]
    sign = jnp.where((m % 32) < n_freq, -1.0, 1.0)[None, :]
    cos = jnp.concatenate([jnp.cos(ang), jnp.ones((n_rows - n_lat_rows, LANES), F32)], axis=0)
    sin = jnp.concatenate([jnp.sin(ang) * sign, jnp.zeros((n_rows - n_lat_rows, LANES), F32)], axis=0)
    return cos * scale, sin * scale


def kernel(x, c, ctx, c_ctx, w_ada, b_ada, ln_g, ln_b, hg_w_in, hg_lb, hg_norm_g, hg_w_out, da_w_in, da_lam,
           da_norm_g, da_w_out, moe_w_router, moe_b_router, moe_w_gu, moe_b_gu, moe_w_dn, moe_b_dn):
    bsz, seq, d = x.shape
    lc = ctx.shape[1]
    depth = w_ada.shape[0]
    assert depth == 2 and hg_w_in.shape[0] == 1 and da_w_in.shape[0] == 1
    ltot = seq + lc
    tm = 256
    assert seq % tm == 0 and lc % tm == 0 and seq % GRID_W == 0 and d == DA_HEADS * LANES
    alpha = (2 * depth) ** 0.25
    n_lat_tiles = seq // tm
    n_all_tiles = ltot // tm
    h_heads = d // LANES

    n_mod = bsz + 1
    c_pad = jnp.zeros((16, d), F32).at[:bsz].set(c).at[bsz].set(c_ctx)
    mod = _ada(c_pad, w_ada, b_ada)[:, :n_mod]

    def mods(l):
        return [mod[l, :, k * d:(k + 1) * d].reshape(n_mod, 1, d) for k in range(6)]

    def router_params(l):
        wr = jnp.zeros((d, LANES), F32).at[:, :N_EXPERTS].set(moe_w_router[l])
        wr_hi = wr.astype(BF16)
        wr_lo = (wr - wr_hi.astype(F32)).astype(BF16)
        br = jnp.full((1, LANES), -1e30, F32).at[0, :N_EXPERTS].set(moe_b_router[l])
        return jnp.stack([wr_hi, wr_lo]), br

    xall = jnp.concatenate([x, ctx], axis=1)

    sh1, sc1, g1, sh2, sc2, g2 = mods(0)
    p0 = _proj(xall, sc1, sh1, hg_w_in[0].astype(BF16), n_lat_tiles, tm)
    lb = jnp.cumsum(jax.nn.softmax(hg_lb.astype(F32), axis=1), axis=1)[:, 0]
    lb = lb.reshape(2, h_heads, LANES).transpose(1, 0, 2)
    y0 = _gla(p0, lb, hg_norm_g[0].reshape(h_heads, 1, LANES), seq)
    wr, br = router_params(0)
    x0, h20, lg0 = _post(y0, hg_w_out[0].astype(BF16), xall, g1, ln_g[0, 0].reshape(1, d),
                         ln_b[0, 0].reshape(1, d), sc2, sh2, wr, br, n_lat_tiles, n_all_tiles, tm, alpha)
    x1 = _moe_layer(h20, lg0, x0, g2, ln_g[0, 1].reshape(1, d), ln_b[0, 1].reshape(1, d),
                    0, moe_w_gu, moe_b_gu, moe_w_dn, moe_b_dn, n_lat_tiles, n_all_tiles, tm, alpha)

    sh1, sc1, g1, sh2, sc2, g2 = mods(1)
    dh = d // DA_HEADS // 2
    cq, sq = _rope_tables(seq, ltot, dh ** -0.5 * math.log2(math.e))
    ck, sk = _rope_tables(seq, ltot, 1.0)
    p1 = _proj(x1, sc1, sh1, da_w_in[0].astype(BF16), n_lat_tiles, tm, rope_tabs=(cq, sq, ck, sk))
    lam_init = 0.8 - 0.6 * math.exp(-0.3 * 1)
    lp = da_lam[0].astype(F32)
    lam = (jnp.exp(jnp.sum(lp[0] * lp[1])) - jnp.exp(jnp.sum(lp[2] * lp[3])) + lam_init).reshape(1)
    y1 = _attn(p1, lam, da_norm_g[0].reshape(LANES, 1), seq, 1.0 - lam_init)
    wr, br = router_params(1)
    x2, h21, lg1 = _post(y1, da_w_out[0].astype(BF16), x1, g1, ln_g[1, 0].reshape(1, d),
                         ln_b[1, 0].reshape(1, d), sc2, sh2, wr, br, n_lat_tiles, n_lat_tiles, tm, alpha)
    return _moe_layer(h21, lg1, x2, g2, ln_g[1, 1].reshape(1, d), ln_b[1, 1].reshape(1, d),
                      1, moe_w_gu, moe_b_gu, moe_w_dn, moe_b_dn, n_lat_tiles, n_lat_tiles, tm, alpha)
```

```python
import functools
import math

import jax
import jax.numpy as jnp
from jax import lax
from jax.experimental import pallas as pl
from jax.experimental.pallas import tpu as pltpu

F32 = jnp.float32
BF16 = jnp.bfloat16
HIGHEST = lax.Precision.HIGHEST

LANES = 128
HG_CHUNK = 64
GRID_W = 64
ROPE_BASE = 10000.0
DA_HEADS = 8
N_EXPERTS = 32
TOP_K = 4
SEG_ALIGN = 8
SWIGLU_LIMIT = 7.0
SWIGLU_ALPHA = 1.702
LN_EPS = 1e-5
RMS_EPS = 1e-6
VMEM_LIMIT = 56 * 1024 * 1024


def _cparams(sem):
    return pltpu.CompilerParams(dimension_semantics=sem, vmem_limit_bytes=VMEM_LIMIT)


def _ada_kernel(c_ref, w_ref, b_ref, o_ref):
    c = c_ref[...]
    s = c * jax.nn.sigmoid(c)
    o_ref[0] = jnp.dot(s, w_ref[0], precision=HIGHEST, preferred_element_type=F32) + b_ref[0]


def _ada(c_pad, w_ada, b_ada):
    depth, d, n = w_ada.shape
    rows = c_pad.shape[0]
    tn = 1536
    return pl.pallas_call(
        _ada_kernel,
        grid=(depth, n // tn),
        in_specs=[pl.BlockSpec((rows, d), lambda l, j: (0, 0)),
                  pl.BlockSpec((1, d, tn), lambda l, j: (l, 0, j)),
                  pl.BlockSpec((1, 1, tn), lambda l, j: (l, 0, j))],
        out_specs=pl.BlockSpec((1, rows, tn), lambda l, j: (l, 0, j)),
        out_shape=jax.ShapeDtypeStruct((depth, rows, n), F32),
        compiler_params=_cparams(("arbitrary", "arbitrary")),
        name="ada_mod",
    )(c_pad, w_ada, b_ada.reshape(depth, 1, n))


def _rope(piece, cos, sin, lo_mask):
    rot = jnp.where(lo_mask, pltpu.roll(piece, LANES - 16, 1), pltpu.roll(piece, 16, 1))
    return piece * cos + rot * sin


def _proj_kernel(x_ref, sc_ref, sh_ref, w_ref, *rest, n_rope, cw):
    if n_rope:
        cq_ref, sq_ref, ck_ref, sk_ref, o_ref = rest
    else:
        (o_ref,) = rest
    n = w_ref.shape[1]
    h = (x_ref[0] * (1.0 + sc_ref[0]) + sh_ref[0]).astype(BF16)
    if n_rope:
        lane = lax.broadcasted_iota(jnp.int32, (x_ref.shape[1], LANES), 1)
        lo_mask = (lane % 32) < 16
    per = cw // LANES
    for j in range(n // cw):
        r = jnp.dot(h, w_ref[:, j * cw:(j + 1) * cw], preferred_element_type=F32)
        for g in range(per):
            hd = j * per + g
            piece = r[:, g * LANES:(g + 1) * LANES]
            if hd < n_rope:
                piece = _rope(piece, cq_ref[...], sq_ref[...], lo_mask)
            elif hd < 2 * n_rope:
                piece = _rope(piece, ck_ref[...], sk_ref[...], lo_mask)
            o_ref[0, hd] = piece.astype(BF16)


def _proj(x, sc, sh, w, n_lat_tiles, tm, rope_tabs=None):
    b, ltot, d = x.shape
    n = w.shape[1]
    nb = sc.shape[0] - 1
    n_rope = DA_HEADS if rope_tabs is not None else 0

    def mod_map(i, j):
        return (jnp.where(j < n_lat_tiles, i, nb), 0, 0)

    in_specs = [pl.BlockSpec((1, tm, d), lambda i, j: (i, j, 0)),
                pl.BlockSpec((1, 1, d), mod_map),
                pl.BlockSpec((1, 1, d), mod_map),
                pl.BlockSpec((d, n), lambda i, j: (0, 0))]
    args = [x, sc, sh, w]
    if rope_tabs is not None:
        in_specs += [pl.BlockSpec((tm, LANES), lambda i, j: (j, 0))] * 4
        args += list(rope_tabs)
    return pl.pallas_call(
        functools.partial(_proj_kernel, n_rope=n_rope, cw=512),
        grid=(b, ltot // tm),
        in_specs=in_specs,
        out_specs=pl.BlockSpec((1, n // LANES, tm, LANES), lambda i, j: (i, 0, j, 0)),
        out_shape=jax.ShapeDtypeStruct((b, n // LANES, ltot, LANES), BF16),
        compiler_params=_cparams(("arbitrary", "arbitrary")),
        name="mod_proj_rope" if n_rope else "mod_proj",
    )(*args)


GLA_BLOCK = 256


def _gla_local(items):
    nc = GLA_BLOCK // HG_CHUNK
    kks, cats = [], []
    for (_, _, z, lb, _, _, _) in items:
        f = lb + (1.0 - lb) * jax.nn.sigmoid(z)
        logf = jnp.log(f)
        kks.append(1.0 - f)
        hi = logf.astype(BF16)
        lo = (logf - hi.astype(F32)).astype(BF16)
        cats.append(jnp.concatenate([hi, lo], axis=1))
    parts = [jnp.dot(it[4], cat, preferred_element_type=F32) for it, cat in zip(items, cats)]
    q_decs, k_invs, k_ends, decs = [], [], [], []
    for (q, _, _, _, _, _, end_row), kk, part in zip(items, kks, parts):
        bcum = part[:, :LANES] + part[:, LANES:]
        b_end = bcum.reshape(nc, HG_CHUNK, LANES)[:, end_row:end_row + 1, :]
        b_end_rows = jnp.broadcast_to(b_end, (nc, HG_CHUNK, LANES)).reshape(GLA_BLOCK, LANES)
        q_decs.append((q * jnp.exp(bcum)).astype(BF16))
        k_invs.append((kk * jnp.exp(-bcum)).astype(BF16))
        k_ends.append((kk * jnp.exp(b_end_rows - bcum)).astype(BF16))
        decs.append(jnp.exp(b_end.reshape(nc, LANES)))
    scores = [lax.dot_general(qd, ki, (((1,), (1,)), ((), ())), preferred_element_type=F32)
              for qd, ki in zip(q_decs, k_invs)]
    scores = [jnp.where(it[5], a, 0.0).astype(BF16) for it, a in zip(items, scores)]
    outs = [jnp.dot(a, it[1], preferred_element_type=F32) for it, a in zip(items, scores)]
    return list(zip(outs, q_decs, k_ends, decs))


def _gla_kernel(q_ref, i_ref, g_ref, zf_ref, zb_ref, lb_ref, ng_ref, y_ref,
                of_ref, ob_ref, qd_ref, ke_ref, de_ref, st_ref, *, n_lat, n_ctx):
    c = HG_CHUNK
    blk = GLA_BLOCK
    nc = blk // c
    ltot = (n_lat + n_ctx) * c
    row = lax.broadcasted_iota(jnp.int32, (blk, blk), 0)
    col = lax.broadcasted_iota(jnp.int32, (blk, blk), 1)
    same = (row // c) == (col // c)
    masks = (same & (col <= row), same & (col >= row))
    tris = (masks[0].astype(BF16), masks[1].astype(BF16))
    z_refs = (zf_ref, zb_ref)
    o_refs = (of_ref, ob_ref)
    end_rows = (c - 1, 0)

    n_blk = ltot // blk
    per_step = 3 if n_blk % 3 == 0 else 1

    def local(t, carry):
        work = []
        for u in range(per_step):
            r0 = pl.multiple_of((t * per_step + u) * blk, blk)
            c0 = pl.multiple_of((t * per_step + u) * nc, nc)
            q = q_ref[0, 0, pl.ds(r0, blk), :].astype(F32)
            v = i_ref[0, 0, pl.ds(r0, blk), :]
            zs = [z_refs[d][0, 0, pl.ds(r0, blk), :].astype(F32) for d in range(2)]
            work.append((r0, c0, q, v, zs))
        res = _gla_local([(q, v, zs[d], lb_ref[0, d:d + 1, :], tris[d], masks[d], end_rows[d])
                          for (_, _, q, v, zs) in work for d in range(2)])
        for u, (r0, c0, _, _, _) in enumerate(work):
            for d in range(2):
                o, q_dec, k_end, dec = res[2 * u + d]
                o_refs[d][pl.ds(r0, blk), :] = o
                qd_ref[d, pl.ds(r0, blk), :] = q_dec
                ke_ref[d, pl.ds(r0, blk), :] = k_end
                de_ref[d, pl.ds(c0, nc), :] = dec
        return carry
    lax.fori_loop(0, n_blk // per_step, local, 0)

    st_ref[...] = jnp.zeros_like(st_ref)

    def segment(first, n):
        per = 8 if n % 8 == 0 else (4 if n % 4 == 0 else 1)

        def body(jj, carry):
            todo = []
            for u in range(per):
                j = jj * per + u
                for d, ch in enumerate((first + j, first + n - 1 - j)):
                    r0 = pl.multiple_of(ch * c, c)
                    kv = lax.dot_general(i_ref[0, 0, pl.ds(r0, c), :], ke_ref[d, pl.ds(r0, c), :],
                                         (((0,), (0,)), ((), ())), preferred_element_type=F32)
                    todo.append((d, r0, kv, de_ref[d, pl.ds(ch, 1), :], qd_ref[d, pl.ds(r0, c), :]))
            st = [st_ref[0], st_ref[1]]
            inter = []
            for (d, r0, kv, dec, q_dec) in todo:
                inter.append((d, r0, lax.dot_general(q_dec, st[d].astype(BF16), (((1,), (1,)), ((), ())),
                                                     preferred_element_type=F32)))
                st[d] = st[d] * dec + kv
            for (d, r0, o) in inter:
                o_refs[d][pl.ds(r0, c), :] += o
            st_ref[0] = st[0]
            st_ref[1] = st[1]
            return carry
        lax.fori_loop(0, n // per, body, 0)

    segment(n_lat, n_ctx)
    segment(0, n_lat)

    ng = ng_ref[0]
    rt = 256

    def readout(t, carry):
        r0 = pl.multiple_of(t * rt, rt)
        o = of_ref[pl.ds(r0, rt), :] + ob_ref[pl.ds(r0, rt), :]
        ms = jnp.mean(o * o, axis=-1, keepdims=True)
        g = g_ref[0, 0, pl.ds(r0, rt), :].astype(F32)
        y = o * lax.rsqrt(ms + RMS_EPS) * ng * (g * jax.nn.sigmoid(g))
        y_ref[0, 0, pl.ds(r0, rt), :] = y.astype(BF16)
        return carry
    lax.fori_loop(0, ltot // rt, readout, 0)


def _gla(p, lb, ng, n_lat_rows):
    b, nh5, ltot, _ = p.shape
    h = nh5 // 5
    n_lat = n_lat_rows // HG_CHUNK
    n_ctx = (ltot - n_lat_rows) // HG_CHUNK

    def spec(k):
        return pl.BlockSpec((1, 1, ltot, LANES), lambda i, j, k=k: (i, k * h + j, 0, 0))

    return pl.pallas_call(
        functools.partial(_gla_kernel, n_lat=n_lat, n_ctx=n_ctx),
        grid=(b, h),
        in_specs=[spec(0), spec(1), spec(2), spec(3), spec(4),
                  pl.BlockSpec((1, 2, LANES), lambda i, j: (j, 0, 0)),
                  pl.BlockSpec((1, 1, LANES), lambda i, j: (j, 0, 0))],
        out_specs=pl.BlockSpec((1, 1, ltot, LANES), lambda i, j: (i, j, 0, 0)),
        out_shape=jax.ShapeDtypeStruct((b, h, ltot, LANES), BF16),
        scratch_shapes=[pltpu.VMEM((ltot, LANES), F32), pltpu.VMEM((ltot, LANES), F32),
                        pltpu.VMEM((2, ltot, LANES), BF16), pltpu.VMEM((2, ltot, LANES), BF16),
                        pltpu.VMEM((2, ltot // HG_CHUNK, LANES), F32),
                        pltpu.VMEM((2, LANES, LANES), F32)],
        compiler_params=_cparams(("arbitrary", "arbitrary")),
        name="hgrn2_gla",
    )(p, p, p, p, p, lb, ng)


SUM_ROWS = 16


BOUND_SLACK = 1.0 + 2.0 ** -5
MIN_SOFTMAX_SUM = 1e-30


def _attn_kernel(lam_ref, q_ref, k_ref, v_ref, ng_ref, y_ref, vt_ref, kmax_ref, acc_ref, *, tk, out_scale):
    tq = q_ref.shape[2]
    lk = k_ref.shape[2]
    lam = lam_ref[0]
    n_chunks = lk // tk
    srow = lax.broadcasted_iota(jnp.int32, (8, LANES), 0)
    slane = lax.broadcasted_iota(jnp.int32, (8, LANES), 1)
    sel = (srow == slane // 64).astype(BF16)

    def sq_norms(x):
        return lax.dot_general(sel, x * x, (((1,), (1,)), ((), ())), preferred_element_type=F32)

    @pl.when(pl.program_id(2) == 0)
    def _():
        blk = 256
        kn = jnp.zeros((8, blk), F32)
        for cb in range(lk // blk):
            vt_ref[0:LANES, cb * blk:(cb + 1) * blk] = jnp.transpose(
                v_ref[0, 0, cb * blk:(cb + 1) * blk, :].astype(F32)).astype(BF16)
            kn = jnp.maximum(kn, sq_norms(k_ref[0, 0, cb * blk:(cb + 1) * blk, :]))
        vt_ref[LANES:LANES + SUM_ROWS, :] = jnp.ones((SUM_ROWS, lk), BF16)
        kmax_ref[...] = jnp.broadcast_to(jnp.sqrt(jnp.max(kn, axis=1, keepdims=True)), kmax_ref.shape)

    q = q_ref[0, 0]
    lane = lax.broadcasted_iota(jnp.int32, (tq, LANES), 1)
    zero = jnp.zeros_like(q)
    qm = (jnp.where(lane < 64, q, zero), jnp.where(lane >= 64, q, zero))

    def scores(ci, c):
        kc = k_ref[0, 0, ci * tk:(ci + 1) * tk, :]
        return lax.dot_general(kc, qm[c], (((1,), (1,)), ((), ())), preferred_element_type=F32)

    bound = jnp.sqrt(sq_norms(q)) * kmax_ref[...] * BOUND_SLACK
    acc = [jnp.zeros((LANES + SUM_ROWS, tq), F32) for _ in range(2)]
    for ci in range(n_chunks):
        vt = vt_ref[:, ci * tk:(ci + 1) * tk]
        for c in range(2):
            p = jnp.exp2(scores(ci, c) - bound[c:c + 1]).astype(BF16)
            acc[c] = acc[c] + jnp.dot(vt, p, preferred_element_type=F32)
    acc_ref[0] = acc[0]
    acc_ref[1] = acc[1]
    sums = jnp.minimum(acc[0][LANES:LANES + 1], acc[1][LANES:LANES + 1])
    underflow = jnp.logical_not(jnp.min(sums) > MIN_SOFTMAX_SUM)

    @pl.when(underflow)
    def _():
        def body(ci, carry):
            r0 = pl.multiple_of(ci * tk, tk)
            kc = k_ref[0, 0, pl.ds(r0, tk), :]
            vt = vt_ref[:, pl.ds(r0, tk)]
            out = []
            for c in range(2):
                m_c, a_c = carry[c]
                s_c = lax.dot_general(kc, qm[c], (((1,), (1,)), ((), ())), preferred_element_type=F32)
                m_new = jnp.maximum(m_c, jnp.max(s_c, axis=0, keepdims=True))
                p = jnp.exp2(s_c - m_new).astype(BF16)
                out.append((m_new, jnp.exp2(m_c - m_new) * a_c + jnp.dot(vt, p, preferred_element_type=F32)))
            return tuple(out)
        init = tuple((jnp.full((1, tq), -1e30, F32), jnp.zeros((LANES + SUM_ROWS, tq), F32))
                     for _ in range(2))
        (_, b0), (_, b1) = lax.fori_loop(0, n_chunks, body, init)
        acc_ref[0] = b0
        acc_ref[1] = b1

    a0 = acc_ref[0]
    a1 = acc_ref[1]
    o = a0[:LANES] / a0[LANES:LANES + 1] - lam * (a1[:LANES] / a1[LANES:LANES + 1])
    ms = jnp.mean(o * o, axis=0, keepdims=True)
    y = o * lax.rsqrt(ms + RMS_EPS) * ng_ref[...] * out_scale
    y_ref[0, 0] = jnp.transpose(y).astype(BF16)


def _attn(p, lam, ng, n_lat_rows, out_scale, tq=256, tk=768):
    b, nh3, ltot, _ = p.shape
    h = nh3 // 3
    assert ltot % tk == 0 and n_lat_rows % tq == 0 and ltot % 256 == 0
    return pl.pallas_call(
        functools.partial(_attn_kernel, tk=tk, out_scale=out_scale),
        grid=(b, h, n_lat_rows // tq),
        in_specs=[pl.BlockSpec(memory_space=pltpu.SMEM),
                  pl.BlockSpec((1, 1, tq, LANES), lambda i, j, t: (i, j, t, 0)),
                  pl.BlockSpec((1, 1, ltot, LANES), lambda i, j, t: (i, h + j, 0, 0)),
                  pl.BlockSpec((1, 1, ltot, LANES), lambda i, j, t: (i, 2 * h + j, 0, 0)),
                  pl.BlockSpec((LANES, 1), lambda i, j, t: (0, 0))],
        out_specs=pl.BlockSpec((1, 1, tq, LANES), lambda i, j, t: (i, j, t, 0)),
        out_shape=jax.ShapeDtypeStruct((b, h, n_lat_rows, LANES), BF16),
        scratch_shapes=[pltpu.VMEM((LANES + SUM_ROWS, ltot), BF16), pltpu.VMEM((8, tq), F32),
                        pltpu.VMEM((2, LANES + SUM_ROWS, tq), F32)],
        compiler_params=_cparams(("arbitrary", "arbitrary", "arbitrary")),
        name="diff_attn",
    )(lam, p, p, p, ng)


def _layer_norm(x, g, b):
    mu = jnp.mean(x, axis=-1, keepdims=True)
    xc = x - mu
    var = jnp.mean(xc * xc, axis=-1, keepdims=True)
    return xc * lax.rsqrt(var + LN_EPS) * g + b


def _post_kernel(y_ref, w_ref, x_ref, g1_ref, lng_ref, lnb_ref, sc2_ref, sh2_ref, wr_ref, br_ref,
                 xo_ref, h2_ref, lg_ref, *, alpha):
    nh = y_ref.shape[1]
    y = jnp.concatenate([y_ref[0, k] for k in range(nh)], axis=-1)
    m = jnp.dot(y, w_ref[...], preferred_element_type=F32)
    xl = _layer_norm(alpha * x_ref[0] + g1_ref[0] * m, lng_ref[...], lnb_ref[...])
    xo_ref[0] = xl
    h2 = xl * (1.0 + sc2_ref[0]) + sh2_ref[0]
    h_hi = h2.astype(BF16)
    h2_ref[0] = h_hi
    h_lo = (h2 - h_hi.astype(F32)).astype(BF16)
    lg_ref[0] = (jnp.dot(h_hi, wr_ref[0], preferred_element_type=F32)
                 + jnp.dot(h_lo, wr_ref[0], preferred_element_type=F32)
                 + jnp.dot(h_hi, wr_ref[1], preferred_element_type=F32) + br_ref[...])


def _post(y, w_out, x, g1, lng, lnb, sc2, sh2, wr, br, n_lat_tiles, n_tiles, tm, alpha):
    b, nh, _, _ = y.shape
    d = x.shape[2]
    nb = g1.shape[0] - 1
    rows = n_tiles * tm

    def mod_map(i, j):
        return (jnp.where(j < n_lat_tiles, i, nb), 0, 0)

    tok = pl.BlockSpec((1, tm, d), lambda i, j: (i, j, 0))
    vec = pl.BlockSpec((1, d), lambda i, j: (0, 0))
    return pl.pallas_call(
        functools.partial(_post_kernel, alpha=alpha),
        grid=(b, n_tiles),
        in_specs=[pl.BlockSpec((1, nh, tm, LANES), lambda i, j: (i, 0, j, 0)),
                  pl.BlockSpec((d, d), lambda i, j: (0, 0)),
                  tok, pl.BlockSpec((1, 1, d), mod_map), vec, vec,
                  pl.BlockSpec((1, 1, d), mod_map), pl.BlockSpec((1, 1, d), mod_map),
                  pl.BlockSpec((2, d, LANES), lambda i, j: (0, 0, 0)),
                  pl.BlockSpec((1, LANES), lambda i, j: (0, 0))],
        out_specs=[tok, tok, pl.BlockSpec((1, tm, LANES), lambda i, j: (i, j, 0))],
        out_shape=[jax.ShapeDtypeStruct((b, rows, d), F32),
                   jax.ShapeDtypeStruct((b, rows, d), BF16),
                   jax.ShapeDtypeStruct((b, rows, LANES), F32)],
        compiler_params=_cparams(("arbitrary", "arbitrary")),
        name="out_proj_ln",
    )(y, w_out, x, g1, lng, lnb, sc2, sh2, wr, br)


def _router_kernel(lg_ref, idx_ref, gate_ref, lpos_ref, lpost_ref, tile_ref, cnt_ref, carry_ref):
    tm = lg_ref.shape[0]

    @pl.when(pl.program_id(0) == 0)
    def _():
        carry_ref[...] = jnp.zeros_like(carry_ref)

    work = lg_ref[...]
    lane = lax.broadcasted_iota(jnp.int32, (tm, LANES), 1)
    lane_f = lane.astype(F32)
    vals, onehots = [], []
    idx_out = jnp.zeros((tm, LANES), jnp.int32)
    for k in range(TOP_K):
        mx = jnp.max(work, axis=-1, keepdims=True)
        am = jnp.min(jnp.where(work == mx, lane_f, float(LANES)), axis=-1, keepdims=True)
        oh = lane_f == am
        vals.append(mx)
        onehots.append(oh)
        idx_out = jnp.where(lane == k, am.astype(jnp.int32), idx_out)
        work = jnp.where(oh, -3e38, work)
    es = [jnp.exp(v - vals[0]) for v in vals]
    denom = es[0] + es[1] + es[2] + es[3]
    gate_out = jnp.zeros((tm, LANES), F32)
    for k in range(TOP_K):
        gate_out = jnp.where(lane == k, es[k] / denom, gate_out)
    member = (onehots[0] | onehots[1] | onehots[2] | onehots[3])
    r = lax.broadcasted_iota(jnp.int32, (tm, tm), 0)
    c = lax.broadcasted_iota(jnp.int32, (tm, tm), 1)
    strict = (c < r).astype(BF16)
    before = jnp.dot(strict, member.astype(BF16), preferred_element_type=F32)
    cnt_tile = jnp.sum(member.astype(F32), axis=0, keepdims=True)
    cnt_tile = jnp.floor((cnt_tile + (SEG_ALIGN - 1)) * (1.0 / SEG_ALIGN)) * SEG_ALIGN
    ei = lax.broadcasted_iota(jnp.int32, (LANES, LANES), 0)
    ej = lax.broadcasted_iota(jnp.int32, (LANES, LANES), 1)
    seg_start = jnp.dot(jnp.broadcast_to(cnt_tile, (8, LANES)).astype(BF16), (ei < ej).astype(BF16),
                        preferred_element_type=F32)[0:1]
    slot = seg_start + before
    lpos_out = jnp.zeros((tm, LANES), jnp.int32)
    for k in range(TOP_K):
        lp = jnp.sum(jnp.where(onehots[k], slot, 0.0), axis=-1, keepdims=True)
        lpos_out = jnp.where(lane == k, lp.astype(jnp.int32), lpos_out)
    idx_ref[...] = idx_out
    gate_ref[...] = gate_out
    lpos_ref[...] = lpos_out
    lpost_ref[...] = jnp.transpose(lpos_out.astype(F32))[0:8, :].astype(jnp.int32)
    row8 = lax.broadcasted_iota(jnp.int32, (8, LANES), 0)
    tile_ref[...] = jnp.where(row8 == 0, carry_ref[...], jnp.where(row8 == 1, cnt_tile, 0.0))
    carry_ref[...] = carry_ref[...] + cnt_tile
    cnt_ref[...] = carry_ref[...]


def _router(logits, tm):
    t = logits.shape[0]
    tile = pl.BlockSpec((tm, LANES), lambda i: (i, 0))
    return pl.pallas_call(
        _router_kernel,
        grid=(t // tm,),
        in_specs=[tile],
        out_specs=[tile, tile, tile, pl.BlockSpec((8, tm), lambda i: (0, i)),
                   pl.BlockSpec((8, LANES), lambda i: (i, 0)), pl.BlockSpec((1, LANES), lambda i: (0, 0))],
        out_shape=[jax.ShapeDtypeStruct((t, LANES), jnp.int32),
                   jax.ShapeDtypeStruct((t, LANES), F32),
                   jax.ShapeDtypeStruct((t, LANES), jnp.int32),
                   jax.ShapeDtypeStruct((8, t), jnp.int32),
                   jax.ShapeDtypeStruct((t // tm * 8, LANES), F32),
                   jax.ShapeDtypeStruct((1, LANES), F32)],
        scratch_shapes=[pltpu.VMEM((1, LANES), F32)],
        compiler_params=_cparams(("arbitrary",)),
        name="router_topk",
    )(logits)


HI_MASK = 0xFFFF0000


def _pack_rows(x):
    half = x.shape[1] // 2
    bits = lax.bitcast_convert_type(x.astype(BF16).astype(F32), jnp.uint32)
    return (bits[:, :half] & jnp.uint32(HI_MASK)) | (bits[:, half:] >> 16)


def _unpack_rows(p):
    left = lax.bitcast_convert_type(p & jnp.uint32(HI_MASK), F32).astype(BF16)
    right = lax.bitcast_convert_type(p << 16, F32).astype(BF16)
    return left, right


def _copy_pieces(n, local_row, global_row, copy, max_rows, wait=False):
    for b in range(SEG_ALIGN.bit_length() - 1, max_rows.bit_length()):
        size = 1 << b

        @pl.when((n & size) != 0)
        def _():
            lo = n & (size - 1)
            piece = copy(pl.multiple_of(local_row + lo, SEG_ALIGN), pl.multiple_of(global_row + lo, SEG_ALIGN),
                         size)
            if wait:
                piece.wait()
            else:
                piece.start()


def _segment_copies(base_ref, cnt_ref, copy, max_rows, n_slots, fill_row):
    def per_expert(e, off):
        _copy_pieces(cnt_ref[e], off, base_ref[e], copy, max_rows)
        return off + cnt_ref[e]
    used = lax.fori_loop(0, N_EXPERTS, per_expert, 0)
    _copy_pieces(n_slots - used, used, fill_row, copy, max_rows)


def _tile_slots(tm):
    return tm * TOP_K + N_EXPERTS * SEG_ALIGN


def _dispatch_kernel(base_ref, cnt_ref, tail_ref, tail_len_ref, lpost_ref, h_ref, xs_ref,
                     cbuf_ref, zbuf_ref, sem, zsem, *, spare_row):
    tm = h_ref.shape[0]

    @pl.when(pl.program_id(0) == pl.num_programs(0) - 1)
    def _():
        zbuf_ref[...] = jnp.zeros_like(zbuf_ref)

        def zero_copy(src, dst, size):
            return pltpu.make_async_copy(zbuf_ref.at[pl.ds(src, size)], xs_ref.at[pl.ds(dst, size)], zsem)
        for wait in (False, True):
            def per_expert(e, carry, wait=wait):
                _copy_pieces(tail_len_ref[e], 0, tail_ref[e], zero_copy, zbuf_ref.shape[0], wait=wait)
                return carry
            lax.fori_loop(0, N_EXPERTS, per_expert, 0)

    n_slots = _tile_slots(tm)
    slot = lax.broadcasted_iota(jnp.int32, (n_slots, tm), 0)
    lp = lpost_ref[...]
    sel = slot == lp[0:1, :]
    for k in range(1, TOP_K):
        sel = sel | (slot == lp[k:k + 1, :])
    rows = jnp.dot(sel.astype(BF16), h_ref[...].astype(BF16), preferred_element_type=F32)
    step = pl.program_id(0)
    cur = step % 2
    cbuf_ref[cur] = _pack_rows(rows)

    def copy_from(buf):
        def copy(src, dst, size):
            return pltpu.make_async_copy(cbuf_ref.at[buf, pl.ds(src, size)], xs_ref.at[pl.ds(dst, size)],
                                         sem.at[buf])
        return copy
    _segment_copies(base_ref, cnt_ref, copy_from(cur), tm, n_slots, spare_row + cur * tm)

    @pl.when(step > 0)
    def _():
        copy_from(1 - cur)(0, 0, n_slots).wait()

    @pl.when(step == pl.num_programs(0) - 1)
    def _():
        copy_from(cur)(0, 0, n_slots).wait()


def _dispatch(h2, base, cnt, tail, tail_len, lpost, n_rows, tm, tb):
    t, d = h2.shape
    seg = pl.BlockSpec((LANES,), lambda i: (i,), memory_space=pltpu.SMEM)
    whole = pl.BlockSpec(memory_space=pltpu.SMEM)
    return pl.pallas_call(
        functools.partial(_dispatch_kernel, spare_row=n_rows),
        grid=(t // tm,),
        in_specs=[seg, seg, whole, whole,
                  pl.BlockSpec((8, tm), lambda i: (0, i)),
                  pl.BlockSpec((tm, d), lambda i: (i, 0))],
        out_specs=pl.BlockSpec(memory_space=pl.ANY),
        out_shape=jax.ShapeDtypeStruct((n_rows + 2 * tm, d // 2), jnp.uint32),
        scratch_shapes=[pltpu.VMEM((2, _tile_slots(tm), d // 2), jnp.uint32),
                        pltpu.VMEM((tb, d // 2), jnp.uint32),
                        pltpu.SemaphoreType.DMA((2,)), pltpu.SemaphoreType.DMA(())],
        compiler_params=_cparams(("arbitrary",)),
        name="moe_dispatch",
    )(base, cnt, tail, tail_len, lpost, h2)


def _ffn_kernel(be_ref, nu_ref, x_ref, wgu_ref, bgu_ref, wdn_ref, bdn_ref, y_ref, wgu_bf_ref, wdn_bf_ref):
    de = wdn_ref.shape[2]
    i = pl.program_id(0)
    used = i < nu_ref[0]

    @pl.when(used & ((i == 0) | (be_ref[i] != be_ref[jnp.maximum(i - 1, 0)])))
    def _():
        wgu_bf_ref[...] = wgu_ref[0, 0].astype(BF16)
        wdn_bf_ref[...] = wdn_ref[0, 0].astype(BF16)

    @pl.when(used)
    def _():
        half = x_ref.shape[1]
        x_left, x_right = _unpack_rows(x_ref[...])
        gu = (jnp.dot(x_left, wgu_bf_ref[:half, :], preferred_element_type=F32)
              + jnp.dot(x_right, wgu_bf_ref[half:, :], preferred_element_type=F32) + bgu_ref[0, 0])
        gate = jnp.minimum(gu[:, :de], SWIGLU_LIMIT)
        up = jnp.clip(gu[:, de:], -SWIGLU_LIMIT, SWIGLU_LIMIT)
        act = (up + 1.0) * gate * jax.nn.sigmoid(SWIGLU_ALPHA * gate)
        y = jnp.dot(act.astype(BF16), wdn_bf_ref[...], preferred_element_type=F32) + bdn_ref[0, 0]
        y_ref[...] = _pack_rows(y)

    @pl.when(jnp.logical_not(used))
    def _():
        y_ref[...] = jnp.zeros_like(y_ref)


def _ffn(xs, n_rows, block_expert, n_used, layer, wgu, bgu, wdn, bdn, tb):
    p, half = n_rows, xs.shape[1]
    d = 2 * half
    nl, e, _, n2 = wgu.shape
    de = wdn.shape[2]
    grid_spec = pltpu.PrefetchScalarGridSpec(
        num_scalar_prefetch=2,
        grid=(p // tb,),
        in_specs=[pl.BlockSpec((tb, half), lambda i, be, nu: (i, 0)),
                  pl.BlockSpec((1, 1, d, n2), lambda i, be, nu: (layer, be[i], 0, 0)),
                  pl.BlockSpec((1, 1, 1, n2), lambda i, be, nu: (layer, be[i], 0, 0)),
                  pl.BlockSpec((1, 1, de, d), lambda i, be, nu: (layer, be[i], 0, 0)),
                  pl.BlockSpec((1, 1, 1, d), lambda i, be, nu: (layer, be[i], 0, 0))],
        out_specs=pl.BlockSpec((tb, half), lambda i, be, nu: (i, 0)),
        scratch_shapes=[pltpu.VMEM((d, n2), BF16), pltpu.VMEM((de, d), BF16)],
    )
    return pl.pallas_call(
        _ffn_kernel,
        grid_spec=grid_spec,
        out_shape=jax.ShapeDtypeStruct((p, half), jnp.uint32),
        compiler_params=_cparams(("arbitrary",)),
        name="moe_ffn",
    )(block_expert, n_used, xs, wgu, bgu.reshape(nl, e, 1, n2), wdn, bdn.reshape(nl, e, 1, d))


def _combine_kernel(base_ref, cnt_ref, base_next_ref, cnt_next_ref, lpos_ref, gate_ref, x_ref, g2_ref,
                    lng_ref, lnb_ref, ys_ref, o_ref, gbuf_ref, sem, *, alpha):
    tm = x_ref.shape[1]
    n_slots = _tile_slots(tm)
    step = pl.program_id(0) * pl.num_programs(1) + pl.program_id(1)
    n_steps = pl.num_programs(0) * pl.num_programs(1)
    cur = step % 2

    def gather(bases, cnts, buf):
        def copy(dst, src, size):
            return pltpu.make_async_copy(ys_ref.at[pl.ds(src, size)], gbuf_ref.at[buf, pl.ds(dst, size)],
                                         sem.at[buf])
        _segment_copies(bases, cnts, copy, tm, n_slots, 0)
        return copy

    @pl.when(step == 0)
    def _():
        gather(base_ref, cnt_ref, cur)

    @pl.when(step + 1 < n_steps)
    def _():
        gather(base_next_ref, cnt_next_ref, 1 - cur)

    def copy(dst, src, size):
        return pltpu.make_async_copy(ys_ref.at[pl.ds(src, size)], gbuf_ref.at[cur, pl.ds(dst, size)],
                                     sem.at[cur])

    lane = lax.broadcasted_iota(jnp.int32, (tm, n_slots), 1)
    lpos = lpos_ref[...]
    gate = gate_ref[...]
    w = jnp.zeros((tm, n_slots), F32)
    for k in range(TOP_K):
        w = jnp.where(lane == lpos[:, k:k + 1], gate[:, k:k + 1], w)
    w_hi = w.astype(BF16)
    w_lo = (w - w_hi.astype(F32)).astype(BF16)

    copy(0, 0, n_slots).wait()
    y_left, y_right = _unpack_rows(gbuf_ref[cur])
    f = jnp.concatenate(
        [jnp.dot(w_hi, y, preferred_element_type=F32) + jnp.dot(w_lo, y, preferred_element_type=F32)
         for y in (y_left, y_right)], axis=1)
    o_ref[0] = _layer_norm(alpha * x_ref[0] + g2_ref[0] * f, lng_ref[...], lnb_ref[...])


def _combine(base, cnt, lpos, gates, x, g2, lng, lnb, ys, n_lat_tiles, n_tiles, tm, alpha):
    b, _, d = x.shape
    nb = g2.shape[0] - 1

    def mod_map(i, j):
        return (jnp.where(j < n_lat_tiles, i, nb), 0, 0)

    tok = pl.BlockSpec((1, tm, d), lambda i, j: (i, j, 0))
    vec = pl.BlockSpec((1, d), lambda i, j: (0, 0))
    last = b * n_tiles - 1
    seg = pl.BlockSpec((LANES,), lambda i, j: (i * n_tiles + j,), memory_space=pltpu.SMEM)
    seg_next = pl.BlockSpec((LANES,), lambda i, j: (jnp.minimum(i * n_tiles + j + 1, last),),
                            memory_space=pltpu.SMEM)
    per_tok = pl.BlockSpec((tm, LANES), lambda i, j: (i * n_tiles + j, 0))
    return pl.pallas_call(
        functools.partial(_combine_kernel, alpha=alpha),
        grid=(b, n_tiles),
        in_specs=[seg, seg, seg_next, seg_next, per_tok, per_tok,
                  tok, pl.BlockSpec((1, 1, d), mod_map), vec, vec,
                  pl.BlockSpec(memory_space=pl.ANY)],
        out_specs=tok,
        out_shape=jax.ShapeDtypeStruct((b, n_tiles * tm, d), F32),
        scratch_shapes=[pltpu.VMEM((2, _tile_slots(tm), d // 2), jnp.uint32), pltpu.SemaphoreType.DMA((2,))],
        compiler_params=_cparams(("arbitrary", "arbitrary")),
        name="moe_combine_ln",
    )(base, cnt, base, cnt, lpos, gates, x, g2, lng, lnb, ys)


def _moe_layer(h2, logits, x_res, g2, lng, lnb, layer, wgu, bgu, wdn, bdn, n_lat_tiles, n_tiles, tm, alpha):
    b, rows, d = h2.shape
    t = b * rows
    tb = 512
    _, gates, lpos, lpost, tiles, counts = _router(logits.reshape(t, LANES), tm)
    counts = counts[0, :N_EXPERTS].astype(jnp.int32)
    padded = (counts + tb - 1) // tb * tb
    pend = jnp.cumsum(padded)
    pstart = pend - padded
    tiles = tiles.reshape(t // tm, 8, LANES)[:, :2, :].astype(jnp.int32)
    seg_base = (jnp.pad(pstart, (0, LANES - N_EXPERTS))[None, :] + tiles[:, 0]).reshape(-1)
    seg_cnt = tiles[:, 1].reshape(-1)
    n_rows = t * TOP_K + (t // tm) * N_EXPERTS * (SEG_ALIGN - 1)
    n_rows = (n_rows + tb - 1) // tb * tb + N_EXPERTS * tb
    n_blocks = n_rows // tb
    block_start = jnp.arange(n_blocks, dtype=jnp.int32) * tb
    block_expert = jnp.minimum(jnp.sum((pend[None, :] <= block_start[:, None]).astype(jnp.int32), axis=1),
                               N_EXPERTS - 1).astype(jnp.int32)
    n_used = (pend[-1:] // tb).astype(jnp.int32)
    tail = jnp.pad(pstart + counts, (0, LANES - N_EXPERTS))
    tail_len = jnp.pad(padded - counts, (0, LANES - N_EXPERTS))
    xs = _dispatch(h2.reshape(t, d), seg_base, seg_cnt, tail, tail_len, lpost, n_rows, tm, tb)
    ys = _ffn(xs, n_rows, block_expert, n_used, layer, wgu, bgu, wdn, bdn, tb)
    return _combine(seg_base, seg_cnt, lpos, gates, x_res, g2, lng, lnb, ys, n_lat_tiles, n_tiles, tm, alpha)


def _rope_tables(n_lat_rows, n_rows, scale):
    pos = jnp.arange(n_lat_rows)
    lane = jnp.arange(LANES)
    m = lane % 64
    n_freq = 16
    inv = ROPE_BASE ** (-(m % n_freq).astype(F32) / n_freq)
    p = jnp.where((m // 32)[None, :] == 0, (pos // GRID_W)[:, None], (pos % GRID_W)[:, None]).astype(F32)
    ang = p * inv[None, :]
    sign = jnp.where((m % 32) < n_freq, -1.0, 1.0)[None, :]
    cos = jnp.concatenate([jnp.cos(ang), jnp.ones((n_rows - n_lat_rows, LANES), F32)], axis=0)
    sin = jnp.concatenate([jnp.sin(ang) * sign, jnp.zeros((n_rows - n_lat_rows, LANES), F32)], axis=0)
    return cos * scale, sin * scale


def kernel(x, c, ctx, c_ctx, w_ada, b_ada, ln_g, ln_b, hg_w_in, hg_lb, hg_norm_g, hg_w_out, da_w_in, da_lam,
           da_norm_g, da_w_out, moe_w_router, moe_b_router, moe_w_gu, moe_b_gu, moe_w_dn, moe_b_dn):
    bsz, seq, d = x.shape
    lc = ctx.shape[1]
    depth = w_ada.shape[0]
    assert depth == 2 and hg_w_in.shape[0] == 1 and da_w_in.shape[0] == 1
    ltot = seq + lc
    tm = 256
    assert seq % tm == 0 and lc % tm == 0 and seq % GRID_W == 0 and d == DA_HEADS * LANES
    alpha = (2 * depth) ** 0.25
    n_lat_tiles = seq // tm
    n_all_tiles = ltot // tm
    h_heads = d // LANES

    n_mod = bsz + 1
    c_pad = jnp.zeros((16, d), F32).at[:bsz].set(c).at[bsz].set(c_ctx)
    mod = _ada(c_pad, w_ada, b_ada)[:, :n_mod]

    def mods(l):
        return [mod[l, :, k * d:(k + 1) * d].reshape(n_mod, 1, d) for k in range(6)]

    def router_params(l):
        wr = jnp.zeros((d, LANES), F32).at[:, :N_EXPERTS].set(moe_w_router[l])
        wr_hi = wr.astype(BF16)
        wr_lo = (wr - wr_hi.astype(F32)).astype(BF16)
        br = jnp.full((1, LANES), -1e30, F32).at[0, :N_EXPERTS].set(moe_b_router[l])
        return jnp.stack([wr_hi, wr_lo]), br

    xall = jnp.concatenate([x, ctx], axis=1)

    sh1, sc1, g1, sh2, sc2, g2 = mods(0)
    p0 = _proj(xall, sc1, sh1, hg_w_in[0].astype(BF16), n_lat_tiles, tm)
    lb = jnp.cumsum(jax.nn.softmax(hg_lb.astype(F32), axis=1), axis=1)[:, 0]
    lb = lb.reshape(2, h_heads, LANES).transpose(1, 0, 2)
    y0 = _gla(p0, lb, hg_norm_g[0].reshape(h_heads, 1, LANES), seq)
    wr, br = router_params(0)
    x0, h20, lg0 = _post(y0, hg_w_out[0].astype(BF16), xall, g1, ln_g[0, 0].reshape(1, d),
                         ln_b[0, 0].reshape(1, d), sc2, sh2, wr, br, n_lat_tiles, n_all_tiles, tm, alpha)
    x1 = _moe_layer(h20, lg0, x0, g2, ln_g[0, 1].reshape(1, d), ln_b[0, 1].reshape(1, d),
                    0, moe_w_gu, moe_b_gu, moe_w_dn, moe_b_dn, n_lat_tiles, n_all_tiles, tm, alpha)

    sh1, sc1, g1, sh2, sc2, g2 = mods(1)
    dh = d // DA_HEADS // 2
    cq, sq = _rope_tables(seq, ltot, dh ** -0.5 * math.log2(math.e))
    ck, sk = _rope_tables(seq, ltot, 1.0)
    p1 = _proj(x1, sc1, sh1, da_w_in[0].astype(BF16), n_lat_tiles, tm, rope_tabs=(cq, sq, ck, sk))
    lam_init = 0.8 - 0.6 * math.exp(-0.3 * 1)
    lp = da_lam[0].astype(F32)
    lam = (jnp.exp(jnp.sum(lp[0] * lp[1])) - jnp.exp(jnp.sum(lp[2] * lp[3])) + lam_init).reshape(1)
    y1 = _attn(p1, lam, da_norm_g[0].reshape(LANES, 1), seq, 1.0 - lam_init)
    wr, br = router_params(1)
    x2, h21, lg1 = _post(y1, da_w_out[0].astype(BF16), x1, g1, ln_g[1, 0].reshape(1, d),
                         ln_b[1, 0].reshape(1, d), sc2, sh2, wr, br, n_lat_tiles, n_lat_tiles, tm, alpha)
    return _moe_layer(h21, lg1, x2, g2, ln_g[1, 1].reshape(1, d), ln_b[1, 1].reshape(1, d),
                      1, moe_w_gu, moe_b_gu, moe_w_dn, moe_b_dn, n_lat_tiles, n_lat_tiles, tm, alpha)
```

```python
import functools
import math

import jax
import jax.numpy as jnp
from jax import lax
from jax.experimental import pallas as pl
from jax.experimental.pallas import tpu as pltpu

F32 = jnp.float32
BF16 = jnp.bfloat16
HIGHEST = lax.Precision.HIGHEST

LANES = 128
HG_CHUNK = 64
GRID_W = 64
ROPE_BASE = 10000.0
DA_HEADS = 8
N_EXPERTS = 32
TOP_K = 4
SEG_ALIGN = 8
SWIGLU_LIMIT = 7.0
SWIGLU_ALPHA = 1.702
LN_EPS = 1e-5
RMS_EPS = 1e-6
VMEM_LIMIT = 56 * 1024 * 1024


def _cparams(sem):
    return pltpu.CompilerParams(dimension_semantics=sem, vmem_limit_bytes=VMEM_LIMIT)


def _ada_kernel(c_ref, w_ref, b_ref, o_ref):
    c = c_ref[...]
    s = c * jax.nn.sigmoid(c)
    o_ref[0] = jnp.dot(s, w_ref[0], precision=HIGHEST, preferred_element_type=F32) + b_ref[0]


def _ada(c_pad, w_ada, b_ada):
    depth, d, n = w_ada.shape
    rows = c_pad.shape[0]
    tn = 1536
    return pl.pallas_call(
        _ada_kernel,
        grid=(depth, n // tn),
        in_specs=[pl.BlockSpec((rows, d), lambda l, j: (0, 0)),
                  pl.BlockSpec((1, d, tn), lambda l, j: (l, 0, j)),
                  pl.BlockSpec((1, 1, tn), lambda l, j: (l, 0, j))],
        out_specs=pl.BlockSpec((1, rows, tn), lambda l, j: (l, 0, j)),
        out_shape=jax.ShapeDtypeStruct((depth, rows, n), F32),
        compiler_params=_cparams(("arbitrary", "arbitrary")),
        name="ada_mod",
    )(c_pad, w_ada, b_ada.reshape(depth, 1, n))


def _rope(piece, cos, sin, lo_mask):
    rot = jnp.where(lo_mask, pltpu.roll(piece, LANES - 16, 1), pltpu.roll(piece, 16, 1))
    return piece * cos + rot * sin


def _token_tile(x_ref, xc_ref, n_lat_tiles):
    if xc_ref is None:
        return x_ref[0]
    return jnp.where(pl.program_id(1) < n_lat_tiles, x_ref[0], xc_ref[0])


def _stream_specs(x, tm, n_lat_tiles):
    if not isinstance(x, tuple):
        return [pl.BlockSpec((1, tm, x.shape[2]), lambda i, j: (i, j, 0))], [x]
    d = x[0].shape[2]
    return ([pl.BlockSpec((1, tm, d), lambda i, j: (i, jnp.minimum(j, n_lat_tiles - 1), 0)),
             pl.BlockSpec((1, tm, d), lambda i, j: (i, jnp.maximum(j - n_lat_tiles, 0), 0))], list(x))


def _proj_kernel(*refs, n_rope, cw, two_streams, n_lat_tiles):
    x_ref, refs = refs[0], refs[1:]
    xc_ref = None
    if two_streams:
        xc_ref, refs = refs[0], refs[1:]
    sc_ref, sh_ref, w_ref, *rest = refs
    if n_rope:
        cq_ref, sq_ref, ck_ref, sk_ref, o_ref = rest
    else:
        (o_ref,) = rest
    n = w_ref.shape[1]
    h = (_token_tile(x_ref, xc_ref, n_lat_tiles) * (1.0 + sc_ref[0]) + sh_ref[0]).astype(BF16)
    if n_rope:
        lane = lax.broadcasted_iota(jnp.int32, (x_ref.shape[1], LANES), 1)
        lo_mask = (lane % 32) < 16
    per = cw // LANES
    for j in range(n // cw):
        r = jnp.dot(h, w_ref[:, j * cw:(j + 1) * cw], preferred_element_type=F32)
        for g in range(per):
            hd = j * per + g
            piece = r[:, g * LANES:(g + 1) * LANES]
            if hd < n_rope:
                piece = _rope(piece, cq_ref[...], sq_ref[...], lo_mask)
            elif hd < 2 * n_rope:
                piece = _rope(piece, ck_ref[...], sk_ref[...], lo_mask)
            o_ref[0, hd] = piece.astype(BF16)


def _proj(x, sc, sh, w, n_lat_tiles, tm, rope_tabs=None):
    two_streams = isinstance(x, tuple)
    b, _, d = (x[0] if two_streams else x).shape
    ltot = x[0].shape[1] + x[1].shape[1] if two_streams else x.shape[1]
    n = w.shape[1]
    nb = sc.shape[0] - 1
    n_rope = DA_HEADS if rope_tabs is not None else 0

    def mod_map(i, j):
        return (jnp.where(j < n_lat_tiles, i, nb), 0, 0)

    in_specs, args = _stream_specs(x, tm, n_lat_tiles)
    in_specs += [pl.BlockSpec((1, 1, d), mod_map),
                 pl.BlockSpec((1, 1, d), mod_map),
                 pl.BlockSpec((d, n), lambda i, j: (0, 0))]
    args += [sc, sh, w]
    if rope_tabs is not None:
        in_specs += [pl.BlockSpec((tm, LANES), lambda i, j: (j, 0))] * 4
        args += list(rope_tabs)
    return pl.pallas_call(
        functools.partial(_proj_kernel, n_rope=n_rope, cw=512, two_streams=two_streams,
                          n_lat_tiles=n_lat_tiles),
        grid=(b, ltot // tm),
        in_specs=in_specs,
        out_specs=pl.BlockSpec((1, n // LANES, tm, LANES), lambda i, j: (i, 0, j, 0)),
        out_shape=jax.ShapeDtypeStruct((b, n // LANES, ltot, LANES), BF16),
        compiler_params=_cparams(("arbitrary", "arbitrary")),
        name="mod_proj_rope" if n_rope else "mod_proj",
    )(*args)


GLA_BLOCK = 256


def _gla_local(items):
    nc = GLA_BLOCK // HG_CHUNK
    kks, cats = [], []
    for (_, _, z, lb, _, _, _) in items:
        f = lb + (1.0 - lb) * jax.nn.sigmoid(z)
        logf = jnp.log(f)
        kks.append(1.0 - f)
        hi = logf.astype(BF16)
        lo = (logf - hi.astype(F32)).astype(BF16)
        cats.append(jnp.concatenate([hi, lo], axis=1))
    parts = [jnp.dot(it[4], cat, preferred_element_type=F32) for it, cat in zip(items, cats)]
    q_decs, k_invs, k_ends, decs = [], [], [], []
    for (q, _, _, _, _, _, end_row), kk, part in zip(items, kks, parts):
        bcum = part[:, :LANES] + part[:, LANES:]
        b_end = bcum.reshape(nc, HG_CHUNK, LANES)[:, end_row:end_row + 1, :]
        b_end_rows = jnp.broadcast_to(b_end, (nc, HG_CHUNK, LANES)).reshape(GLA_BLOCK, LANES)
        q_decs.append((q * jnp.exp(bcum)).astype(BF16))
        k_invs.append((kk * jnp.exp(-bcum)).astype(BF16))
        k_ends.append((kk * jnp.exp(b_end_rows - bcum)).astype(BF16))
        decs.append(jnp.exp(b_end.reshape(nc, LANES)))
    scores = [lax.dot_general(qd, ki, (((1,), (1,)), ((), ())), preferred_element_type=F32)
              for qd, ki in zip(q_decs, k_invs)]
    scores = [jnp.where(it[5], a, 0.0).astype(BF16) for it, a in zip(items, scores)]
    outs = [jnp.dot(a, it[1], preferred_element_type=F32) for it, a in zip(items, scores)]
    return list(zip(outs, q_decs, k_ends, decs))


def _gla_kernel(q_ref, i_ref, g_ref, zf_ref, zb_ref, lb_ref, ng_ref, y_ref,
                of_ref, ob_ref, qd_ref, ke_ref, de_ref, st_ref, *, n_lat, n_ctx):
    c = HG_CHUNK
    blk = GLA_BLOCK
    nc = blk // c
    ltot = (n_lat + n_ctx) * c
    row = lax.broadcasted_iota(jnp.int32, (blk, blk), 0)
    col = lax.broadcasted_iota(jnp.int32, (blk, blk), 1)
    same = (row // c) == (col // c)
    masks = (same & (col <= row), same & (col >= row))
    tris = (masks[0].astype(BF16), masks[1].astype(BF16))
    z_refs = (zf_ref, zb_ref)
    o_refs = (of_ref, ob_ref)
    end_rows = (c - 1, 0)

    n_blk = ltot // blk
    per_step = 3 if n_blk % 3 == 0 else 1

    def local(t, carry):
        work = []
        for u in range(per_step):
            r0 = pl.multiple_of((t * per_step + u) * blk, blk)
            c0 = pl.multiple_of((t * per_step + u) * nc, nc)
            q = q_ref[0, 0, pl.ds(r0, blk), :].astype(F32)
            v = i_ref[0, 0, pl.ds(r0, blk), :]
            zs = [z_refs[d][0, 0, pl.ds(r0, blk), :].astype(F32) for d in range(2)]
            work.append((r0, c0, q, v, zs))
        res = _gla_local([(q, v, zs[d], lb_ref[0, d:d + 1, :], tris[d], masks[d], end_rows[d])
                          for (_, _, q, v, zs) in work for d in range(2)])
        for u, (r0, c0, _, _, _) in enumerate(work):
            for d in range(2):
                o, q_dec, k_end, dec = res[2 * u + d]
                o_refs[d][pl.ds(r0, blk), :] = o
                qd_ref[d, pl.ds(r0, blk), :] = q_dec
                ke_ref[d, pl.ds(r0, blk), :] = k_end
                de_ref[d, pl.ds(c0, nc), :] = dec
        return carry
    lax.fori_loop(0, n_blk // per_step, local, 0)

    st_ref[...] = jnp.zeros_like(st_ref)

    def segment(first, n):
        per = 8 if n % 8 == 0 else (4 if n % 4 == 0 else 1)

        def body(jj, carry):
            todo = []
            for u in range(per):
                j = jj * per + u
                for d, ch in enumerate((first + j, first + n - 1 - j)):
                    r0 = pl.multiple_of(ch * c, c)
                    kv = lax.dot_general(i_ref[0, 0, pl.ds(r0, c), :], ke_ref[d, pl.ds(r0, c), :],
                                         (((0,), (0,)), ((), ())), preferred_element_type=F32)
                    todo.append((d, r0, kv, de_ref[d, pl.ds(ch, 1), :], qd_ref[d, pl.ds(r0, c), :]))
            st = [st_ref[0], st_ref[1]]
            inter = []
            for (d, r0, kv, dec, q_dec) in todo:
                inter.append((d, r0, lax.dot_general(q_dec, st[d].astype(BF16), (((1,), (1,)), ((), ())),
                                                     preferred_element_type=F32)))
                st[d] = st[d] * dec + kv
            for (d, r0, o) in inter:
                o_refs[d][pl.ds(r0, c), :] += o
            st_ref[0] = st[0]
            st_ref[1] = st[1]
            return carry
        lax.fori_loop(0, n // per, body, 0)

    segment(n_lat, n_ctx)
    segment(0, n_lat)

    ng = ng_ref[0]
    rt = 256

    def readout(t, carry):
        r0 = pl.multiple_of(t * rt, rt)
        o = of_ref[pl.ds(r0, rt), :] + ob_ref[pl.ds(r0, rt), :]
        ms = jnp.mean(o * o, axis=-1, keepdims=True)
        g = g_ref[0, 0, pl.ds(r0, rt), :].astype(F32)
        y = o * lax.rsqrt(ms + RMS_EPS) * ng * (g * jax.nn.sigmoid(g))
        y_ref[0, 0, pl.ds(r0, rt), :] = y.astype(BF16)
        return carry
    lax.fori_loop(0, ltot // rt, readout, 0)


def _gla(p, lb, ng, n_lat_rows):
    b, nh5, ltot, _ = p.shape
    h = nh5 // 5
    n_lat = n_lat_rows // HG_CHUNK
    n_ctx = (ltot - n_lat_rows) // HG_CHUNK

    def spec(k):
        return pl.BlockSpec((1, 1, ltot, LANES), lambda i, j, k=k: (i, k * h + j, 0, 0))

    return pl.pallas_call(
        functools.partial(_gla_kernel, n_lat=n_lat, n_ctx=n_ctx),
        grid=(b, h),
        in_specs=[spec(0), spec(1), spec(2), spec(3), spec(4),
                  pl.BlockSpec((1, 2, LANES), lambda i, j: (j, 0, 0)),
                  pl.BlockSpec((1, 1, LANES), lambda i, j: (j, 0, 0))],
        out_specs=pl.BlockSpec((1, 1, ltot, LANES), lambda i, j: (i, j, 0, 0)),
        out_shape=jax.ShapeDtypeStruct((b, h, ltot, LANES), BF16),
        scratch_shapes=[pltpu.VMEM((ltot, LANES), F32), pltpu.VMEM((ltot, LANES), F32),
                        pltpu.VMEM((2, ltot, LANES), BF16), pltpu.VMEM((2, ltot, LANES), BF16),
                        pltpu.VMEM((2, ltot // HG_CHUNK, LANES), F32),
                        pltpu.VMEM((2, LANES, LANES), F32)],
        compiler_params=_cparams(("arbitrary", "arbitrary")),
        name="hgrn2_gla",
    )(p, p, p, p, p, lb, ng)


SUM_ROWS = 16


BOUND_SLACK = 1.0 + 2.0 ** -5
MIN_SOFTMAX_SUM = 1e-30


def _attn_kernel(lam_ref, q_ref, k_ref, v_ref, ng_ref, y_ref, vt_ref, kmax_ref, acc_ref, *, tk, out_scale):
    tq = q_ref.shape[2]
    lk = k_ref.shape[2]
    lam = lam_ref[0]
    n_chunks = lk // tk
    srow = lax.broadcasted_iota(jnp.int32, (8, LANES), 0)
    slane = lax.broadcasted_iota(jnp.int32, (8, LANES), 1)
    sel = (srow == slane // 64).astype(BF16)

    def sq_norms(x):
        return lax.dot_general(sel, x * x, (((1,), (1,)), ((), ())), preferred_element_type=F32)

    @pl.when(pl.program_id(2) == 0)
    def _():
        blk = 256
        kn = jnp.zeros((8, blk), F32)
        for cb in range(lk // blk):
            vt_ref[0:LANES, cb * blk:(cb + 1) * blk] = jnp.transpose(
                v_ref[0, 0, cb * blk:(cb + 1) * blk, :].astype(F32)).astype(BF16)
            kn = jnp.maximum(kn, sq_norms(k_ref[0, 0, cb * blk:(cb + 1) * blk, :]))
        vt_ref[LANES:LANES + SUM_ROWS, :] = jnp.ones((SUM_ROWS, lk), BF16)
        kmax_ref[...] = jnp.broadcast_to(jnp.sqrt(jnp.max(kn, axis=1, keepdims=True)), kmax_ref.shape)

    q = q_ref[0, 0]
    lane = lax.broadcasted_iota(jnp.int32, (tq, LANES), 1)
    zero = jnp.zeros_like(q)
    qm = (jnp.where(lane < 64, q, zero), jnp.where(lane >= 64, q, zero))

    def scores(ci, c):
        kc = k_ref[0, 0, ci * tk:(ci + 1) * tk, :]
        return lax.dot_general(kc, qm[c], (((1,), (1,)), ((), ())), preferred_element_type=F32)

    bound = jnp.sqrt(sq_norms(q)) * kmax_ref[...] * BOUND_SLACK
    acc = [jnp.zeros((LANES + SUM_ROWS, tq), F32) for _ in range(2)]
    for ci in range(n_chunks):
        vt = vt_ref[:, ci * tk:(ci + 1) * tk]
        for c in range(2):
            p = jnp.exp2(scores(ci, c) - bound[c:c + 1]).astype(BF16)
            acc[c] = acc[c] + jnp.dot(vt, p, preferred_element_type=F32)
    acc_ref[0] = acc[0]
    acc_ref[1] = acc[1]
    sums = jnp.minimum(acc[0][LANES:LANES + 1], acc[1][LANES:LANES + 1])
    underflow = jnp.logical_not(jnp.min(sums) > MIN_SOFTMAX_SUM)

    @pl.when(underflow)
    def _():
        def body(ci, carry):
            r0 = pl.multiple_of(ci * tk, tk)
            kc = k_ref[0, 0, pl.ds(r0, tk), :]
            vt = vt_ref[:, pl.ds(r0, tk)]
            out = []
            for c in range(2):
                m_c, a_c = carry[c]
                s_c = lax.dot_general(kc, qm[c], (((1,), (1,)), ((), ())), preferred_element_type=F32)
                m_new = jnp.maximum(m_c, jnp.max(s_c, axis=0, keepdims=True))
                p = jnp.exp2(s_c - m_new).astype(BF16)
                out.append((m_new, jnp.exp2(m_c - m_new) * a_c + jnp.dot(vt, p, preferred_element_type=F32)))
            return tuple(out)
        init = tuple((jnp.full((1, tq), -1e30, F32), jnp.zeros((LANES + SUM_ROWS, tq), F32))
                     for _ in range(2))
        (_, b0), (_, b1) = lax.fori_loop(0, n_chunks, body, init)
        acc_ref[0] = b0
        acc_ref[1] = b1

    a0 = acc_ref[0]
    a1 = acc_ref[1]
    o = a0[:LANES] / a0[LANES:LANES + 1] - lam * (a1[:LANES] / a1[LANES:LANES + 1])
    ms = jnp.mean(o * o, axis=0, keepdims=True)
    y = o * lax.rsqrt(ms + RMS_EPS) * ng_ref[...] * out_scale
    y_ref[0, 0] = jnp.transpose(y).astype(BF16)


def _attn(p, lam, ng, n_lat_rows, out_scale, tq=256, tk=768):
    b, nh3, ltot, _ = p.shape
    h = nh3 // 3
    assert ltot % tk == 0 and n_lat_rows % tq == 0 and ltot % 256 == 0
    return pl.pallas_call(
        functools.partial(_attn_kernel, tk=tk, out_scale=out_scale),
        grid=(b, h, n_lat_rows // tq),
        in_specs=[pl.BlockSpec(memory_space=pltpu.SMEM),
                  pl.BlockSpec((1, 1, tq, LANES), lambda i, j, t: (i, j, t, 0)),
                  pl.BlockSpec((1, 1, ltot, LANES), lambda i, j, t: (i, h + j, 0, 0)),
                  pl.BlockSpec((1, 1, ltot, LANES), lambda i, j, t: (i, 2 * h + j, 0, 0)),
                  pl.BlockSpec((LANES, 1), lambda i, j, t: (0, 0))],
        out_specs=pl.BlockSpec((1, 1, tq, LANES), lambda i, j, t: (i, j, t, 0)),
        out_shape=jax.ShapeDtypeStruct((b, h, n_lat_rows, LANES), BF16),
        scratch_shapes=[pltpu.VMEM((LANES + SUM_ROWS, ltot), BF16), pltpu.VMEM((8, tq), F32),
                        pltpu.VMEM((2, LANES + SUM_ROWS, tq), F32)],
        compiler_params=_cparams(("arbitrary", "arbitrary", "arbitrary")),
        name="diff_attn",
    )(lam, p, p, p, ng)


def _layer_norm(x, g, b):
    mu = jnp.mean(x, axis=-1, keepdims=True)
    xc = x - mu
    var = jnp.mean(xc * xc, axis=-1, keepdims=True)
    return xc * lax.rsqrt(var + LN_EPS) * g + b


def _post_kernel(y_ref, w_ref, *refs, alpha, two_streams, n_lat_tiles):
    x_ref, refs = refs[0], refs[1:]
    xc_ref = None
    if two_streams:
        xc_ref, refs = refs[0], refs[1:]
    g1_ref, lng_ref, lnb_ref, sc2_ref, sh2_ref, wr_ref, br_ref, xo_ref, h2_ref, lg_ref = refs
    nh = y_ref.shape[1]
    y = jnp.concatenate([y_ref[0, k] for k in range(nh)], axis=-1)
    m = jnp.dot(y, w_ref[...], preferred_element_type=F32)
    x_res = _token_tile(x_ref, xc_ref, n_lat_tiles)
    xl = _layer_norm(alpha * x_res + g1_ref[0] * m, lng_ref[...], lnb_ref[...])
    xo_ref[0] = xl
    h2 = xl * (1.0 + sc2_ref[0]) + sh2_ref[0]
    h_hi = h2.astype(BF16)
    h2_ref[0] = h_hi
    h_lo = (h2 - h_hi.astype(F32)).astype(BF16)
    lg_ref[0] = (jnp.dot(h_hi, wr_ref[0], preferred_element_type=F32)
                 + jnp.dot(h_lo, wr_ref[0], preferred_element_type=F32)
                 + jnp.dot(h_hi, wr_ref[1], preferred_element_type=F32) + br_ref[...])


def _post(y, w_out, x, g1, lng, lnb, sc2, sh2, wr, br, n_lat_tiles, n_tiles, tm, alpha):
    b, nh, _, _ = y.shape
    two_streams = isinstance(x, tuple)
    d = (x[0] if two_streams else x).shape[2]
    nb = g1.shape[0] - 1
    rows = n_tiles * tm

    def mod_map(i, j):
        return (jnp.where(j < n_lat_tiles, i, nb), 0, 0)

    tok = pl.BlockSpec((1, tm, d), lambda i, j: (i, j, 0))
    vec = pl.BlockSpec((1, d), lambda i, j: (0, 0))
    x_specs, x_args = _stream_specs(x, tm, n_lat_tiles)
    return pl.pallas_call(
        functools.partial(_post_kernel, alpha=alpha, two_streams=two_streams, n_lat_tiles=n_lat_tiles),
        grid=(b, n_tiles),
        in_specs=[pl.BlockSpec((1, nh, tm, LANES), lambda i, j: (i, 0, j, 0)),
                  pl.BlockSpec((d, d), lambda i, j: (0, 0)),
                  *x_specs, pl.BlockSpec((1, 1, d), mod_map), vec, vec,
                  pl.BlockSpec((1, 1, d), mod_map), pl.BlockSpec((1, 1, d), mod_map),
                  pl.BlockSpec((2, d, LANES), lambda i, j: (0, 0, 0)),
                  pl.BlockSpec((1, LANES), lambda i, j: (0, 0))],
        out_specs=[tok, tok, pl.BlockSpec((1, tm, LANES), lambda i, j: (i, j, 0))],
        out_shape=[jax.ShapeDtypeStruct((b, rows, d), F32),
                   jax.ShapeDtypeStruct((b, rows, d), BF16),
                   jax.ShapeDtypeStruct((b, rows, LANES), F32)],
        compiler_params=_cparams(("arbitrary", "arbitrary")),
        name="out_proj_ln",
    )(y, w_out, *x_args, g1, lng, lnb, sc2, sh2, wr, br)


def _router_kernel(lg_ref, idx_ref, gate_ref, lpos_ref, lpost_ref, tile_ref, cnt_ref, carry_ref):
    tm = lg_ref.shape[0]

    @pl.when(pl.program_id(0) == 0)
    def _():
        carry_ref[...] = jnp.zeros_like(carry_ref)

    work = lg_ref[...]
    lane = lax.broadcasted_iota(jnp.int32, (tm, LANES), 1)
    lane_f = lane.astype(F32)
    vals, onehots = [], []
    idx_out = jnp.zeros((tm, LANES), jnp.int32)
    for k in range(TOP_K):
        mx = jnp.max(work, axis=-1, keepdims=True)
        am = jnp.min(jnp.where(work == mx, lane_f, float(LANES)), axis=-1, keepdims=True)
        oh = lane_f == am
        vals.append(mx)
        onehots.append(oh)
        idx_out = jnp.where(lane == k, am.astype(jnp.int32), idx_out)
        work = jnp.where(oh, -3e38, work)
    es = [jnp.exp(v - vals[0]) for v in vals]
    denom = es[0] + es[1] + es[2] + es[3]
    gate_out = jnp.zeros((tm, LANES), F32)
    for k in range(TOP_K):
        gate_out = jnp.where(lane == k, es[k] / denom, gate_out)
    member = (onehots[0] | onehots[1] | onehots[2] | onehots[3])
    r = lax.broadcasted_iota(jnp.int32, (tm, tm), 0)
    c = lax.broadcasted_iota(jnp.int32, (tm, tm), 1)
    strict = (c < r).astype(BF16)
    before = jnp.dot(strict, member.astype(BF16), preferred_element_type=F32)
    cnt_tile = jnp.sum(member.astype(F32), axis=0, keepdims=True)
    cnt_tile = jnp.floor((cnt_tile + (SEG_ALIGN - 1)) * (1.0 / SEG_ALIGN)) * SEG_ALIGN
    ei = lax.broadcasted_iota(jnp.int32, (LANES, LANES), 0)
    ej = lax.broadcasted_iota(jnp.int32, (LANES, LANES), 1)
    seg_start = jnp.dot(jnp.broadcast_to(cnt_tile, (8, LANES)).astype(BF16), (ei < ej).astype(BF16),
                        preferred_element_type=F32)[0:1]
    slot = seg_start + before
    lpos_out = jnp.zeros((tm, LANES), jnp.int32)
    for k in range(TOP_K):
        lp = jnp.sum(jnp.where(onehots[k], slot, 0.0), axis=-1, keepdims=True)
        lpos_out = jnp.where(lane == k, lp.astype(jnp.int32), lpos_out)
    idx_ref[...] = idx_out
    gate_ref[...] = gate_out
    lpos_ref[...] = lpos_out
    lpost_ref[...] = jnp.transpose(lpos_out.astype(F32))[0:8, :].astype(jnp.int32)
    row8 = lax.broadcasted_iota(jnp.int32, (8, LANES), 0)
    tile_ref[...] = jnp.where(row8 == 0, carry_ref[...], jnp.where(row8 == 1, cnt_tile, 0.0))
    carry_ref[...] = carry_ref[...] + cnt_tile
    cnt_ref[...] = carry_ref[...]


def _router(logits, tm):
    t = logits.shape[0]
    tile = pl.BlockSpec((tm, LANES), lambda i: (i, 0))
    return pl.pallas_call(
        _router_kernel,
        grid=(t // tm,),
        in_specs=[tile],
        out_specs=[tile, tile, tile, pl.BlockSpec((8, tm), lambda i: (0, i)),
                   pl.BlockSpec((8, LANES), lambda i: (i, 0)), pl.BlockSpec((1, LANES), lambda i: (0, 0))],
        out_shape=[jax.ShapeDtypeStruct((t, LANES), jnp.int32),
                   jax.ShapeDtypeStruct((t, LANES), F32),
                   jax.ShapeDtypeStruct((t, LANES), jnp.int32),
                   jax.ShapeDtypeStruct((8, t), jnp.int32),
                   jax.ShapeDtypeStruct((t // tm * 8, LANES), F32),
                   jax.ShapeDtypeStruct((1, LANES), F32)],
        scratch_shapes=[pltpu.VMEM((1, LANES), F32)],
        compiler_params=_cparams(("arbitrary",)),
        name="router_topk",
    )(logits)


HI_MASK = 0xFFFF0000


def _pack_rows(x):
    half = x.shape[1] // 2
    bits = lax.bitcast_convert_type(x.astype(BF16).astype(F32), jnp.uint32)
    return (bits[:, :half] & jnp.uint32(HI_MASK)) | (bits[:, half:] >> 16)


def _unpack_rows(p):
    left = lax.bitcast_convert_type(p & jnp.uint32(HI_MASK), F32).astype(BF16)
    right = lax.bitcast_convert_type(p << 16, F32).astype(BF16)
    return left, right


def _copy_pieces(n, local_row, global_row, copy, max_rows, wait=False):
    for b in range(SEG_ALIGN.bit_length() - 1, max_rows.bit_length()):
        size = 1 << b

        @pl.when((n & size) != 0)
        def _():
            lo = n & (size - 1)
            piece = copy(pl.multiple_of(local_row + lo, SEG_ALIGN), pl.multiple_of(global_row + lo, SEG_ALIGN),
                         size)
            if wait:
                piece.wait()
            else:
                piece.start()


def _segment_copies(base_ref, cnt_ref, copy, max_rows, n_slots, fill_row):
    def per_expert(e, off):
        _copy_pieces(cnt_ref[e], off, base_ref[e], copy, max_rows)
        return off + cnt_ref[e]
    used = lax.fori_loop(0, N_EXPERTS, per_expert, 0)
    _copy_pieces(n_slots - used, used, fill_row, copy, max_rows)


def _tile_slots(tm):
    return tm * TOP_K + N_EXPERTS * SEG_ALIGN


def _dispatch_kernel(base_ref, cnt_ref, tail_ref, tail_len_ref, lpost_ref, h_ref, xs_ref,
                     cbuf_ref, zbuf_ref, sem, zsem, *, spare_row):
    tm = h_ref.shape[0]

    @pl.when(pl.program_id(0) == pl.num_programs(0) - 1)
    def _():
        zbuf_ref[...] = jnp.zeros_like(zbuf_ref)

        def zero_copy(src, dst, size):
            return pltpu.make_async_copy(zbuf_ref.at[pl.ds(src, size)], xs_ref.at[pl.ds(dst, size)], zsem)
        for wait in (False, True):
            def per_expert(e, carry, wait=wait):
                _copy_pieces(tail_len_ref[e], 0, tail_ref[e], zero_copy, zbuf_ref.shape[0], wait=wait)
                return carry
            lax.fori_loop(0, N_EXPERTS, per_expert, 0)

    n_slots = _tile_slots(tm)
    slot = lax.broadcasted_iota(jnp.int32, (n_slots, tm), 0)
    lp = lpost_ref[...]
    sel = slot == lp[0:1, :]
    for k in range(1, TOP_K):
        sel = sel | (slot == lp[k:k + 1, :])
    rows = jnp.dot(sel.astype(BF16), h_ref[...].astype(BF16), preferred_element_type=F32)
    step = pl.program_id(0)
    cur = step % 2
    cbuf_ref[cur] = _pack_rows(rows)

    def copy_from(buf):
        def copy(src, dst, size):
            return pltpu.make_async_copy(cbuf_ref.at[buf, pl.ds(src, size)], xs_ref.at[pl.ds(dst, size)],
                                         sem.at[buf])
        return copy
    _segment_copies(base_ref, cnt_ref, copy_from(cur), tm, n_slots, spare_row + cur * tm)

    @pl.when(step > 0)
    def _():
        copy_from(1 - cur)(0, 0, n_slots).wait()

    @pl.when(step == pl.num_programs(0) - 1)
    def _():
        copy_from(cur)(0, 0, n_slots).wait()


def _dispatch(h2, base, cnt, tail, tail_len, lpost, n_rows, tm, tb):
    t, d = h2.shape
    seg = pl.BlockSpec((LANES,), lambda i: (i,), memory_space=pltpu.SMEM)
    whole = pl.BlockSpec(memory_space=pltpu.SMEM)
    return pl.pallas_call(
        functools.partial(_dispatch_kernel, spare_row=n_rows),
        grid=(t // tm,),
        in_specs=[seg, seg, whole, whole,
                  pl.BlockSpec((8, tm), lambda i: (0, i)),
                  pl.BlockSpec((tm, d), lambda i: (i, 0))],
        out_specs=pl.BlockSpec(memory_space=pl.ANY),
        out_shape=jax.ShapeDtypeStruct((n_rows + 2 * tm, d // 2), jnp.uint32),
        scratch_shapes=[pltpu.VMEM((2, _tile_slots(tm), d // 2), jnp.uint32),
                        pltpu.VMEM((tb, d // 2), jnp.uint32),
                        pltpu.SemaphoreType.DMA((2,)), pltpu.SemaphoreType.DMA(())],
        compiler_params=_cparams(("arbitrary",)),
        name="moe_dispatch",
    )(base, cnt, tail, tail_len, lpost, h2)


def _ffn_kernel(be_ref, nu_ref, x_ref, wgu_ref, bgu_ref, wdn_ref, bdn_ref, y_ref, wgu_bf_ref, wdn_bf_ref):
    de = wdn_ref.shape[2]
    i = pl.program_id(0)
    used = i < nu_ref[0]

    @pl.when(used & ((i == 0) | (be_ref[i] != be_ref[jnp.maximum(i - 1, 0)])))
    def _():
        wgu_bf_ref[...] = wgu_ref[0, 0].astype(BF16)
        wdn_bf_ref[...] = wdn_ref[0, 0].astype(BF16)

    @pl.when(used)
    def _():
        half = x_ref.shape[1]
        x_left, x_right = _unpack_rows(x_ref[...])
        gu = (jnp.dot(x_left, wgu_bf_ref[:half, :], preferred_element_type=F32)
              + jnp.dot(x_right, wgu_bf_ref[half:, :], preferred_element_type=F32) + bgu_ref[0, 0])
        gate = jnp.minimum(gu[:, :de], SWIGLU_LIMIT)
        up = jnp.clip(gu[:, de:], -SWIGLU_LIMIT, SWIGLU_LIMIT)
        act = (up + 1.0) * gate * jax.nn.sigmoid(SWIGLU_ALPHA * gate)
        y = jnp.dot(act.astype(BF16), wdn_bf_ref[...], preferred_element_type=F32) + bdn_ref[0, 0]
        y_ref[...] = _pack_rows(y)

    @pl.when(jnp.logical_not(used))
    def _():
        y_ref[...] = jnp.zeros_like(y_ref)


def _ffn(xs, n_rows, block_expert, n_used, layer, wgu, bgu, wdn, bdn, tb):
    p, half = n_rows, xs.shape[1]
    d = 2 * half
    nl, e, _, n2 = wgu.shape
    de = wdn.shape[2]
    grid_spec = pltpu.PrefetchScalarGridSpec(
        num_scalar_prefetch=2,
        grid=(p // tb,),
        in_specs=[pl.BlockSpec((tb, half), lambda i, be, nu: (i, 0)),
                  pl.BlockSpec((1, 1, d, n2), lambda i, be, nu: (layer, be[i], 0, 0)),
                  pl.BlockSpec((1, 1, 1, n2), lambda i, be, nu: (layer, be[i], 0, 0)),
                  pl.BlockSpec((1, 1, de, d), lambda i, be, nu: (layer, be[i], 0, 0)),
                  pl.BlockSpec((1, 1, 1, d), lambda i, be, nu: (layer, be[i], 0, 0))],
        out_specs=pl.BlockSpec((tb, half), lambda i, be, nu: (i, 0)),
        scratch_shapes=[pltpu.VMEM((d, n2), BF16), pltpu.VMEM((de, d), BF16)],
    )
    return pl.pallas_call(
        _ffn_kernel,
        grid_spec=grid_spec,
        out_shape=jax.ShapeDtypeStruct((p, half), jnp.uint32),
        compiler_params=_cparams(("arbitrary",)),
        name="moe_ffn",
    )(block_expert, n_used, xs, wgu, bgu.reshape(nl, e, 1, n2), wdn, bdn.reshape(nl, e, 1, d))


def _combine_kernel(base_ref, cnt_ref, base_next_ref, cnt_next_ref, lpos_ref, gate_ref, x_ref, g2_ref,
                    lng_ref, lnb_ref, ys_ref, o_ref, gbuf_ref, sem, *, alpha):
    tm = x_ref.shape[1]
    n_slots = _tile_slots(tm)
    step = pl.program_id(0) * pl.num_programs(1) + pl.program_id(1)
    n_steps = pl.num_programs(0) * pl.num_programs(1)
    cur = step % 2

    def gather(bases, cnts, buf):
        def copy(dst, src, size):
            return pltpu.make_async_copy(ys_ref.at[pl.ds(src, size)], gbuf_ref.at[buf, pl.ds(dst, size)],
                                         sem.at[buf])
        _segment_copies(bases, cnts, copy, tm, n_slots, 0)
        return copy

    @pl.when(step == 0)
    def _():
        gather(base_ref, cnt_ref, cur)

    @pl.when(step + 1 < n_steps)
    def _():
        gather(base_next_ref, cnt_next_ref, 1 - cur)

    def copy(dst, src, size):
        return pltpu.make_async_copy(ys_ref.at[pl.ds(src, size)], gbuf_ref.at[cur, pl.ds(dst, size)],
                                     sem.at[cur])

    lane = lax.broadcasted_iota(jnp.int32, (tm, n_slots), 1)
    lpos = lpos_ref[...]
    gate = gate_ref[...]
    w = jnp.zeros((tm, n_slots), F32)
    for k in range(TOP_K):
        w = jnp.where(lane == lpos[:, k:k + 1], gate[:, k:k + 1], w)
    w = w.astype(BF16)

    copy(0, 0, n_slots).wait()
    y_left, y_right = _unpack_rows(gbuf_ref[cur])
    f = jnp.concatenate([jnp.dot(w, y, preferred_element_type=F32) for y in (y_left, y_right)], axis=1)
    o_ref[0] = _layer_norm(alpha * x_ref[0] + g2_ref[0] * f, lng_ref[...], lnb_ref[...])


def _combine(base, cnt, lpos, gates, x, g2, lng, lnb, ys, n_lat_tiles, n_tiles, tm, alpha):
    b, _, d = x.shape
    nb = g2.shape[0] - 1

    def mod_map(i, j):
        return (jnp.where(j < n_lat_tiles, i, nb), 0, 0)

    tok = pl.BlockSpec((1, tm, d), lambda i, j: (i, j, 0))
    vec = pl.BlockSpec((1, d), lambda i, j: (0, 0))
    last = b * n_tiles - 1
    seg = pl.BlockSpec((LANES,), lambda i, j: (i * n_tiles + j,), memory_space=pltpu.SMEM)
    seg_next = pl.BlockSpec((LANES,), lambda i, j: (jnp.minimum(i * n_tiles + j + 1, last),),
                            memory_space=pltpu.SMEM)
    per_tok = pl.BlockSpec((tm, LANES), lambda i, j: (i * n_tiles + j, 0))
    return pl.pallas_call(
        functools.partial(_combine_kernel, alpha=alpha),
        grid=(b, n_tiles),
        in_specs=[seg, seg, seg_next, seg_next, per_tok, per_tok,
                  tok, pl.BlockSpec((1, 1, d), mod_map), vec, vec,
                  pl.BlockSpec(memory_space=pl.ANY)],
        out_specs=tok,
        out_shape=jax.ShapeDtypeStruct((b, n_tiles * tm, d), F32),
        scratch_shapes=[pltpu.VMEM((2, _tile_slots(tm), d // 2), jnp.uint32), pltpu.SemaphoreType.DMA((2,))],
        compiler_params=_cparams(("arbitrary", "arbitrary")),
        name="moe_combine_ln",
    )(base, cnt, base, cnt, lpos, gates, x, g2, lng, lnb, ys)


def _moe_layer(h2, logits, x_res, g2, lng, lnb, layer, wgu, bgu, wdn, bdn, n_lat_tiles, n_tiles, tm, alpha):
    b, rows, d = h2.shape
    t = b * rows
    tb = 512
    _, gates, lpos, lpost, tiles, counts = _router(logits.reshape(t, LANES), tm)
    counts = counts[0, :N_EXPERTS].astype(jnp.int32)
    padded = (counts + tb - 1) // tb * tb
    pend = jnp.cumsum(padded)
    pstart = pend - padded
    tiles = tiles.reshape(t // tm, 8, LANES)[:, :2, :].astype(jnp.int32)
    seg_base = (jnp.pad(pstart, (0, LANES - N_EXPERTS))[None, :] + tiles[:, 0]).reshape(-1)
    seg_cnt = tiles[:, 1].reshape(-1)
    n_rows = t * TOP_K + (t // tm) * N_EXPERTS * (SEG_ALIGN - 1)
    n_rows = (n_rows + tb - 1) // tb * tb + N_EXPERTS * tb
    n_blocks = n_rows // tb
    block_start = jnp.arange(n_blocks, dtype=jnp.int32) * tb
    block_expert = jnp.minimum(jnp.sum((pend[None, :] <= block_start[:, None]).astype(jnp.int32), axis=1),
                               N_EXPERTS - 1).astype(jnp.int32)
    n_used = (pend[-1:] // tb).astype(jnp.int32)
    tail = jnp.pad(pstart + counts, (0, LANES - N_EXPERTS))
    tail_len = jnp.pad(padded - counts, (0, LANES - N_EXPERTS))
    xs = _dispatch(h2.reshape(t, d), seg_base, seg_cnt, tail, tail_len, lpost, n_rows, tm, tb)
    ys = _ffn(xs, n_rows, block_expert, n_used, layer, wgu, bgu, wdn, bdn, tb)
    return _combine(seg_base, seg_cnt, lpos, gates, x_res, g2, lng, lnb, ys, n_lat_tiles, n_tiles, tm, alpha)


def _rope_tables(n_lat_rows, n_rows, scale):
    pos = jnp.arange(n_lat_rows)
    lane = jnp.arange(LANES)
    m = lane % 64
    n_freq = 16
    inv = ROPE_BASE ** (-(m % n_freq).astype(F32) / n_freq)
    p = jnp.where((m // 32)[None, :] == 0, (pos // GRID_W)[:, None], (pos % GRID_W)[:, None]).astype(F32)
    ang = p * inv[None, :]
    sign = jnp.where((m % 32) < n_freq, -1.0, 1.0)[None, :]
    cos = jnp.concatenate([jnp.cos(ang), jnp.ones((n_rows - n_lat_rows, LANES), F32)], axis=0)
    sin = jnp.concatenate([jnp.sin(ang) * sign, jnp.zeros((n_rows - n_lat_rows, LANES), F32)], axis=0)
    return cos * scale, sin * scale


def kernel(x, c, ctx, c_ctx, w_ada, b_ada, ln_g, ln_b, hg_w_in, hg_lb, hg_norm_g, hg_w_out, da_w_in, da_lam,
           da_norm_g, da_w_out, moe_w_router, moe_b_router, moe_w_gu, moe_b_gu, moe_w_dn, moe_b_dn):
    bsz, seq, d = x.shape
    lc = ctx.shape[1]
    depth = w_ada.shape[0]
    assert depth == 2 and hg_w_in.shape[0] == 1 and da_w_in.shape[0] == 1
    ltot = seq + lc
    tm = 256
    assert seq % tm == 0 and lc % tm == 0 and seq % GRID_W == 0 and d == DA_HEADS * LANES
    alpha = (2 * depth) ** 0.25
    n_lat_tiles = seq // tm
    n_all_tiles = ltot // tm
    h_heads = d // LANES

    n_mod = bsz + 1
    c_pad = jnp.zeros((16, d), F32).at[:bsz].set(c).at[bsz].set(c_ctx)
    mod = _ada(c_pad, w_ada, b_ada)[:, :n_mod]

    def mods(l):
        return [mod[l, :, k * d:(k + 1) * d].reshape(n_mod, 1, d) for k in range(6)]

    def router_params(l):
        wr = jnp.zeros((d, LANES), F32).at[:, :N_EXPERTS].set(moe_w_router[l])
        wr_hi = wr.astype(BF16)
        wr_lo = (wr - wr_hi.astype(F32)).astype(BF16)
        br = jnp.full((1, LANES), -1e30, F32).at[0, :N_EXPERTS].set(moe_b_router[l])
        return jnp.stack([wr_hi, wr_lo]), br

    xall = (x, ctx)

    sh1, sc1, g1, sh2, sc2, g2 = mods(0)
    p0 = _proj(xall, sc1, sh1, hg_w_in[0].astype(BF16), n_lat_tiles, tm)
    lb = jnp.cumsum(jax.nn.softmax(hg_lb.astype(F32), axis=1), axis=1)[:, 0]
    lb = lb.reshape(2, h_heads, LANES).transpose(1, 0, 2)
    y0 = _gla(p0, lb, hg_norm_g[0].reshape(h_heads, 1, LANES), seq)
    wr, br = router_params(0)
    x0, h20, lg0 = _post(y0, hg_w_out[0].astype(BF16), xall, g1, ln_g[0, 0].reshape(1, d),
                         ln_b[0, 0].reshape(1, d), sc2, sh2, wr, br, n_lat_tiles, n_all_tiles, tm, alpha)
    x1 = _moe_layer(h20, lg0, x0, g2, ln_g[0, 1].reshape(1, d), ln_b[0, 1].reshape(1, d),
                    0, moe_w_gu, moe_b_gu, moe_w_dn, moe_b_dn, n_lat_tiles, n_all_tiles, tm, alpha)

    sh1, sc1, g1, sh2, sc2, g2 = mods(1)
    dh = d // DA_HEADS // 2
    cq, sq = _rope_tables(seq, ltot, dh ** -0.5 * math.log2(math.e))
    ck, sk = _rope_tables(seq, ltot, 1.0)
    p1 = _proj(x1, sc1, sh1, da_w_in[0].astype(BF16), n_lat_tiles, tm, rope_tabs=(cq, sq, ck, sk))
    lam_init = 0.8 - 0.6 * math.exp(-0.3 * 1)
    lp = da_lam[0].astype(F32)
    lam = (jnp.exp(jnp.sum(lp[0] * lp[1])) - jnp.exp(jnp.sum(lp[2] * lp[3])) + lam_init).reshape(1)
    y1 = _attn(p1, lam, da_norm_g[0].reshape(LANES, 1), seq, 1.0 - lam_init)
    wr, br = router_params(1)
    x2, h21, lg1 = _post(y1, da_w_out[0].astype(BF16), x1, g1, ln_g[1, 0].reshape(1, d),
                         ln_b[1, 0].reshape(1, d), sc2, sh2, wr, br, n_lat_tiles, n_lat_tiles, tm, alpha)
    return _moe_layer(h21, lg1, x2, g2, ln_g[1, 1].reshape(1, d), ln_b[1, 1].reshape(1, d),
                      1, moe_w_gu, moe_b_gu, moe_w_dn, moe_b_dn, n_lat_tiles, n_lat_tiles, tm, alpha)
```

```python
import functools
import math

import jax
import jax.numpy as jnp
from jax import lax
from jax.experimental import pallas as pl
from jax.experimental.pallas import tpu as pltpu

F32 = jnp.float32
BF16 = jnp.bfloat16
HIGHEST = lax.Precision.HIGHEST

LANES = 128
HG_CHUNK = 64
GRID_W = 64
ROPE_BASE = 10000.0
DA_HEADS = 8
N_EXPERTS = 32
TOP_K = 4
SEG_ALIGN = 8
SWIGLU_LIMIT = 7.0
SWIGLU_ALPHA = 1.702
LN_EPS = 1e-5
RMS_EPS = 1e-6
VMEM_LIMIT = 56 * 1024 * 1024
MXU_COLS = 256

TOKEN_TILE = 256
FFN_BLOCK = 512
ADA_TILE_N = 1536
PROJ_CHUNK = 512
ATTN_TQ = 1024
ATTN_TK = 768
READOUT_ROWS = 256


def _cparams(sem):
    return pltpu.CompilerParams(dimension_semantics=sem, vmem_limit_bytes=VMEM_LIMIT)


def _ada_kernel(c_ref, w_ref, b_ref, o_ref):
    c = c_ref[...]
    s = c * jax.nn.sigmoid(c)
    o_ref[0] = jnp.dot(s, w_ref[0], precision=HIGHEST, preferred_element_type=F32) + b_ref[0]


def _ada(c_pad, w_ada, b_ada):
    depth, d, n = w_ada.shape
    rows = c_pad.shape[0]
    tn = ADA_TILE_N
    return pl.pallas_call(
        _ada_kernel,
        grid=(depth, n // tn),
        in_specs=[pl.BlockSpec((rows, d), lambda l, j: (0, 0)),
                  pl.BlockSpec((1, d, tn), lambda l, j: (l, 0, j)),
                  pl.BlockSpec((1, 1, tn), lambda l, j: (l, 0, j))],
        out_specs=pl.BlockSpec((1, rows, tn), lambda l, j: (l, 0, j)),
        out_shape=jax.ShapeDtypeStruct((depth, rows, n), F32),
        compiler_params=_cparams(("arbitrary", "arbitrary")),
        name="ada_mod",
    )(c_pad, w_ada, b_ada.reshape(depth, 1, n))


def _rope(piece, cos, sin, lo_mask):
    rot = jnp.where(lo_mask, pltpu.roll(piece, LANES - 16, 1), pltpu.roll(piece, 16, 1))
    return piece * cos + rot * sin


def _token_tile(x_ref, xc_ref, n_lat_tiles):
    if xc_ref is None:
        return x_ref[0]
    return jnp.where(pl.program_id(1) < n_lat_tiles, x_ref[0], xc_ref[0])


def _stream_specs(x, tm, n_lat_tiles):
    if not isinstance(x, tuple):
        return [pl.BlockSpec((1, tm, x.shape[2]), lambda i, j: (i, j, 0))], [x]
    d = x[0].shape[2]
    return ([pl.BlockSpec((1, tm, d), lambda i, j: (i, jnp.minimum(j, n_lat_tiles - 1), 0)),
             pl.BlockSpec((1, tm, d), lambda i, j: (i, jnp.maximum(j - n_lat_tiles, 0), 0))], list(x))


def _proj_kernel(*refs, n_rope, cw, two_streams, n_lat_tiles):
    x_ref, refs = refs[0], refs[1:]
    xc_ref = None
    if two_streams:
        xc_ref, refs = refs[0], refs[1:]
    sc_ref, sh_ref, w_ref, *rest = refs
    if n_rope:
        cq_ref, sq_ref, ck_ref, sk_ref, o_ref = rest
    else:
        (o_ref,) = rest
    n = w_ref.shape[1]
    h = (_token_tile(x_ref, xc_ref, n_lat_tiles) * (1.0 + sc_ref[0]) + sh_ref[0]).astype(BF16)
    if n_rope:
        lane = lax.broadcasted_iota(jnp.int32, (x_ref.shape[1], LANES), 1)
        lo_mask = (lane % 32) < 16
    per = cw // LANES
    for j in range(n // cw):
        r = jnp.dot(h, w_ref[:, j * cw:(j + 1) * cw], preferred_element_type=F32)
        for g in range(per):
            hd = j * per + g
            piece = r[:, g * LANES:(g + 1) * LANES]
            if hd < n_rope:
                piece = _rope(piece, cq_ref[...], sq_ref[...], lo_mask)
            elif hd < 2 * n_rope:
                piece = _rope(piece, ck_ref[...], sk_ref[...], lo_mask)
            o_ref[0, hd] = piece.astype(BF16)


def _proj(x, sc, sh, w, n_lat_tiles, tm, rope_tabs=None):
    two_streams = isinstance(x, tuple)
    b, _, d = (x[0] if two_streams else x).shape
    ltot = x[0].shape[1] + x[1].shape[1] if two_streams else x.shape[1]
    n = w.shape[1]
    nb = sc.shape[0] - 1
    n_rope = DA_HEADS if rope_tabs is not None else 0

    def mod_map(i, j):
        return (jnp.where(j < n_lat_tiles, i, nb), 0, 0)

    in_specs, args = _stream_specs(x, tm, n_lat_tiles)
    in_specs += [pl.BlockSpec((1, 1, d), mod_map),
                 pl.BlockSpec((1, 1, d), mod_map),
                 pl.BlockSpec((d, n), lambda i, j: (0, 0))]
    args += [sc, sh, w]
    if rope_tabs is not None:
        in_specs += [pl.BlockSpec((tm, LANES), lambda i, j: (j, 0))] * 4
        args += list(rope_tabs)
    return pl.pallas_call(
        functools.partial(_proj_kernel, n_rope=n_rope, cw=PROJ_CHUNK, two_streams=two_streams,
                          n_lat_tiles=n_lat_tiles),
        grid=(b, ltot // tm),
        in_specs=in_specs,
        out_specs=pl.BlockSpec((1, n // LANES, tm, LANES), lambda i, j: (i, 0, j, 0)),
        out_shape=jax.ShapeDtypeStruct((b, n // LANES, ltot, LANES), BF16),
        compiler_params=_cparams(("arbitrary", "arbitrary")),
        name="mod_proj_rope" if n_rope else "mod_proj",
    )(*args)


GLA_BLOCK = MXU_COLS


def _gla_local(items):
    nc = GLA_BLOCK // HG_CHUNK
    kks, cats = [], []
    for (_, _, z, lb, _, _, _) in items:
        f = lb + (1.0 - lb) * jax.nn.sigmoid(z)
        logf = jnp.log(f)
        kks.append(1.0 - f)
        hi = logf.astype(BF16)
        lo = (logf - hi.astype(F32)).astype(BF16)
        cats.append(jnp.concatenate([hi, lo], axis=1))
    parts = [jnp.dot(it[4], cat, preferred_element_type=F32) for it, cat in zip(items, cats)]
    q_decs, k_invs, k_ends, decs = [], [], [], []
    for (q, _, _, _, _, _, end_row), kk, part in zip(items, kks, parts):
        bcum = part[:, :LANES] + part[:, LANES:]
        b_end = bcum.reshape(nc, HG_CHUNK, LANES)[:, end_row:end_row + 1, :]
        b_end_rows = jnp.broadcast_to(b_end, (nc, HG_CHUNK, LANES)).reshape(GLA_BLOCK, LANES)
        q_decs.append((q * jnp.exp(bcum)).astype(BF16))
        k_invs.append((kk * jnp.exp(-bcum)).astype(BF16))
        k_ends.append((kk * jnp.exp(b_end_rows - bcum)).astype(BF16))
        decs.append(jnp.exp(b_end.reshape(nc, LANES)))
    scores = [lax.dot_general(qd, ki, (((1,), (1,)), ((), ())), preferred_element_type=F32)
              for qd, ki in zip(q_decs, k_invs)]
    scores = [jnp.where(it[5], a, 0.0).astype(BF16) for it, a in zip(items, scores)]
    outs = [jnp.dot(a, it[1], preferred_element_type=F32) for it, a in zip(items, scores)]
    return list(zip(outs, q_decs, k_ends, decs))


def _gla_kernel(q_ref, i_ref, g_ref, zf_ref, zb_ref, lb_ref, ng_ref, y_ref,
                of_ref, ob_ref, qd_ref, ke_ref, de_ref, st_ref, *, n_lat, n_ctx):
    c = HG_CHUNK
    blk = GLA_BLOCK
    nc = blk // c
    ltot = (n_lat + n_ctx) * c
    row = lax.broadcasted_iota(jnp.int32, (blk, blk), 0)
    col = lax.broadcasted_iota(jnp.int32, (blk, blk), 1)
    same = (row // c) == (col // c)
    masks = (same & (col <= row), same & (col >= row))
    tris = (masks[0].astype(BF16), masks[1].astype(BF16))
    z_refs = (zf_ref, zb_ref)
    o_refs = (of_ref, ob_ref)
    end_rows = (c - 1, 0)

    n_blk = ltot // blk
    per_step = 3 if n_blk % 3 == 0 else 1

    def local(t, carry):
        work = []
        for u in range(per_step):
            r0 = pl.multiple_of((t * per_step + u) * blk, blk)
            c0 = pl.multiple_of((t * per_step + u) * nc, nc)
            q = q_ref[0, 0, pl.ds(r0, blk), :].astype(F32)
            v = i_ref[0, 0, pl.ds(r0, blk), :]
            zs = [z_refs[d][0, 0, pl.ds(r0, blk), :].astype(F32) for d in range(2)]
            work.append((r0, c0, q, v, zs))
        res = _gla_local([(q, v, zs[d], lb_ref[0, d:d + 1, :], tris[d], masks[d], end_rows[d])
                          for (_, _, q, v, zs) in work for d in range(2)])
        for u, (r0, c0, _, _, _) in enumerate(work):
            for d in range(2):
                o, q_dec, k_end, dec = res[2 * u + d]
                o_refs[d][pl.ds(r0, blk), :] = o
                qd_ref[d, pl.ds(r0, blk), :] = q_dec
                ke_ref[d, pl.ds(r0, blk), :] = k_end
                de_ref[d, pl.ds(c0, nc), :] = dec
        return carry
    lax.fori_loop(0, n_blk // per_step, local, 0)

    st_ref[...] = jnp.zeros_like(st_ref)

    def segment(first, n):
        per = 8 if n % 8 == 0 else (4 if n % 4 == 0 else 1)

        def body(jj, carry):
            todo = []
            for u in range(per):
                j = jj * per + u
                for d, ch in enumerate((first + j, first + n - 1 - j)):
                    r0 = pl.multiple_of(ch * c, c)
                    kv = lax.dot_general(i_ref[0, 0, pl.ds(r0, c), :], ke_ref[d, pl.ds(r0, c), :],
                                         (((0,), (0,)), ((), ())), preferred_element_type=F32)
                    todo.append((d, r0, kv, de_ref[d, pl.ds(ch, 1), :], qd_ref[d, pl.ds(r0, c), :]))
            st = [st_ref[0], st_ref[1]]
            inter = []
            for (d, r0, kv, dec, q_dec) in todo:
                inter.append((d, r0, lax.dot_general(q_dec, st[d].astype(BF16), (((1,), (1,)), ((), ())),
                                                     preferred_element_type=F32)))
                st[d] = st[d] * dec + kv
            for (d, r0, o) in inter:
                o_refs[d][pl.ds(r0, c), :] += o
            st_ref[0] = st[0]
            st_ref[1] = st[1]
            return carry
        lax.fori_loop(0, n // per, body, 0)

    segment(n_lat, n_ctx)
    segment(0, n_lat)

    ng = ng_ref[0]
    rt = READOUT_ROWS

    def readout(t, carry):
        r0 = pl.multiple_of(t * rt, rt)
        o = of_ref[pl.ds(r0, rt), :] + ob_ref[pl.ds(r0, rt), :]
        ms = jnp.mean(o * o, axis=-1, keepdims=True)
        g = g_ref[0, 0, pl.ds(r0, rt), :].astype(F32)
        y = o * lax.rsqrt(ms + RMS_EPS) * ng * (g * jax.nn.sigmoid(g))
        y_ref[0, 0, pl.ds(r0, rt), :] = y.astype(BF16)
        return carry
    lax.fori_loop(0, ltot // rt, readout, 0, unroll=3 if (ltot // rt) % 3 == 0 else 1)


def _gla(p, lb, ng, n_lat_rows):
    b, nh5, ltot, _ = p.shape
    h = nh5 // 5
    n_lat = n_lat_rows // HG_CHUNK
    n_ctx = (ltot - n_lat_rows) // HG_CHUNK

    def spec(k):
        return pl.BlockSpec((1, 1, ltot, LANES), lambda i, j, k=k: (i, k * h + j, 0, 0))

    return pl.pallas_call(
        functools.partial(_gla_kernel, n_lat=n_lat, n_ctx=n_ctx),
        grid=(b, h),
        in_specs=[spec(0), spec(1), spec(2), spec(3), spec(4),
                  pl.BlockSpec((1, 2, LANES), lambda i, j: (j, 0, 0)),
                  pl.BlockSpec((1, 1, LANES), lambda i, j: (j, 0, 0))],
        out_specs=pl.BlockSpec((1, 1, ltot, LANES), lambda i, j: (i, j, 0, 0)),
        out_shape=jax.ShapeDtypeStruct((b, h, ltot, LANES), BF16),
        scratch_shapes=[pltpu.VMEM((ltot, LANES), F32), pltpu.VMEM((ltot, LANES), F32),
                        pltpu.VMEM((2, ltot, LANES), BF16), pltpu.VMEM((2, ltot, LANES), BF16),
                        pltpu.VMEM((2, ltot // HG_CHUNK, LANES), F32),
                        pltpu.VMEM((2, LANES, LANES), F32)],
        compiler_params=_cparams(("arbitrary", "arbitrary")),
        name="hgrn2_gla",
    )(p, p, p, p, p, lb, ng)


SUM_ROWS = 16


BOUND_SLACK = 1.0 + 2.0 ** -5
MIN_SOFTMAX_SUM = 1e-30


def _attn_kernel(lam_ref, q_ref, k_ref, v_ref, ng_ref, y_ref, vt_ref, kmax_ref, acc_ref, *, tk, out_scale):
    tq = q_ref.shape[2]
    lk = k_ref.shape[2]
    lam = lam_ref[0]
    n_chunks = lk // tk
    srow = lax.broadcasted_iota(jnp.int32, (8, LANES), 0)
    slane = lax.broadcasted_iota(jnp.int32, (8, LANES), 1)
    sel = (srow == slane // 64).astype(BF16)

    def sq_norms(x):
        return lax.dot_general(sel, x * x, (((1,), (1,)), ((), ())), preferred_element_type=F32)

    @pl.when(pl.program_id(2) == 0)
    def _():
        blk = MXU_COLS
        kn = jnp.zeros((8, blk), F32)
        for cb in range(lk // blk):
            vt_ref[0:LANES, cb * blk:(cb + 1) * blk] = jnp.transpose(
                v_ref[0, 0, cb * blk:(cb + 1) * blk, :].astype(F32)).astype(BF16)
            kn = jnp.maximum(kn, sq_norms(k_ref[0, 0, cb * blk:(cb + 1) * blk, :]))
        vt_ref[LANES:LANES + SUM_ROWS, :] = jnp.ones((SUM_ROWS, lk), BF16)
        kmax_ref[...] = jnp.broadcast_to(jnp.sqrt(jnp.max(kn, axis=1, keepdims=True)), kmax_ref.shape)

    q = q_ref[0, 0]
    lane = lax.broadcasted_iota(jnp.int32, (tq, LANES), 1)
    zero = jnp.zeros_like(q)
    qm = (jnp.where(lane < 64, q, zero), jnp.where(lane >= 64, q, zero))

    def scores(ci, c):
        kc = k_ref[0, 0, ci * tk:(ci + 1) * tk, :]
        return lax.dot_general(kc, qm[c], (((1,), (1,)), ((), ())), preferred_element_type=F32)

    bound = jnp.sqrt(sq_norms(q)) * kmax_ref[...] * BOUND_SLACK
    acc = [jnp.zeros((LANES + SUM_ROWS, tq), F32) for _ in range(2)]
    for ci in range(n_chunks):
        vt = vt_ref[:, ci * tk:(ci + 1) * tk]
        for c in range(2):
            p = jnp.exp2(scores(ci, c) - bound[c:c + 1]).astype(BF16)
            acc[c] = acc[c] + jnp.dot(vt, p, preferred_element_type=F32)
    acc_ref[0] = acc[0]
    acc_ref[1] = acc[1]
    sums = jnp.minimum(acc[0][LANES:LANES + 1], acc[1][LANES:LANES + 1])
    underflow = jnp.logical_not(jnp.min(sums) > MIN_SOFTMAX_SUM)

    @pl.when(underflow)
    def _():
        def body(ci, carry):
            r0 = pl.multiple_of(ci * tk, tk)
            kc = k_ref[0, 0, pl.ds(r0, tk), :]
            vt = vt_ref[:, pl.ds(r0, tk)]
            out = []
            for c in range(2):
                m_c, a_c = carry[c]
                s_c = lax.dot_general(kc, qm[c], (((1,), (1,)), ((), ())), preferred_element_type=F32)
                m_new = jnp.maximum(m_c, jnp.max(s_c, axis=0, keepdims=True))
                p = jnp.exp2(s_c - m_new).astype(BF16)
                out.append((m_new, jnp.exp2(m_c - m_new) * a_c + jnp.dot(vt, p, preferred_element_type=F32)))
            return tuple(out)
        init = tuple((jnp.full((1, tq), -1e30, F32), jnp.zeros((LANES + SUM_ROWS, tq), F32))
                     for _ in range(2))
        (_, b0), (_, b1) = lax.fori_loop(0, n_chunks, body, init)
        acc_ref[0] = b0
        acc_ref[1] = b1

    a0 = acc_ref[0]
    a1 = acc_ref[1]
    o = a0[:LANES] / a0[LANES:LANES + 1] - lam * (a1[:LANES] / a1[LANES:LANES + 1])
    ms = jnp.mean(o * o, axis=0, keepdims=True)
    y = o * lax.rsqrt(ms + RMS_EPS) * ng_ref[...] * out_scale
    y_ref[0, 0] = jnp.transpose(y).astype(BF16)


def _attn(p, lam, ng, n_lat_rows, out_scale, tq=ATTN_TQ, tk=ATTN_TK):
    b, nh3, ltot, _ = p.shape
    h = nh3 // 3
    assert ltot % tk == 0 and n_lat_rows % tq == 0 and ltot % MXU_COLS == 0
    return pl.pallas_call(
        functools.partial(_attn_kernel, tk=tk, out_scale=out_scale),
        grid=(b, h, n_lat_rows // tq),
        in_specs=[pl.BlockSpec(memory_space=pltpu.SMEM),
                  pl.BlockSpec((1, 1, tq, LANES), lambda i, j, t: (i, j, t, 0)),
                  pl.BlockSpec((1, 1, ltot, LANES), lambda i, j, t: (i, h + j, 0, 0)),
                  pl.BlockSpec((1, 1, ltot, LANES), lambda i, j, t: (i, 2 * h + j, 0, 0)),
                  pl.BlockSpec((LANES, 1), lambda i, j, t: (0, 0))],
        out_specs=pl.BlockSpec((1, 1, tq, LANES), lambda i, j, t: (i, j, t, 0)),
        out_shape=jax.ShapeDtypeStruct((b, h, n_lat_rows, LANES), BF16),
        scratch_shapes=[pltpu.VMEM((LANES + SUM_ROWS, ltot), BF16), pltpu.VMEM((8, tq), F32),
                        pltpu.VMEM((2, LANES + SUM_ROWS, tq), F32)],
        compiler_params=_cparams(("arbitrary", "arbitrary", "arbitrary")),
        name="diff_attn",
    )(lam, p, p, p, ng)


def _layer_norm(x, g, b):
    mu = jnp.mean(x, axis=-1, keepdims=True)
    xc = x - mu
    var = jnp.mean(xc * xc, axis=-1, keepdims=True)
    return xc * lax.rsqrt(var + LN_EPS) * g + b


def _post_kernel(y_ref, w_ref, *refs, alpha, two_streams, n_lat_tiles):
    x_ref, refs = refs[0], refs[1:]
    xc_ref = None
    if two_streams:
        xc_ref, refs = refs[0], refs[1:]
    g1_ref, lng_ref, lnb_ref, sc2_ref, sh2_ref, wr_ref, br_ref, xo_ref, h2_ref, lg_ref = refs
    nh = y_ref.shape[1]
    y = jnp.concatenate([y_ref[0, k] for k in range(nh)], axis=-1)
    m = jnp.dot(y, w_ref[...], preferred_element_type=F32)
    x_res = _token_tile(x_ref, xc_ref, n_lat_tiles)
    xl = _layer_norm(alpha * x_res + g1_ref[0] * m, lng_ref[...], lnb_ref[...])
    xo_ref[0] = xl
    h2 = xl * (1.0 + sc2_ref[0]) + sh2_ref[0]
    h_hi = h2.astype(BF16)
    h2_ref[0] = h_hi
    h_lo = (h2 - h_hi.astype(F32)).astype(BF16)
    lg_ref[0] = (jnp.dot(h_hi, wr_ref[0], preferred_element_type=F32)
                 + jnp.dot(h_lo, wr_ref[0], preferred_element_type=F32)
                 + jnp.dot(h_hi, wr_ref[1], preferred_element_type=F32) + br_ref[...])


def _post(y, w_out, x, g1, lng, lnb, sc2, sh2, wr, br, n_lat_tiles, n_tiles, tm, alpha):
    b, nh, _, _ = y.shape
    two_streams = isinstance(x, tuple)
    d = (x[0] if two_streams else x).shape[2]
    nb = g1.shape[0] - 1
    rows = n_tiles * tm

    def mod_map(i, j):
        return (jnp.where(j < n_lat_tiles, i, nb), 0, 0)

    tok = pl.BlockSpec((1, tm, d), lambda i, j: (i, j, 0))
    vec = pl.BlockSpec((1, d), lambda i, j: (0, 0))
    x_specs, x_args = _stream_specs(x, tm, n_lat_tiles)
    return pl.pallas_call(
        functools.partial(_post_kernel, alpha=alpha, two_streams=two_streams, n_lat_tiles=n_lat_tiles),
        grid=(b, n_tiles),
        in_specs=[pl.BlockSpec((1, nh, tm, LANES), lambda i, j: (i, 0, j, 0)),
                  pl.BlockSpec((d, d), lambda i, j: (0, 0)),
                  *x_specs, pl.BlockSpec((1, 1, d), mod_map), vec, vec,
                  pl.BlockSpec((1, 1, d), mod_map), pl.BlockSpec((1, 1, d), mod_map),
                  pl.BlockSpec((2, d, LANES), lambda i, j: (0, 0, 0)),
                  pl.BlockSpec((1, LANES), lambda i, j: (0, 0))],
        out_specs=[tok, tok, pl.BlockSpec((1, tm, LANES), lambda i, j: (i, j, 0))],
        out_shape=[jax.ShapeDtypeStruct((b, rows, d), F32),
                   jax.ShapeDtypeStruct((b, rows, d), BF16),
                   jax.ShapeDtypeStruct((b, rows, LANES), F32)],
        compiler_params=_cparams(("arbitrary", "arbitrary")),
        name="out_proj_ln",
    )(y, w_out, *x_args, g1, lng, lnb, sc2, sh2, wr, br)


def _router_kernel(lg_ref, gate_ref, lpos_ref, lpost_ref, tile_ref, cnt_ref, carry_ref):
    tm = lg_ref.shape[0]
    ne = N_EXPERTS

    @pl.when(pl.program_id(0) == 0)
    def _():
        carry_ref[...] = jnp.zeros_like(carry_ref)

    work = jnp.transpose(lg_ref[...])[0:ne, :]
    row_f = lax.broadcasted_iota(jnp.int32, (ne, tm), 0).astype(F32)
    vals, onehots = [], []
    for k in range(TOP_K):
        mx = jnp.max(work, axis=0, keepdims=True)
        am = jnp.min(jnp.where(work == mx, row_f, float(ne)), axis=0, keepdims=True)
        oh = row_f == am
        vals.append(mx)
        onehots.append(oh)
        work = jnp.where(oh, -3e38, work)
    es = [jnp.exp(v - vals[0]) for v in vals]
    denom = es[0] + es[1] + es[2] + es[3]
    member = (onehots[0] | onehots[1] | onehots[2] | onehots[3])
    s_idx = lax.broadcasted_iota(jnp.int32, (tm, tm), 0)
    t_idx = lax.broadcasted_iota(jnp.int32, (tm, tm), 1)
    before = jnp.dot(member.astype(BF16), (s_idx < t_idx).astype(BF16), preferred_element_type=F32)
    cnt = jnp.sum(member.astype(F32), axis=1, keepdims=True)
    cnt = jnp.floor((cnt + (SEG_ALIGN - 1)) * (1.0 / SEG_ALIGN)) * SEG_ALIGN
    cnt_b = jnp.broadcast_to(cnt, (ne, LANES))
    ei = lax.broadcasted_iota(jnp.int32, (ne, ne), 0)
    ej = lax.broadcasted_iota(jnp.int32, (ne, ne), 1)
    seg_start = jnp.dot((ej < ei).astype(BF16), cnt_b.astype(BF16), preferred_element_type=F32)[:, 0:1]
    slot = seg_start + before
    sub8 = lax.broadcasted_iota(jnp.int32, (8, tm), 0)
    lp8 = jnp.zeros((8, tm), F32)
    g8 = jnp.zeros((8, tm), F32)
    for k in range(TOP_K):
        lp8 = jnp.where(sub8 == k, jnp.sum(jnp.where(onehots[k], slot, 0.0), axis=0, keepdims=True), lp8)
        g8 = jnp.where(sub8 == k, es[k] / denom, g8)
    lpost_ref[...] = lp8.astype(jnp.int32)
    pad = jnp.zeros((LANES - 8, tm), F32)
    lpos_ref[...] = jnp.transpose(jnp.concatenate([lp8, pad], axis=0)).astype(jnp.int32)
    gate_ref[...] = jnp.transpose(jnp.concatenate([g8, pad], axis=0))
    cnt_lane = jnp.transpose(jnp.concatenate([cnt_b, jnp.zeros((LANES - ne, LANES), F32)], axis=0))[0:1, :]
    row8 = lax.broadcasted_iota(jnp.int32, (8, LANES), 0)
    tile_ref[...] = jnp.where(row8 == 0, carry_ref[...], jnp.where(row8 == 1, cnt_lane, 0.0))
    carry_ref[...] = carry_ref[...] + cnt_lane
    cnt_ref[...] = carry_ref[...]


def _router(logits, tm):
    t = logits.shape[0]
    tile = pl.BlockSpec((tm, LANES), lambda i: (i, 0))
    return pl.pallas_call(
        _router_kernel,
        grid=(t // tm,),
        in_specs=[tile],
        out_specs=[tile, tile, pl.BlockSpec((8, tm), lambda i: (0, i)),
                   pl.BlockSpec((8, LANES), lambda i: (i, 0)), pl.BlockSpec((1, LANES), lambda i: (0, 0))],
        out_shape=[jax.ShapeDtypeStruct((t, LANES), F32),
                   jax.ShapeDtypeStruct((t, LANES), jnp.int32),
                   jax.ShapeDtypeStruct((8, t), jnp.int32),
                   jax.ShapeDtypeStruct((t // tm * 8, LANES), F32),
                   jax.ShapeDtypeStruct((1, LANES), F32)],
        scratch_shapes=[pltpu.VMEM((1, LANES), F32)],
        compiler_params=_cparams(("arbitrary",)),
        name="router_topk",
    )(logits)


HI_MASK = 0xFFFF0000


def _pack_rows(x, is_bf16_valued=False):
    half = x.shape[1] // 2
    if not is_bf16_valued:
        x = x.astype(BF16).astype(F32)
    bits = lax.bitcast_convert_type(x, jnp.uint32)
    return bits[:, :half] | (bits[:, half:] >> 16)


def _unpack_rows(p):
    left = lax.bitcast_convert_type(p & jnp.uint32(HI_MASK), F32).astype(BF16)
    right = lax.bitcast_convert_type(p << 16, F32).astype(BF16)
    return left, right


def _copy_pieces(n, local_row, global_row, copy, max_rows, wait=False):
    for b in range(SEG_ALIGN.bit_length() - 1, max_rows.bit_length()):
        size = 1 << b

        @pl.when((n & size) != 0)
        def _():
            lo = n & (size - 1)
            piece = copy(pl.multiple_of(local_row + lo, SEG_ALIGN), pl.multiple_of(global_row + lo, SEG_ALIGN),
                         size)
            if wait:
                piece.wait()
            else:
                piece.start()


def _segment_copies(base_ref, cnt_ref, copy, max_rows, n_slots, fill_row):
    def per_expert(e, off):
        _copy_pieces(cnt_ref[e], off, base_ref[e], copy, max_rows)
        return off + cnt_ref[e]
    used = lax.fori_loop(0, N_EXPERTS, per_expert, 0)
    _copy_pieces(n_slots - used, used, fill_row, copy, max_rows)


def _tile_slots(tm):
    return tm * TOP_K + N_EXPERTS * SEG_ALIGN


def _dispatch_kernel(base_ref, cnt_ref, tail_ref, tail_len_ref, lpost_ref, h_ref, xs_ref,
                     cbuf_ref, zbuf_ref, sem, zsem, *, spare_row):
    tm = h_ref.shape[0]

    @pl.when(pl.program_id(0) == pl.num_programs(0) - 1)
    def _():
        zbuf_ref[...] = jnp.zeros_like(zbuf_ref)

        def zero_copy(src, dst, size):
            return pltpu.make_async_copy(zbuf_ref.at[pl.ds(src, size)], xs_ref.at[pl.ds(dst, size)], zsem)
        for wait in (False, True):
            def per_expert(e, carry, wait=wait):
                _copy_pieces(tail_len_ref[e], 0, tail_ref[e], zero_copy, zbuf_ref.shape[0], wait=wait)
                return carry
            lax.fori_loop(0, N_EXPERTS, per_expert, 0)

    n_slots = _tile_slots(tm)
    slot = lax.broadcasted_iota(jnp.int32, (n_slots, tm), 0)
    lp = lpost_ref[...]
    sel = slot == lp[0:1, :]
    for k in range(1, TOP_K):
        sel = sel | (slot == lp[k:k + 1, :])
    rows = jnp.dot(sel.astype(BF16), h_ref[...].astype(BF16), preferred_element_type=F32)
    step = pl.program_id(0)
    cur = step % 2
    cbuf_ref[cur] = _pack_rows(rows, is_bf16_valued=True)

    def copy_from(buf):
        def copy(src, dst, size):
            return pltpu.make_async_copy(cbuf_ref.at[buf, pl.ds(src, size)], xs_ref.at[pl.ds(dst, size)],
                                         sem.at[buf])
        return copy
    _segment_copies(base_ref, cnt_ref, copy_from(cur), tm, n_slots, spare_row + cur * tm)

    @pl.when(step > 0)
    def _():
        copy_from(1 - cur)(0, 0, n_slots).wait()

    @pl.when(step == pl.num_programs(0) - 1)
    def _():
        copy_from(cur)(0, 0, n_slots).wait()


def _dispatch(h2, base, cnt, tail, tail_len, lpost, n_rows, tm, tb):
    t, d = h2.shape
    seg = pl.BlockSpec((LANES,), lambda i: (i,), memory_space=pltpu.SMEM)
    whole = pl.BlockSpec(memory_space=pltpu.SMEM)
    return pl.pallas_call(
        functools.partial(_dispatch_kernel, spare_row=n_rows),
        grid=(t // tm,),
        in_specs=[seg, seg, whole, whole,
                  pl.BlockSpec((8, tm), lambda i: (0, i)),
                  pl.BlockSpec((tm, d), lambda i: (i, 0))],
        out_specs=pl.BlockSpec(memory_space=pl.ANY),
        out_shape=jax.ShapeDtypeStruct((n_rows + 2 * tm, d // 2), jnp.uint32),
        scratch_shapes=[pltpu.VMEM((2, _tile_slots(tm), d // 2), jnp.uint32),
                        pltpu.VMEM((tb, d // 2), jnp.uint32),
                        pltpu.SemaphoreType.DMA((2,)), pltpu.SemaphoreType.DMA(())],
        compiler_params=_cparams(("arbitrary",)),
        name="moe_dispatch",
    )(base, cnt, tail, tail_len, lpost, h2)


def _ffn_kernel(be_ref, nu_ref, x_ref, wgu_ref, bgu_ref, wdn_ref, bdn_ref, y_ref, wgu_bf_ref, wdn_bf_ref):
    de = wdn_ref.shape[2]
    i = pl.program_id(0)
    used = i < nu_ref[0]

    @pl.when(used & ((i == 0) | (be_ref[i] != be_ref[jnp.maximum(i - 1, 0)])))
    def _():
        wgu_bf_ref[...] = wgu_ref[0, 0].astype(BF16)
        wdn_bf_ref[...] = wdn_ref[0, 0].astype(BF16)

    @pl.when(used)
    def _():
        half = x_ref.shape[1]
        x_left, x_right = _unpack_rows(x_ref[...])
        gu = (jnp.dot(x_left, wgu_bf_ref[:half, :], preferred_element_type=F32)
              + jnp.dot(x_right, wgu_bf_ref[half:, :], preferred_element_type=F32) + bgu_ref[0, 0])
        gate = jnp.minimum(gu[:, :de], SWIGLU_LIMIT)
        up = jnp.clip(gu[:, de:], -SWIGLU_LIMIT, SWIGLU_LIMIT)
        act = (up + 1.0) * gate * jax.nn.sigmoid(SWIGLU_ALPHA * gate)
        y = jnp.dot(act.astype(BF16), wdn_bf_ref[...], preferred_element_type=F32) + bdn_ref[0, 0]
        y_ref[...] = _pack_rows(y)

    @pl.when(jnp.logical_not(used))
    def _():
        y_ref[...] = jnp.zeros_like(y_ref)


def _ffn(xs, n_rows, block_expert, n_used, layer, wgu, bgu, wdn, bdn, tb):
    p, half = n_rows, xs.shape[1]
    d = 2 * half
    nl, e, _, n2 = wgu.shape
    de = wdn.shape[2]
    grid_spec = pltpu.PrefetchScalarGridSpec(
        num_scalar_prefetch=2,
        grid=(p // tb,),
        in_specs=[pl.BlockSpec((tb, half), lambda i, be, nu: (i, 0)),
                  pl.BlockSpec((1, 1, d, n2), lambda i, be, nu: (layer, be[i], 0, 0)),
                  pl.BlockSpec((1, 1, 1, n2), lambda i, be, nu: (layer, be[i], 0, 0)),
                  pl.BlockSpec((1, 1, de, d), lambda i, be, nu: (layer, be[i], 0, 0)),
                  pl.BlockSpec((1, 1, 1, d), lambda i, be, nu: (layer, be[i], 0, 0))],
        out_specs=pl.BlockSpec((tb, half), lambda i, be, nu: (i, 0)),
        scratch_shapes=[pltpu.VMEM((d, n2), BF16), pltpu.VMEM((de, d), BF16)],
    )
    return pl.pallas_call(
        _ffn_kernel,
        grid_spec=grid_spec,
        out_shape=jax.ShapeDtypeStruct((p, half), jnp.uint32),
        compiler_params=_cparams(("arbitrary",)),
        name="moe_ffn",
    )(block_expert, n_used, xs, wgu, bgu.reshape(nl, e, 1, n2), wdn, bdn.reshape(nl, e, 1, d))


def _combine_kernel(base_ref, cnt_ref, base_next_ref, cnt_next_ref, lpos_ref, gate_ref, x_ref, g2_ref,
                    lng_ref, lnb_ref, ys_ref, o_ref, gbuf_ref, sem, *, alpha):
    tm = x_ref.shape[1]
    n_slots = _tile_slots(tm)
    step = pl.program_id(0) * pl.num_programs(1) + pl.program_id(1)
    n_steps = pl.num_programs(0) * pl.num_programs(1)
    cur = step % 2

    def gather(bases, cnts, buf):
        def copy(dst, src, size):
            return pltpu.make_async_copy(ys_ref.at[pl.ds(src, size)], gbuf_ref.at[buf, pl.ds(dst, size)],
                                         sem.at[buf])
        _segment_copies(bases, cnts, copy, tm, n_slots, 0)
        return copy

    @pl.when(step == 0)
    def _():
        gather(base_ref, cnt_ref, cur)

    @pl.when(step + 1 < n_steps)
    def _():
        gather(base_next_ref, cnt_next_ref, 1 - cur)

    def copy(dst, src, size):
        return pltpu.make_async_copy(ys_ref.at[pl.ds(src, size)], gbuf_ref.at[cur, pl.ds(dst, size)],
                                     sem.at[cur])

    lane = lax.broadcasted_iota(jnp.int32, (tm, n_slots), 1)
    lpos = lpos_ref[...]
    gate = gate_ref[...]
    w = jnp.zeros((tm, n_slots), F32)
    for k in range(TOP_K):
        w = jnp.where(lane == lpos[:, k:k + 1], gate[:, k:k + 1], w)
    w = w.astype(BF16)

    copy(0, 0, n_slots).wait()
    y_left, y_right = _unpack_rows(gbuf_ref[cur])
    f = jnp.concatenate([jnp.dot(w, y, preferred_element_type=F32) for y in (y_left, y_right)], axis=1)
    o_ref[0] = _layer_norm(alpha * x_ref[0] + g2_ref[0] * f, lng_ref[...], lnb_ref[...])


def _combine(base, cnt, lpos, gates, x, g2, lng, lnb, ys, n_lat_tiles, n_tiles, tm, alpha):
    b, _, d = x.shape
    nb = g2.shape[0] - 1

    def mod_map(i, j):
        return (jnp.where(j < n_lat_tiles, i, nb), 0, 0)

    tok = pl.BlockSpec((1, tm, d), lambda i, j: (i, j, 0))
    vec = pl.BlockSpec((1, d), lambda i, j: (0, 0))
    last = b * n_tiles - 1
    seg = pl.BlockSpec((LANES,), lambda i, j: (i * n_tiles + j,), memory_space=pltpu.SMEM)
    seg_next = pl.BlockSpec((LANES,), lambda i, j: (jnp.minimum(i * n_tiles + j + 1, last),),
                            memory_space=pltpu.SMEM)
    per_tok = pl.BlockSpec((tm, LANES), lambda i, j: (i * n_tiles + j, 0))
    return pl.pallas_call(
        functools.partial(_combine_kernel, alpha=alpha),
        grid=(b, n_tiles),
        in_specs=[seg, seg, seg_next, seg_next, per_tok, per_tok,
                  tok, pl.BlockSpec((1, 1, d), mod_map), vec, vec,
                  pl.BlockSpec(memory_space=pl.ANY)],
        out_specs=tok,
        out_shape=jax.ShapeDtypeStruct((b, n_tiles * tm, d), F32),
        scratch_shapes=[pltpu.VMEM((2, _tile_slots(tm), d // 2), jnp.uint32), pltpu.SemaphoreType.DMA((2,))],
        compiler_params=_cparams(("arbitrary", "arbitrary")),
        name="moe_combine_ln",
    )(base, cnt, base, cnt, lpos, gates, x, g2, lng, lnb, ys)


def _moe_layer(h2, logits, x_res, g2, lng, lnb, layer, wgu, bgu, wdn, bdn, n_lat_tiles, n_tiles, tm, alpha):
    b, rows, d = h2.shape
    t = b * rows
    tb = FFN_BLOCK
    gates, lpos, lpost, tiles, counts = _router(logits.reshape(t, LANES), tm)
    counts = counts[0, :N_EXPERTS].astype(jnp.int32)
    padded = (counts + tb - 1) // tb * tb
    pend = jnp.cumsum(padded)
    pstart = pend - padded
    tiles = tiles.reshape(t // tm, 8, LANES)[:, :2, :].astype(jnp.int32)
    seg_base = (jnp.pad(pstart, (0, LANES - N_EXPERTS))[None, :] + tiles[:, 0]).reshape(-1)
    seg_cnt = tiles[:, 1].reshape(-1)
    n_rows = t * TOP_K + (t // tm) * N_EXPERTS * (SEG_ALIGN - 1)
    n_rows = (n_rows + tb - 1) // tb * tb + N_EXPERTS * tb
    n_blocks = n_rows // tb
    block_start = jnp.arange(n_blocks, dtype=jnp.int32) * tb
    block_expert = jnp.minimum(jnp.sum((pend[None, :] <= block_start[:, None]).astype(jnp.int32), axis=1),
                               N_EXPERTS - 1).astype(jnp.int32)
    n_used = (pend[-1:] // tb).astype(jnp.int32)
    tail = jnp.pad(pstart + counts, (0, LANES - N_EXPERTS))
    tail_len = jnp.pad(padded - counts, (0, LANES - N_EXPERTS))
    xs = _dispatch(h2.reshape(t, d), seg_base, seg_cnt, tail, tail_len, lpost, n_rows, tm, tb)
    ys = _ffn(xs, n_rows, block_expert, n_used, layer, wgu, bgu, wdn, bdn, tb)
    return _combine(seg_base, seg_cnt, lpos, gates, x_res, g2, lng, lnb, ys, n_lat_tiles, n_tiles, tm, alpha)


def _rope_tables(n_lat_rows, n_rows, scale):
    pos = jnp.arange(n_lat_rows)
    lane = jnp.arange(LANES)
    m = lane % 64
    n_freq = 16
    inv = ROPE_BASE ** (-(m % n_freq).astype(F32) / n_freq)
    p = jnp.where((m // 32)[None, :] == 0, (pos // GRID_W)[:, None], (pos % GRID_W)[:, None]).astype(F32)
    ang = p * inv[None, :]
    sign = jnp.where((m % 32) < n_freq, -1.0, 1.0)[None, :]
    cos = jnp.concatenate([jnp.cos(ang), jnp.ones((n_rows - n_lat_rows, LANES), F32)], axis=0)
    sin = jnp.concatenate([jnp.sin(ang) * sign, jnp.zeros((n_rows - n_lat_rows, LANES), F32)], axis=0)
    return cos * scale, sin * scale


def kernel(x, c, ctx, c_ctx, w_ada, b_ada, ln_g, ln_b, hg_w_in, hg_lb, hg_norm_g, hg_w_out, da_w_in, da_lam,
           da_norm_g, da_w_out, moe_w_router, moe_b_router, moe_w_gu, moe_b_gu, moe_w_dn, moe_b_dn):
    bsz, seq, d = x.shape
    lc = ctx.shape[1]
    depth = w_ada.shape[0]
    assert depth == 2 and hg_w_in.shape[0] == 1 and da_w_in.shape[0] == 1
    ltot = seq + lc
    tm = TOKEN_TILE
    assert seq % tm == 0 and lc % tm == 0 and seq % GRID_W == 0 and d == DA_HEADS * LANES
    alpha = (2 * depth) ** 0.25
    n_lat_tiles = seq // tm
    n_all_tiles = ltot // tm
    h_heads = d // LANES

    n_mod = bsz + 1
    c_pad = jnp.zeros((16, d), F32).at[:bsz].set(c).at[bsz].set(c_ctx)
    mod = _ada(c_pad, w_ada, b_ada)[:, :n_mod]

    def mods(l):
        return [mod[l, :, k * d:(k + 1) * d].reshape(n_mod, 1, d) for k in range(6)]

    def router_params(l):
        wr = jnp.zeros((d, LANES), F32).at[:, :N_EXPERTS].set(moe_w_router[l])
        wr_hi = wr.astype(BF16)
        wr_lo = (wr - wr_hi.astype(F32)).astype(BF16)
        br = jnp.full((1, LANES), -1e30, F32).at[0, :N_EXPERTS].set(moe_b_router[l])
        return jnp.stack([wr_hi, wr_lo]), br

    xall = (x, ctx)

    sh1, sc1, g1, sh2, sc2, g2 = mods(0)
    p0 = _proj(xall, sc1, sh1, hg_w_in[0].astype(BF16), n_lat_tiles, tm)
    lb = jnp.cumsum(jax.nn.softmax(hg_lb.astype(F32), axis=1), axis=1)[:, 0]
    lb = lb.reshape(2, h_heads, LANES).transpose(1, 0, 2)
    y0 = _gla(p0, lb, hg_norm_g[0].reshape(h_heads, 1, LANES), seq)
    wr, br = router_params(0)
    x0, h20, lg0 = _post(y0, hg_w_out[0].astype(BF16), xall, g1, ln_g[0, 0].reshape(1, d),
                         ln_b[0, 0].reshape(1, d), sc2, sh2, wr, br, n_lat_tiles, n_all_tiles, tm, alpha)
    x1 = _moe_layer(h20, lg0, x0, g2, ln_g[0, 1].reshape(1, d), ln_b[0, 1].reshape(1, d),
                    0, moe_w_gu, moe_b_gu, moe_w_dn, moe_b_dn, n_lat_tiles, n_all_tiles, tm, alpha)

    sh1, sc1, g1, sh2, sc2, g2 = mods(1)
    dh = d // DA_HEADS // 2
    cq, sq = _rope_tables(seq, ltot, dh ** -0.5 * math.log2(math.e))
    ck, sk = _rope_tables(seq, ltot, 1.0)
    p1 = _proj(x1, sc1, sh1, da_w_in[0].astype(BF16), n_lat_tiles, tm, rope_tabs=(cq, sq, ck, sk))
    lam_init = 0.8 - 0.6 * math.exp(-0.3 * 1)
    lp = da_lam[0].astype(F32)
    lam = (jnp.exp(jnp.sum(lp[0] * lp[1])) - jnp.exp(jnp.sum(lp[2] * lp[3])) + lam_init).reshape(1)
    y1 = _attn(p1, lam, da_norm_g[0].reshape(LANES, 1), seq, 1.0 - lam_init)
    wr, br = router_params(1)
    x2, h21, lg1 = _post(y1, da_w_out[0].astype(BF16), x1, g1, ln_g[1, 0].reshape(1, d),
                         ln_b[1, 0].reshape(1, d), sc2, sh2, wr, br, n_lat_tiles, n_lat_tiles, tm, alpha)
    return _moe_layer(h21, lg1, x2, g2, ln_g[1, 1].reshape(1, d), ln_b[1, 1].reshape(1, d),
                      1, moe_w_gu, moe_b_gu, moe_w_dn, moe_b_dn, n_lat_tiles, n_lat_tiles, tm, alpha)
```

```python
import functools
import math

import jax
import jax.numpy as jnp
from jax import lax
from jax.experimental import pallas as pl
from jax.experimental.pallas import tpu as pltpu

F32 = jnp.float32
BF16 = jnp.bfloat16
HIGHEST = lax.Precision.HIGHEST

LANES = 128
HG_CHUNK = 64
GRID_W = 64
ROPE_BASE = 10000.0
DA_HEADS = 8
N_EXPERTS = 32
TOP_K = 4
SEG_ALIGN = 8
SWIGLU_LIMIT = 7.0
SWIGLU_ALPHA = 1.702
LN_EPS = 1e-5
RMS_EPS = 1e-6
VMEM_LIMIT = 56 * 1024 * 1024
MXU_COLS = 256

TOKEN_TILE = 256
FFN_BLOCK = 512
ADA_TILE_N = 1536
PROJ_CHUNK = 512
ATTN_TQ = 1024
ATTN_TK = 768
READOUT_ROWS = 256


def _cparams(sem):
    return pltpu.CompilerParams(dimension_semantics=sem, vmem_limit_bytes=VMEM_LIMIT)


def _ada_kernel(c_ref, w_ref, b_ref, o_ref):
    c = c_ref[...]
    s = c * jax.nn.sigmoid(c)
    o_ref[0] = jnp.dot(s, w_ref[0], precision=HIGHEST, preferred_element_type=F32) + b_ref[0]


def _ada(c_pad, w_ada, b_ada):
    depth, d, n = w_ada.shape
    rows = c_pad.shape[0]
    tn = ADA_TILE_N
    return pl.pallas_call(
        _ada_kernel,
        grid=(depth, n // tn),
        in_specs=[pl.BlockSpec((rows, d), lambda l, j: (0, 0)),
                  pl.BlockSpec((1, d, tn), lambda l, j: (l, 0, j)),
                  pl.BlockSpec((1, 1, tn), lambda l, j: (l, 0, j))],
        out_specs=pl.BlockSpec((1, rows, tn), lambda l, j: (l, 0, j)),
        out_shape=jax.ShapeDtypeStruct((depth, rows, n), F32),
        compiler_params=_cparams(("arbitrary", "arbitrary")),
        name="ada_mod",
    )(c_pad, w_ada, b_ada.reshape(depth, 1, n))


def _rope(piece, cos, sin, lo_mask):
    rot = jnp.where(lo_mask, pltpu.roll(piece, LANES - 16, 1), pltpu.roll(piece, 16, 1))
    return piece * cos + rot * sin


def _token_tile(x_ref, xc_ref, n_lat_tiles):
    if xc_ref is None:
        return x_ref[0]
    return jnp.where(pl.program_id(1) < n_lat_tiles, x_ref[0], xc_ref[0])


def _stream_specs(x, tm, n_lat_tiles):
    if not isinstance(x, tuple):
        return [pl.BlockSpec((1, tm, x.shape[2]), lambda i, j: (i, j, 0))], [x]
    d = x[0].shape[2]
    return ([pl.BlockSpec((1, tm, d), lambda i, j: (i, jnp.minimum(j, n_lat_tiles - 1), 0)),
             pl.BlockSpec((1, tm, d), lambda i, j: (i, jnp.maximum(j - n_lat_tiles, 0), 0))], list(x))


def _proj_kernel(*refs, n_rope, cw, two_streams, n_lat_tiles):
    x_ref, refs = refs[0], refs[1:]
    xc_ref = None
    if two_streams:
        xc_ref, refs = refs[0], refs[1:]
    sc_ref, sh_ref, w_ref, *rest = refs
    if n_rope:
        cq_ref, sq_ref, ck_ref, sk_ref, o_ref = rest
    else:
        (o_ref,) = rest
    n = w_ref.shape[1]
    h = (_token_tile(x_ref, xc_ref, n_lat_tiles) * (1.0 + sc_ref[0]) + sh_ref[0]).astype(BF16)
    if n_rope:
        lane = lax.broadcasted_iota(jnp.int32, (x_ref.shape[1], LANES), 1)
        lo_mask = (lane % 32) < 16
    per = cw // LANES
    for j in range(n // cw):
        r = jnp.dot(h, w_ref[:, j * cw:(j + 1) * cw], preferred_element_type=F32)
        for g in range(per):
            hd = j * per + g
            piece = r[:, g * LANES:(g + 1) * LANES]
            if hd < n_rope:
                piece = _rope(piece, cq_ref[...], sq_ref[...], lo_mask)
            elif hd < 2 * n_rope:
                piece = _rope(piece, ck_ref[...], sk_ref[...], lo_mask)
            o_ref[0, hd] = piece.astype(BF16)


def _proj(x, sc, sh, w, n_lat_tiles, tm, rope_tabs=None):
    two_streams = isinstance(x, tuple)
    b, _, d = (x[0] if two_streams else x).shape
    ltot = x[0].shape[1] + x[1].shape[1] if two_streams else x.shape[1]
    n = w.shape[1]
    nb = sc.shape[0] - 1
    n_rope = DA_HEADS if rope_tabs is not None else 0

    def mod_map(i, j):
        return (jnp.where(j < n_lat_tiles, i, nb), 0, 0)

    in_specs, args = _stream_specs(x, tm, n_lat_tiles)
    in_specs += [pl.BlockSpec((1, 1, d), mod_map),
                 pl.BlockSpec((1, 1, d), mod_map),
                 pl.BlockSpec((d, n), lambda i, j: (0, 0))]
    args += [sc, sh, w]
    if rope_tabs is not None:
        in_specs += [pl.BlockSpec((tm, LANES), lambda i, j: (j, 0))] * 4
        args += list(rope_tabs)
    return pl.pallas_call(
        functools.partial(_proj_kernel, n_rope=n_rope, cw=PROJ_CHUNK, two_streams=two_streams,
                          n_lat_tiles=n_lat_tiles),
        grid=(b, ltot // tm),
        in_specs=in_specs,
        out_specs=pl.BlockSpec((1, n // LANES, tm, LANES), lambda i, j: (i, 0, j, 0)),
        out_shape=jax.ShapeDtypeStruct((b, n // LANES, ltot, LANES), BF16),
        compiler_params=_cparams(("arbitrary", "arbitrary")),
        name="mod_proj_rope" if n_rope else "mod_proj",
    )(*args)


GLA_BLOCK = MXU_COLS


def _gla_local(items):
    nc = GLA_BLOCK // HG_CHUNK
    kks, cats = [], []
    for (_, _, z, lb, _, _, _) in items:
        f = lb + (1.0 - lb) * jax.nn.sigmoid(z)
        logf = jnp.log(f)
        kks.append(1.0 - f)
        hi = logf.astype(BF16)
        lo = (logf - hi.astype(F32)).astype(BF16)
        cats.append(jnp.concatenate([hi, lo], axis=1))
    parts = [jnp.dot(it[4], cat, preferred_element_type=F32) for it, cat in zip(items, cats)]
    q_decs, k_invs, k_ends, decs = [], [], [], []
    for (q, _, _, _, _, _, end_row), kk, part in zip(items, kks, parts):
        bcum = part[:, :LANES] + part[:, LANES:]
        b_end = bcum.reshape(nc, HG_CHUNK, LANES)[:, end_row:end_row + 1, :]
        b_end_rows = jnp.broadcast_to(b_end, (nc, HG_CHUNK, LANES)).reshape(GLA_BLOCK, LANES)
        q_decs.append((q * jnp.exp(bcum)).astype(BF16))
        k_invs.append((kk * jnp.exp(-bcum)).astype(BF16))
        k_ends.append((kk * jnp.exp(b_end_rows - bcum)).astype(BF16))
        decs.append(jnp.exp(b_end.reshape(nc, LANES)))
    scores = [lax.dot_general(qd, ki, (((1,), (1,)), ((), ())), preferred_element_type=F32)
              for qd, ki in zip(q_decs, k_invs)]
    scores = [jnp.where(it[5], a, 0.0).astype(BF16) for it, a in zip(items, scores)]
    outs = [jnp.dot(a, it[1], preferred_element_type=F32) for it, a in zip(items, scores)]
    return list(zip(outs, q_decs, k_ends, decs))


def _gla_kernel(q_ref, i_ref, g_ref, zf_ref, zb_ref, lb_ref, ng_ref, y_ref,
                of_ref, ob_ref, qd_ref, ke_ref, de_ref, st_ref, *, n_lat, n_ctx):
    c = HG_CHUNK
    blk = GLA_BLOCK
    nc = blk // c
    ltot = (n_lat + n_ctx) * c
    row = lax.broadcasted_iota(jnp.int32, (blk, blk), 0)
    col = lax.broadcasted_iota(jnp.int32, (blk, blk), 1)
    same = (row // c) == (col // c)
    masks = (same & (col <= row), same & (col >= row))
    tris = (masks[0].astype(BF16), masks[1].astype(BF16))
    z_refs = (zf_ref, zb_ref)
    o_refs = (of_ref, ob_ref)
    end_rows = (c - 1, 0)

    n_blk = ltot // blk
    per_step = 3 if n_blk % 3 == 0 else 1

    def local(t, carry):
        work = []
        for u in range(per_step):
            r0 = pl.multiple_of((t * per_step + u) * blk, blk)
            c0 = pl.multiple_of((t * per_step + u) * nc, nc)
            q = q_ref[0, 0, pl.ds(r0, blk), :].astype(F32)
            v = i_ref[0, 0, pl.ds(r0, blk), :]
            zs = [z_refs[d][0, 0, pl.ds(r0, blk), :].astype(F32) for d in range(2)]
            work.append((r0, c0, q, v, zs))
        res = _gla_local([(q, v, zs[d], lb_ref[0, d:d + 1, :], tris[d], masks[d], end_rows[d])
                          for (_, _, q, v, zs) in work for d in range(2)])
        for u, (r0, c0, _, _, _) in enumerate(work):
            for d in range(2):
                o, q_dec, k_end, dec = res[2 * u + d]
                o_refs[d][pl.ds(r0, blk), :] = o
                qd_ref[d, pl.ds(r0, blk), :] = q_dec
                ke_ref[d, pl.ds(r0, blk), :] = k_end
                de_ref[d, pl.ds(c0, nc), :] = dec
        return carry
    lax.fori_loop(0, n_blk // per_step, local, 0)

    st_ref[...] = jnp.zeros_like(st_ref)

    def segment(first, n):
        per = 8 if n % 8 == 0 else (4 if n % 4 == 0 else 1)

        def body(jj, carry):
            todo = []
            for u in range(per):
                j = jj * per + u
                for d, ch in enumerate((first + j, first + n - 1 - j)):
                    r0 = pl.multiple_of(ch * c, c)
                    kv = lax.dot_general(i_ref[0, 0, pl.ds(r0, c), :], ke_ref[d, pl.ds(r0, c), :],
                                         (((0,), (0,)), ((), ())), preferred_element_type=F32)
                    todo.append((d, r0, kv, de_ref[d, pl.ds(ch, 1), :], qd_ref[d, pl.ds(r0, c), :]))
            st = [st_ref[0], st_ref[1]]
            inter = []
            for (d, r0, kv, dec, q_dec) in todo:
                inter.append((d, r0, lax.dot_general(q_dec, st[d].astype(BF16), (((1,), (1,)), ((), ())),
                                                     preferred_element_type=F32)))
                st[d] = st[d] * dec + kv
            for (d, r0, o) in inter:
                o_refs[d][pl.ds(r0, c), :] += o
            st_ref[0] = st[0]
            st_ref[1] = st[1]
            return carry
        lax.fori_loop(0, n // per, body, 0)

    segment(n_lat, n_ctx)
    segment(0, n_lat)

    ng = ng_ref[0]
    rt = READOUT_ROWS

    def readout(t, carry):
        r0 = pl.multiple_of(t * rt, rt)
        o = of_ref[pl.ds(r0, rt), :] + ob_ref[pl.ds(r0, rt), :]
        ms = jnp.mean(o * o, axis=-1, keepdims=True)
        g = g_ref[0, 0, pl.ds(r0, rt), :].astype(F32)
        y = o * lax.rsqrt(ms + RMS_EPS) * ng * (g * jax.nn.sigmoid(g))
        y_ref[0, 0, pl.ds(r0, rt), :] = y.astype(BF16)
        return carry
    lax.fori_loop(0, ltot // rt, readout, 0, unroll=3 if (ltot // rt) % 3 == 0 else 1)


def _gla(p, lb, ng, n_lat_rows):
    b, nh5, ltot, _ = p.shape
    h = nh5 // 5
    n_lat = n_lat_rows // HG_CHUNK
    n_ctx = (ltot - n_lat_rows) // HG_CHUNK

    def spec(k):
        return pl.BlockSpec((1, 1, ltot, LANES), lambda i, j, k=k: (i, k * h + j, 0, 0))

    return pl.pallas_call(
        functools.partial(_gla_kernel, n_lat=n_lat, n_ctx=n_ctx),
        grid=(b, h),
        in_specs=[spec(0), spec(1), spec(2), spec(3), spec(4),
                  pl.BlockSpec((1, 2, LANES), lambda i, j: (j, 0, 0)),
                  pl.BlockSpec((1, 1, LANES), lambda i, j: (j, 0, 0))],
        out_specs=pl.BlockSpec((1, 1, ltot, LANES), lambda i, j: (i, j, 0, 0)),
        out_shape=jax.ShapeDtypeStruct((b, h, ltot, LANES), BF16),
        scratch_shapes=[pltpu.VMEM((ltot, LANES), F32), pltpu.VMEM((ltot, LANES), F32),
                        pltpu.VMEM((2, ltot, LANES), BF16), pltpu.VMEM((2, ltot, LANES), BF16),
                        pltpu.VMEM((2, ltot // HG_CHUNK, LANES), F32),
                        pltpu.VMEM((2, LANES, LANES), F32)],
        compiler_params=_cparams(("arbitrary", "arbitrary")),
        name="hgrn2_gla",
    )(p, p, p, p, p, lb, ng)


SUM_ROWS = 16


BOUND_SLACK = 1.0 + 2.0 ** -5
MIN_SOFTMAX_SUM = 1e-30


def _attn_kernel(lam_ref, q_ref, k_ref, v_ref, ng_ref, y_ref, vt_ref, kmax_ref, acc_ref, *, tk, out_scale):
    tq = q_ref.shape[2]
    lk = k_ref.shape[2]
    lam = lam_ref[0]
    n_chunks = lk // tk
    srow = lax.broadcasted_iota(jnp.int32, (8, LANES), 0)
    slane = lax.broadcasted_iota(jnp.int32, (8, LANES), 1)
    sel = (srow == slane // 64).astype(BF16)

    def sq_norms(x):
        return lax.dot_general(sel, x * x, (((1,), (1,)), ((), ())), preferred_element_type=F32)

    @pl.when(pl.program_id(2) == 0)
    def _():
        blk = MXU_COLS
        kn = jnp.zeros((8, blk), F32)
        for cb in range(lk // blk):
            vt_ref[0:LANES, cb * blk:(cb + 1) * blk] = jnp.transpose(
                v_ref[0, 0, cb * blk:(cb + 1) * blk, :].astype(F32)).astype(BF16)
            kn = jnp.maximum(kn, sq_norms(k_ref[0, 0, cb * blk:(cb + 1) * blk, :]))
        vt_ref[LANES:LANES + SUM_ROWS, :] = jnp.ones((SUM_ROWS, lk), BF16)
        kmax_ref[...] = jnp.broadcast_to(jnp.sqrt(jnp.max(kn, axis=1, keepdims=True)), kmax_ref.shape)

    q = q_ref[0, 0]
    lane = lax.broadcasted_iota(jnp.int32, (tq, LANES), 1)
    zero = jnp.zeros_like(q)
    qm = (jnp.where(lane < 64, q, zero), jnp.where(lane >= 64, q, zero))

    def scores(ci, c):
        kc = k_ref[0, 0, ci * tk:(ci + 1) * tk, :]
        return lax.dot_general(kc, qm[c], (((1,), (1,)), ((), ())), preferred_element_type=F32)

    bound = jnp.sqrt(sq_norms(q)) * kmax_ref[...] * BOUND_SLACK
    acc = [jnp.zeros((LANES, tq), F32) for _ in range(2)]
    part = [jnp.zeros((8, tq), F32) for _ in range(2)]
    for ci in range(n_chunks):
        vt = vt_ref[0:LANES, ci * tk:(ci + 1) * tk]
        for c in range(2):
            pf = jnp.exp2(scores(ci, c) - bound[c:c + 1])
            part[c] = part[c] + jnp.sum(pf.reshape(tk // 8, 8, tq), axis=0)
            acc[c] = acc[c] + jnp.dot(vt, pf.astype(BF16), preferred_element_type=F32)
    sums = []
    for c in range(2):
        total = jnp.sum(part[c], axis=0, keepdims=True)
        acc_ref[c, 0:LANES, :] = acc[c]
        acc_ref[c, LANES:LANES + SUM_ROWS, :] = jnp.broadcast_to(total, (SUM_ROWS, tq))
        sums.append(total)
    underflow = jnp.logical_not(jnp.min(jnp.minimum(sums[0], sums[1])) > MIN_SOFTMAX_SUM)

    @pl.when(underflow)
    def _():
        def body(ci, carry):
            r0 = pl.multiple_of(ci * tk, tk)
            kc = k_ref[0, 0, pl.ds(r0, tk), :]
            vt = vt_ref[:, pl.ds(r0, tk)]
            out = []
            for c in range(2):
                m_c, a_c = carry[c]
                s_c = lax.dot_general(kc, qm[c], (((1,), (1,)), ((), ())), preferred_element_type=F32)
                m_new = jnp.maximum(m_c, jnp.max(s_c, axis=0, keepdims=True))
                p = jnp.exp2(s_c - m_new).astype(BF16)
                out.append((m_new, jnp.exp2(m_c - m_new) * a_c + jnp.dot(vt, p, preferred_element_type=F32)))
            return tuple(out)
        init = tuple((jnp.full((1, tq), -1e30, F32), jnp.zeros((LANES + SUM_ROWS, tq), F32))
                     for _ in range(2))
        (_, b0), (_, b1) = lax.fori_loop(0, n_chunks, body, init)
        acc_ref[0] = b0
        acc_ref[1] = b1

    a0 = acc_ref[0]
    a1 = acc_ref[1]
    o = a0[:LANES] / a0[LANES:LANES + 1] - lam * (a1[:LANES] / a1[LANES:LANES + 1])
    ms = jnp.mean(o * o, axis=0, keepdims=True)
    y = o * lax.rsqrt(ms + RMS_EPS) * ng_ref[...] * out_scale
    y_ref[0, 0] = jnp.transpose(y).astype(BF16)


def _attn(p, lam, ng, n_lat_rows, out_scale, tq=ATTN_TQ, tk=ATTN_TK):
    b, nh3, ltot, _ = p.shape
    h = nh3 // 3
    assert ltot % tk == 0 and n_lat_rows % tq == 0 and ltot % MXU_COLS == 0
    return pl.pallas_call(
        functools.partial(_attn_kernel, tk=tk, out_scale=out_scale),
        grid=(b, h, n_lat_rows // tq),
        in_specs=[pl.BlockSpec(memory_space=pltpu.SMEM),
                  pl.BlockSpec((1, 1, tq, LANES), lambda i, j, t: (i, j, t, 0)),
                  pl.BlockSpec((1, 1, ltot, LANES), lambda i, j, t: (i, h + j, 0, 0)),
                  pl.BlockSpec((1, 1, ltot, LANES), lambda i, j, t: (i, 2 * h + j, 0, 0)),
                  pl.BlockSpec((LANES, 1), lambda i, j, t: (0, 0))],
        out_specs=pl.BlockSpec((1, 1, tq, LANES), lambda i, j, t: (i, j, t, 0)),
        out_shape=jax.ShapeDtypeStruct((b, h, n_lat_rows, LANES), BF16),
        scratch_shapes=[pltpu.VMEM((LANES + SUM_ROWS, ltot), BF16), pltpu.VMEM((8, tq), F32),
                        pltpu.VMEM((2, LANES + SUM_ROWS, tq), F32)],
        compiler_params=_cparams(("arbitrary", "arbitrary", "arbitrary")),
        name="diff_attn",
    )(lam, p, p, p, ng)


def _layer_norm(x, g, b):
    mu = jnp.mean(x, axis=-1, keepdims=True)
    xc = x - mu
    var = jnp.mean(xc * xc, axis=-1, keepdims=True)
    return xc * lax.rsqrt(var + LN_EPS) * g + b


def _post_kernel(y_ref, w_ref, *refs, alpha, two_streams, n_lat_tiles):
    x_ref, refs = refs[0], refs[1:]
    xc_ref = None
    if two_streams:
        xc_ref, refs = refs[0], refs[1:]
    g1_ref, lng_ref, lnb_ref, sc2_ref, sh2_ref, wr_ref, br_ref, xo_ref, h2_ref, lg_ref = refs
    nh = y_ref.shape[1]
    y = jnp.concatenate([y_ref[0, k] for k in range(nh)], axis=-1)
    m = jnp.dot(y, w_ref[...], preferred_element_type=F32)
    x_res = _token_tile(x_ref, xc_ref, n_lat_tiles)
    xl = _layer_norm(alpha * x_res + g1_ref[0] * m, lng_ref[...], lnb_ref[...])
    xo_ref[0] = xl
    h2 = xl * (1.0 + sc2_ref[0]) + sh2_ref[0]
    h_hi = h2.astype(BF16)
    h2_ref[0] = h_hi
    h_lo = (h2 - h_hi.astype(F32)).astype(BF16)
    tm = h_hi.shape[0]
    both = jnp.dot(jnp.concatenate([h_hi, h_lo], axis=0), wr_ref[0], preferred_element_type=F32)
    lg_ref[0] = (both[:tm] + both[tm:]
                 + jnp.dot(h_hi, wr_ref[1], preferred_element_type=F32) + br_ref[...])


def _post(y, w_out, x, g1, lng, lnb, sc2, sh2, wr, br, n_lat_tiles, n_tiles, tm, alpha):
    b, nh, _, _ = y.shape
    two_streams = isinstance(x, tuple)
    d = (x[0] if two_streams else x).shape[2]
    nb = g1.shape[0] - 1
    rows = n_tiles * tm

    def mod_map(i, j):
        return (jnp.where(j < n_lat_tiles, i, nb), 0, 0)

    tok = pl.BlockSpec((1, tm, d), lambda i, j: (i, j, 0))
    vec = pl.BlockSpec((1, d), lambda i, j: (0, 0))
    x_specs, x_args = _stream_specs(x, tm, n_lat_tiles)
    return pl.pallas_call(
        functools.partial(_post_kernel, alpha=alpha, two_streams=two_streams, n_lat_tiles=n_lat_tiles),
        grid=(b, n_tiles),
        in_specs=[pl.BlockSpec((1, nh, tm, LANES), lambda i, j: (i, 0, j, 0)),
                  pl.BlockSpec((d, d), lambda i, j: (0, 0)),
                  *x_specs, pl.BlockSpec((1, 1, d), mod_map), vec, vec,
                  pl.BlockSpec((1, 1, d), mod_map), pl.BlockSpec((1, 1, d), mod_map),
                  pl.BlockSpec((2, d, LANES), lambda i, j: (0, 0, 0)),
                  pl.BlockSpec((1, LANES), lambda i, j: (0, 0))],
        out_specs=[tok, tok, pl.BlockSpec((1, tm, LANES), lambda i, j: (i, j, 0))],
        out_shape=[jax.ShapeDtypeStruct((b, rows, d), F32),
                   jax.ShapeDtypeStruct((b, rows, d), BF16),
                   jax.ShapeDtypeStruct((b, rows, LANES), F32)],
        compiler_params=_cparams(("arbitrary", "arbitrary")),
        name="out_proj_ln",
    )(y, w_out, *x_args, g1, lng, lnb, sc2, sh2, wr, br)


def _router_kernel(lg_ref, gate_ref, lpos_ref, lpost_ref, tile_ref, cnt_ref, carry_ref):
    tm = lg_ref.shape[0]
    ne = N_EXPERTS

    @pl.when(pl.program_id(0) == 0)
    def _():
        carry_ref[...] = jnp.zeros_like(carry_ref)

    work = jnp.transpose(lg_ref[...])[0:ne, :]
    row_f = lax.broadcasted_iota(jnp.int32, (ne, tm), 0).astype(F32)
    vals, onehots = [], []
    for k in range(TOP_K):
        mx = jnp.max(work, axis=0, keepdims=True)
        am = jnp.min(jnp.where(work == mx, row_f, float(ne)), axis=0, keepdims=True)
        oh = row_f == am
        vals.append(mx)
        onehots.append(oh)
        work = jnp.where(oh, -3e38, work)
    es = [jnp.exp(v - vals[0]) for v in vals]
    denom = es[0] + es[1] + es[2] + es[3]
    member = (onehots[0] | onehots[1] | onehots[2] | onehots[3])
    s_idx = lax.broadcasted_iota(jnp.int32, (tm, tm), 0)
    t_idx = lax.broadcasted_iota(jnp.int32, (tm, tm), 1)
    before = jnp.dot(member.astype(BF16), (s_idx < t_idx).astype(BF16), preferred_element_type=F32)
    cnt = jnp.sum(member.astype(F32), axis=1, keepdims=True)
    cnt = jnp.floor((cnt + (SEG_ALIGN - 1)) * (1.0 / SEG_ALIGN)) * SEG_ALIGN
    cnt_b = jnp.broadcast_to(cnt, (ne, LANES))
    ei = lax.broadcasted_iota(jnp.int32, (ne, ne), 0)
    ej = lax.broadcasted_iota(jnp.int32, (ne, ne), 1)
    seg_start = jnp.dot((ej < ei).astype(BF16), cnt_b.astype(BF16), preferred_element_type=F32)[:, 0:1]
    slot = seg_start + before
    sub8 = lax.broadcasted_iota(jnp.int32, (8, tm), 0)
    lp8 = jnp.zeros((8, tm), F32)
    g8 = jnp.zeros((8, tm), F32)
    for k in range(TOP_K):
        lp8 = jnp.where(sub8 == k, jnp.sum(jnp.where(onehots[k], slot, 0.0), axis=0, keepdims=True), lp8)
        g8 = jnp.where(sub8 == k, es[k] / denom, g8)
    lpost_ref[...] = lp8.astype(jnp.int32)
    pad = jnp.zeros((LANES - 8, tm), F32)
    lpos_ref[...] = jnp.transpose(jnp.concatenate([lp8, pad], axis=0)).astype(jnp.int32)
    gate_ref[...] = jnp.transpose(jnp.concatenate([g8, pad], axis=0))
    cnt_lane = jnp.transpose(jnp.concatenate([cnt_b, jnp.zeros((LANES - ne, LANES), F32)], axis=0))[0:1, :]
    row8 = lax.broadcasted_iota(jnp.int32, (8, LANES), 0)
    tile_ref[...] = jnp.where(row8 == 0, carry_ref[...], jnp.where(row8 == 1, cnt_lane, 0.0))
    carry_ref[...] = carry_ref[...] + cnt_lane
    cnt_ref[...] = carry_ref[...]


def _router(logits, tm):
    t = logits.shape[0]
    tile = pl.BlockSpec((tm, LANES), lambda i: (i, 0))
    return pl.pallas_call(
        _router_kernel,
        grid=(t // tm,),
        in_specs=[tile],
        out_specs=[tile, tile, pl.BlockSpec((8, tm), lambda i: (0, i)),
                   pl.BlockSpec((8, LANES), lambda i: (i, 0)), pl.BlockSpec((1, LANES), lambda i: (0, 0))],
        out_shape=[jax.ShapeDtypeStruct((t, LANES), F32),
                   jax.ShapeDtypeStruct((t, LANES), jnp.int32),
                   jax.ShapeDtypeStruct((8, t), jnp.int32),
                   jax.ShapeDtypeStruct((t // tm * 8, LANES), F32),
                   jax.ShapeDtypeStruct((1, LANES), F32)],
        scratch_shapes=[pltpu.VMEM((1, LANES), F32)],
        compiler_params=_cparams(("arbitrary",)),
        name="router_topk",
    )(logits)


HI_MASK = 0xFFFF0000


def _pack_rows(x, is_bf16_valued=False):
    half = x.shape[1] // 2
    if not is_bf16_valued:
        x = x.astype(BF16).astype(F32)
    bits = lax.bitcast_convert_type(x, jnp.uint32)
    return bits[:, :half] | (bits[:, half:] >> 16)


def _unpack_rows(p):
    left = lax.bitcast_convert_type(p & jnp.uint32(HI_MASK), F32).astype(BF16)
    right = lax.bitcast_convert_type(p << 16, F32).astype(BF16)
    return left, right


def _copy_pieces(n, local_row, global_row, copy, max_rows, wait=False):
    for b in range(SEG_ALIGN.bit_length() - 1, max_rows.bit_length()):
        size = 1 << b

        @pl.when((n & size) != 0)
        def _():
            lo = n & (size - 1)
            piece = copy(pl.multiple_of(local_row + lo, SEG_ALIGN), pl.multiple_of(global_row + lo, SEG_ALIGN),
                         size)
            if wait:
                piece.wait()
            else:
                piece.start()


def _segment_copies(base_ref, cnt_ref, copy, max_rows, n_slots, fill_row):
    def per_expert(e, off):
        _copy_pieces(cnt_ref[e], off, base_ref[e], copy, max_rows)
        return off + cnt_ref[e]
    used = lax.fori_loop(0, N_EXPERTS, per_expert, 0)
    _copy_pieces(n_slots - used, used, fill_row, copy, max_rows)


def _tile_slots(tm):
    return tm * TOP_K + N_EXPERTS * SEG_ALIGN


def _dispatch_kernel(base_ref, cnt_ref, tail_ref, tail_len_ref, lpost_ref, h_ref, xs_ref,
                     cbuf_ref, zbuf_ref, sem, zsem, *, spare_row):
    tm = h_ref.shape[0]

    @pl.when(pl.program_id(0) == pl.num_programs(0) - 1)
    def _():
        zbuf_ref[...] = jnp.zeros_like(zbuf_ref)

        def zero_copy(src, dst, size):
            return pltpu.make_async_copy(zbuf_ref.at[pl.ds(src, size)], xs_ref.at[pl.ds(dst, size)], zsem)
        for wait in (False, True):
            def per_expert(e, carry, wait=wait):
                _copy_pieces(tail_len_ref[e], 0, tail_ref[e], zero_copy, zbuf_ref.shape[0], wait=wait)
                return carry
            lax.fori_loop(0, N_EXPERTS, per_expert, 0)

    n_slots = _tile_slots(tm)
    slot = lax.broadcasted_iota(jnp.int32, (n_slots, tm), 0)
    lp = lpost_ref[...]
    sel = slot == lp[0:1, :]
    for k in range(1, TOP_K):
        sel = sel | (slot == lp[k:k + 1, :])
    rows = jnp.dot(sel.astype(BF16), h_ref[...].astype(BF16), preferred_element_type=F32)
    step = pl.program_id(0)
    cur = step % 2
    cbuf_ref[cur] = _pack_rows(rows, is_bf16_valued=True)

    def copy_from(buf):
        def copy(src, dst, size):
            return pltpu.make_async_copy(cbuf_ref.at[buf, pl.ds(src, size)], xs_ref.at[pl.ds(dst, size)],
                                         sem.at[buf])
        return copy
    _segment_copies(base_ref, cnt_ref, copy_from(cur), tm, n_slots, spare_row + cur * tm)

    @pl.when(step > 0)
    def _():
        copy_from(1 - cur)(0, 0, n_slots).wait()

    @pl.when(step == pl.num_programs(0) - 1)
    def _():
        copy_from(cur)(0, 0, n_slots).wait()


def _dispatch(h2, base, cnt, tail, tail_len, lpost, n_rows, tm, tb):
    t, d = h2.shape
    seg = pl.BlockSpec((LANES,), lambda i: (i,), memory_space=pltpu.SMEM)
    whole = pl.BlockSpec(memory_space=pltpu.SMEM)
    return pl.pallas_call(
        functools.partial(_dispatch_kernel, spare_row=n_rows),
        grid=(t // tm,),
        in_specs=[seg, seg, whole, whole,
                  pl.BlockSpec((8, tm), lambda i: (0, i)),
                  pl.BlockSpec((tm, d), lambda i: (i, 0))],
        out_specs=pl.BlockSpec(memory_space=pl.ANY),
        out_shape=jax.ShapeDtypeStruct((n_rows + 2 * tm, d // 2), jnp.uint32),
        scratch_shapes=[pltpu.VMEM((2, _tile_slots(tm), d // 2), jnp.uint32),
                        pltpu.VMEM((tb, d // 2), jnp.uint32),
                        pltpu.SemaphoreType.DMA((2,)), pltpu.SemaphoreType.DMA(())],
        compiler_params=_cparams(("arbitrary",)),
        name="moe_dispatch",
    )(base, cnt, tail, tail_len, lpost, h2)


def _ffn_kernel(be_ref, nu_ref, x_ref, wgu_ref, bgu_ref, wdn_ref, bdn_ref, y_ref, wgu_bf_ref, wdn_bf_ref):
    de = wdn_ref.shape[2]
    i = pl.program_id(0)
    used = i < nu_ref[0]

    @pl.when(used & ((i == 0) | (be_ref[i] != be_ref[jnp.maximum(i - 1, 0)])))
    def _():
        wgu_bf_ref[...] = wgu_ref[0, 0].astype(BF16)
        wdn_bf_ref[...] = wdn_ref[0, 0].astype(BF16)

    @pl.when(used)
    def _():
        half = x_ref.shape[1]
        x_left, x_right = _unpack_rows(x_ref[...])
        gu = (jnp.dot(x_left, wgu_bf_ref[:half, :], preferred_element_type=F32)
              + jnp.dot(x_right, wgu_bf_ref[half:, :], preferred_element_type=F32) + bgu_ref[0, 0])
        gate = jnp.minimum(gu[:, :de], SWIGLU_LIMIT)
        up = jnp.clip(gu[:, de:], -SWIGLU_LIMIT, SWIGLU_LIMIT)
        act = (up + 1.0) * gate * jax.nn.sigmoid(SWIGLU_ALPHA * gate)
        y = jnp.dot(act.astype(BF16), wdn_bf_ref[...], preferred_element_type=F32) + bdn_ref[0, 0]
        y_ref[...] = _pack_rows(y)

    @pl.when(jnp.logical_not(used))
    def _():
        y_ref[...] = jnp.zeros_like(y_ref)


def _ffn(xs, n_rows, block_expert, n_used, layer, wgu, bgu, wdn, bdn, tb):
    p, half = n_rows, xs.shape[1]
    d = 2 * half
    nl, e, _, n2 = wgu.shape
    de = wdn.shape[2]
    grid_spec = pltpu.PrefetchScalarGridSpec(
        num_scalar_prefetch=2,
        grid=(p // tb,),
        in_specs=[pl.BlockSpec((tb, half), lambda i, be, nu: (i, 0)),
                  pl.BlockSpec((1, 1, d, n2), lambda i, be, nu: (layer, be[i], 0, 0)),
                  pl.BlockSpec((1, 1, 1, n2), lambda i, be, nu: (layer, be[i], 0, 0)),
                  pl.BlockSpec((1, 1, de, d), lambda i, be, nu: (layer, be[i], 0, 0)),
                  pl.BlockSpec((1, 1, 1, d), lambda i, be, nu: (layer, be[i], 0, 0))],
        out_specs=pl.BlockSpec((tb, half), lambda i, be, nu: (i, 0)),
        scratch_shapes=[pltpu.VMEM((d, n2), BF16), pltpu.VMEM((de, d), BF16)],
    )
    return pl.pallas_call(
        _ffn_kernel,
        grid_spec=grid_spec,
        out_shape=jax.ShapeDtypeStruct((p, half), jnp.uint32),
        compiler_params=_cparams(("arbitrary",)),
        name="moe_ffn",
    )(block_expert, n_used, xs, wgu, bgu.reshape(nl, e, 1, n2), wdn, bdn.reshape(nl, e, 1, d))


def _combine_kernel(base_ref, cnt_ref, base_next_ref, cnt_next_ref, lpos_ref, gate_ref, x_ref, g2_ref,
                    lng_ref, lnb_ref, ys_ref, o_ref, gbuf_ref, sem, *, alpha):
    tm = x_ref.shape[1]
    n_slots = _tile_slots(tm)
    step = pl.program_id(0) * pl.num_programs(1) + pl.program_id(1)
    n_steps = pl.num_programs(0) * pl.num_programs(1)
    cur = step % 2

    def gather(bases, cnts, buf):
        def copy(dst, src, size):
            return pltpu.make_async_copy(ys_ref.at[pl.ds(src, size)], gbuf_ref.at[buf, pl.ds(dst, size)],
                                         sem.at[buf])
        _segment_copies(bases, cnts, copy, tm, n_slots, 0)
        return copy

    @pl.when(step == 0)
    def _():
        gather(base_ref, cnt_ref, cur)

    @pl.when(step + 1 < n_steps)
    def _():
        gather(base_next_ref, cnt_next_ref, 1 - cur)

    def copy(dst, src, size):
        return pltpu.make_async_copy(ys_ref.at[pl.ds(src, size)], gbuf_ref.at[cur, pl.ds(dst, size)],
                                     sem.at[cur])

    lane = lax.broadcasted_iota(jnp.int32, (tm, n_slots), 1)
    lpos = lpos_ref[...]
    gate = gate_ref[...]
    w = jnp.zeros((tm, n_slots), F32)
    for k in range(TOP_K):
        w = jnp.where(lane == lpos[:, k:k + 1], gate[:, k:k + 1], w)
    w = w.astype(BF16)

    copy(0, 0, n_slots).wait()
    y_left, y_right = _unpack_rows(gbuf_ref[cur])
    f = jnp.concatenate([jnp.dot(w, y, preferred_element_type=F32) for y in (y_left, y_right)], axis=1)
    o_ref[0] = _layer_norm(alpha * x_ref[0] + g2_ref[0] * f, lng_ref[...], lnb_ref[...])


def _combine(base, cnt, lpos, gates, x, g2, lng, lnb, ys, n_lat_tiles, n_tiles, tm, alpha):
    b, _, d = x.shape
    nb = g2.shape[0] - 1

    def mod_map(i, j):
        return (jnp.where(j < n_lat_tiles, i, nb), 0, 0)

    tok = pl.BlockSpec((1, tm, d), lambda i, j: (i, j, 0))
    vec = pl.BlockSpec((1, d), lambda i, j: (0, 0))
    last = b * n_tiles - 1
    seg = pl.BlockSpec((LANES,), lambda i, j: (i * n_tiles + j,), memory_space=pltpu.SMEM)
    seg_next = pl.BlockSpec((LANES,), lambda i, j: (jnp.minimum(i * n_tiles + j + 1, last),),
                            memory_space=pltpu.SMEM)
    per_tok = pl.BlockSpec((tm, LANES), lambda i, j: (i * n_tiles + j, 0))
    return pl.pallas_call(
        functools.partial(_combine_kernel, alpha=alpha),
        grid=(b, n_tiles),
        in_specs=[seg, seg, seg_next, seg_next, per_tok, per_tok,
                  tok, pl.BlockSpec((1, 1, d), mod_map), vec, vec,
                  pl.BlockSpec(memory_space=pl.ANY)],
        out_specs=tok,
        out_shape=jax.ShapeDtypeStruct((b, n_tiles * tm, d), F32),
        scratch_shapes=[pltpu.VMEM((2, _tile_slots(tm), d // 2), jnp.uint32), pltpu.SemaphoreType.DMA((2,))],
        compiler_params=_cparams(("arbitrary", "arbitrary")),
        name="moe_combine_ln",
    )(base, cnt, base, cnt, lpos, gates, x, g2, lng, lnb, ys)


def _moe_layer(h2, logits, x_res, g2, lng, lnb, layer, wgu, bgu, wdn, bdn, n_lat_tiles, n_tiles, tm, alpha):
    b, rows, d = h2.shape
    t = b * rows
    tb = FFN_BLOCK
    gates, lpos, lpost, tiles, counts = _router(logits.reshape(t, LANES), tm)
    counts = counts[0, :N_EXPERTS].astype(jnp.int32)
    padded = (counts + tb - 1) // tb * tb
    pend = jnp.cumsum(padded)
    pstart = pend - padded
    tiles = tiles.reshape(t // tm, 8, LANES)[:, :2, :].astype(jnp.int32)
    seg_base = (jnp.pad(pstart, (0, LANES - N_EXPERTS))[None, :] + tiles[:, 0]).reshape(-1)
    seg_cnt = tiles[:, 1].reshape(-1)
    n_rows = t * TOP_K + (t // tm) * N_EXPERTS * (SEG_ALIGN - 1)
    n_rows = (n_rows + tb - 1) // tb * tb + N_EXPERTS * tb
    n_blocks = n_rows // tb
    block_start = jnp.arange(n_blocks, dtype=jnp.int32) * tb
    block_expert = jnp.minimum(jnp.sum((pend[None, :] <= block_start[:, None]).astype(jnp.int32), axis=1),
                               N_EXPERTS - 1).astype(jnp.int32)
    n_used = (pend[-1:] // tb).astype(jnp.int32)
    tail = jnp.pad(pstart + counts, (0, LANES - N_EXPERTS))
    tail_len = jnp.pad(padded - counts, (0, LANES - N_EXPERTS))
    xs = _dispatch(h2.reshape(t, d), seg_base, seg_cnt, tail, tail_len, lpost, n_rows, tm, tb)
    ys = _ffn(xs, n_rows, block_expert, n_used, layer, wgu, bgu, wdn, bdn, tb)
    return _combine(seg_base, seg_cnt, lpos, gates, x_res, g2, lng, lnb, ys, n_lat_tiles, n_tiles, tm, alpha)


def _rope_tables(n_lat_rows, n_rows, scale):
    pos = jnp.arange(n_lat_rows)
    lane = jnp.arange(LANES)
    m = lane % 64
    n_freq = 16
    inv = ROPE_BASE ** (-(m % n_freq).astype(F32) / n_freq)
    p = jnp.where((m // 32)[None, :] == 0, (pos // GRID_W)[:, None], (pos % GRID_W)[:, None]).astype(F32)
    ang = p * inv[None, :]
    sign = jnp.where((m % 32) < n_freq, -1.0, 1.0)[None, :]
    cos = jnp.concatenate([jnp.cos(ang), jnp.ones((n_rows - n_lat_rows, LANES), F32)], axis=0)
    sin = jnp.concatenate([jnp.sin(ang) * sign, jnp.zeros((n_rows - n_lat_rows, LANES), F32)], axis=0)
    return cos * scale, sin * scale


def kernel(x, c, ctx, c_ctx, w_ada, b_ada, ln_g, ln_b, hg_w_in, hg_lb, hg_norm_g, hg_w_out, da_w_in, da_lam,
           da_norm_g, da_w_out, moe_w_router, moe_b_router, moe_w_gu, moe_b_gu, moe_w_dn, moe_b_dn):
    bsz, seq, d = x.shape
    lc = ctx.shape[1]
    depth = w_ada.shape[0]
    assert depth == 2 and hg_w_in.shape[0] == 1 and da_w_in.shape[0] == 1
    ltot = seq + lc
    tm = TOKEN_TILE
    assert seq % tm == 0 and lc % tm == 0 and seq % GRID_W == 0 and d == DA_HEADS * LANES
    alpha = (2 * depth) ** 0.25
    n_lat_tiles = seq // tm
    n_all_tiles = ltot // tm
    h_heads = d // LANES

    n_mod = bsz + 1
    c_pad = jnp.zeros((16, d), F32).at[:bsz].set(c).at[bsz].set(c_ctx)
    mod = _ada(c_pad, w_ada, b_ada)[:, :n_mod]

    def mods(l):
        return [mod[l, :, k * d:(k + 1) * d].reshape(n_mod, 1, d) for k in range(6)]

    def router_params(l):
        wr = jnp.zeros((d, LANES), F32).at[:, :N_EXPERTS].set(moe_w_router[l])
        wr_hi = wr.astype(BF16)
        wr_lo = (wr - wr_hi.astype(F32)).astype(BF16)
        br = jnp.full((1, LANES), -1e30, F32).at[0, :N_EXPERTS].set(moe_b_router[l])
        return jnp.stack([wr_hi, wr_lo]), br

    xall = (x, ctx)

    sh1, sc1, g1, sh2, sc2, g2 = mods(0)
    p0 = _proj(xall, sc1, sh1, hg_w_in[0].astype(BF16), n_lat_tiles, tm)
    lb = jnp.cumsum(jax.nn.softmax(hg_lb.astype(F32), axis=1), axis=1)[:, 0]
    lb = lb.reshape(2, h_heads, LANES).transpose(1, 0, 2)
    y0 = _gla(p0, lb, hg_norm_g[0].reshape(h_heads, 1, LANES), seq)
    wr, br = router_params(0)
    x0, h20, lg0 = _post(y0, hg_w_out[0].astype(BF16), xall, g1, ln_g[0, 0].reshape(1, d),
                         ln_b[0, 0].reshape(1, d), sc2, sh2, wr, br, n_lat_tiles, n_all_tiles, tm, alpha)
    x1 = _moe_layer(h20, lg0, x0, g2, ln_g[0, 1].reshape(1, d), ln_b[0, 1].reshape(1, d),
                    0, moe_w_gu, moe_b_gu, moe_w_dn, moe_b_dn, n_lat_tiles, n_all_tiles, tm, alpha)

    sh1, sc1, g1, sh2, sc2, g2 = mods(1)
    dh = d // DA_HEADS // 2
    cq, sq = _rope_tables(seq, ltot, dh ** -0.5 * math.log2(math.e))
    ck, sk = _rope_tables(seq, ltot, 1.0)
    p1 = _proj(x1, sc1, sh1, da_w_in[0].astype(BF16), n_lat_tiles, tm, rope_tabs=(cq, sq, ck, sk))
    lam_init = 0.8 - 0.6 * math.exp(-0.3 * 1)
    lp = da_lam[0].astype(F32)
    lam = (jnp.exp(jnp.sum(lp[0] * lp[1])) - jnp.exp(jnp.sum(lp[2] * lp[3])) + lam_init).reshape(1)
    y1 = _attn(p1, lam, da_norm_g[0].reshape(LANES, 1), seq, 1.0 - lam_init)
    wr, br = router_params(1)
    x2, h21, lg1 = _post(y1, da_w_out[0].astype(BF16), x1, g1, ln_g[1, 0].reshape(1, d),
                         ln_b[1, 0].reshape(1, d), sc2, sh2, wr, br, n_lat_tiles, n_lat_tiles, tm, alpha)
    return _moe_layer(h21, lg1, x2, g2, ln_g[1, 1].reshape(1, d), ln_b[1, 1].reshape(1, d),
                      1, moe_w_gu, moe_b_gu, moe_w_dn, moe_b_dn, n_lat_tiles, n_lat_tiles, tm, alpha)
```

```python
import functools
import math

import jax
import jax.numpy as jnp
from jax import lax
from jax.experimental import pallas as pl
from jax.experimental.pallas import tpu as pltpu

F32 = jnp.float32
BF16 = jnp.bfloat16
HIGHEST = lax.Precision.HIGHEST

LANES = 128
HG_CHUNK = 64
GRID_W = 64
ROPE_BASE = 10000.0
DA_HEADS = 8
N_EXPERTS = 32
TOP_K = 4
SEG_ALIGN = 8
SWIGLU_LIMIT = 7.0
SWIGLU_ALPHA = 1.702
LN_EPS = 1e-5
RMS_EPS = 1e-6
VMEM_LIMIT = 56 * 1024 * 1024
MXU_COLS = 256

TOKEN_TILE = 256
FFN_BLOCK = 512
ADA_TILE_N = 1536
PROJ_CHUNK = 512
ATTN_TQ = 1024
ATTN_TK = 768
READOUT_ROWS = 256


def _cparams(sem):
    return pltpu.CompilerParams(dimension_semantics=sem, vmem_limit_bytes=VMEM_LIMIT)


def _ada_kernel(c_ref, w_ref, b_ref, o_ref):
    c = c_ref[...]
    s = c * jax.nn.sigmoid(c)
    o_ref[0] = jnp.dot(s, w_ref[0], precision=HIGHEST, preferred_element_type=F32) + b_ref[0]


def _ada(c_pad, w_ada, b_ada):
    depth, d, n = w_ada.shape
    rows = c_pad.shape[0]
    tn = ADA_TILE_N
    return pl.pallas_call(
        _ada_kernel,
        grid=(depth, n // tn),
        in_specs=[pl.BlockSpec((rows, d), lambda l, j: (0, 0)),
                  pl.BlockSpec((1, d, tn), lambda l, j: (l, 0, j)),
                  pl.BlockSpec((1, 1, tn), lambda l, j: (l, 0, j))],
        out_specs=pl.BlockSpec((1, rows, tn), lambda l, j: (l, 0, j)),
        out_shape=jax.ShapeDtypeStruct((depth, rows, n), F32),
        compiler_params=_cparams(("arbitrary", "arbitrary")),
        name="ada_mod",
    )(c_pad, w_ada, b_ada.reshape(depth, 1, n))


def _rope(piece, cos, sin, lo_mask):
    rot = jnp.where(lo_mask, pltpu.roll(piece, LANES - 16, 1), pltpu.roll(piece, 16, 1))
    return piece * cos + rot * sin


def _token_tile(x_ref, xc_ref, n_lat_tiles):
    if xc_ref is None:
        return x_ref[0]
    return jnp.where(pl.program_id(1) < n_lat_tiles, x_ref[0], xc_ref[0])


def _stream_specs(x, tm, n_lat_tiles):
    if not isinstance(x, tuple):
        return [pl.BlockSpec((1, tm, x.shape[2]), lambda i, j: (i, j, 0))], [x]
    d = x[0].shape[2]
    return ([pl.BlockSpec((1, tm, d), lambda i, j: (i, jnp.minimum(j, n_lat_tiles - 1), 0)),
             pl.BlockSpec((1, tm, d), lambda i, j: (i, jnp.maximum(j - n_lat_tiles, 0), 0))], list(x))


def _proj_kernel(*refs, n_rope, cw, two_streams, n_lat_tiles):
    x_ref, refs = refs[0], refs[1:]
    xc_ref = None
    if two_streams:
        xc_ref, refs = refs[0], refs[1:]
    sc_ref, sh_ref, w_ref, *rest = refs
    if n_rope:
        cq_ref, sq_ref, ck_ref, sk_ref, o_ref = rest
    else:
        (o_ref,) = rest
    n = w_ref.shape[1]
    h = (_token_tile(x_ref, xc_ref, n_lat_tiles) * (1.0 + sc_ref[0]) + sh_ref[0]).astype(BF16)
    if n_rope:
        lane = lax.broadcasted_iota(jnp.int32, (x_ref.shape[1], LANES), 1)
        lo_mask = (lane % 32) < 16
    per = cw // LANES
    for j in range(n // cw):
        r = jnp.dot(h, w_ref[:, j * cw:(j + 1) * cw], preferred_element_type=F32)
        for g in range(per):
            hd = j * per + g
            piece = r[:, g * LANES:(g + 1) * LANES]
            if hd < n_rope:
                piece = _rope(piece, cq_ref[...], sq_ref[...], lo_mask)
            elif hd < 2 * n_rope:
                piece = _rope(piece, ck_ref[...], sk_ref[...], lo_mask)
            o_ref[0, hd] = piece.astype(BF16)


def _proj(x, sc, sh, w, n_lat_tiles, tm, rope_tabs=None):
    two_streams = isinstance(x, tuple)
    b, _, d = (x[0] if two_streams else x).shape
    ltot = x[0].shape[1] + x[1].shape[1] if two_streams else x.shape[1]
    n = w.shape[1]
    nb = sc.shape[0] - 1
    n_rope = DA_HEADS if rope_tabs is not None else 0

    def mod_map(i, j):
        return (jnp.where(j < n_lat_tiles, i, nb), 0, 0)

    in_specs, args = _stream_specs(x, tm, n_lat_tiles)
    in_specs += [pl.BlockSpec((1, 1, d), mod_map),
                 pl.BlockSpec((1, 1, d), mod_map),
                 pl.BlockSpec((d, n), lambda i, j: (0, 0))]
    args += [sc, sh, w]
    if rope_tabs is not None:
        in_specs += [pl.BlockSpec((tm, LANES), lambda i, j: (j, 0))] * 4
        args += list(rope_tabs)
    return pl.pallas_call(
        functools.partial(_proj_kernel, n_rope=n_rope, cw=PROJ_CHUNK, two_streams=two_streams,
                          n_lat_tiles=n_lat_tiles),
        grid=(b, ltot // tm),
        in_specs=in_specs,
        out_specs=pl.BlockSpec((1, n // LANES, tm, LANES), lambda i, j: (i, 0, j, 0)),
        out_shape=jax.ShapeDtypeStruct((b, n // LANES, ltot, LANES), BF16),
        compiler_params=_cparams(("arbitrary", "arbitrary")),
        name="mod_proj_rope" if n_rope else "mod_proj",
    )(*args)


GLA_BLOCK = MXU_COLS


def _gla_local(items):
    nc = GLA_BLOCK // HG_CHUNK
    kks, cats = [], []
    for (_, _, z, lb, _, _, _) in items:
        f = lb + (1.0 - lb) * jax.nn.sigmoid(z)
        logf = jnp.log(f)
        kks.append(1.0 - f)
        hi = logf.astype(BF16)
        lo = (logf - hi.astype(F32)).astype(BF16)
        cats.append(jnp.concatenate([hi, lo], axis=1))
    parts = [jnp.dot(it[4], cat, preferred_element_type=F32) for it, cat in zip(items, cats)]
    q_decs, k_invs, k_ends, decs = [], [], [], []
    for (q, _, _, _, _, _, end_row), kk, part in zip(items, kks, parts):
        bcum = part[:, :LANES] + part[:, LANES:]
        b_end = bcum.reshape(nc, HG_CHUNK, LANES)[:, end_row:end_row + 1, :]
        b_end_rows = jnp.broadcast_to(b_end, (nc, HG_CHUNK, LANES)).reshape(GLA_BLOCK, LANES)
        q_decs.append((q * jnp.exp(bcum)).astype(BF16))
        k_invs.append((kk * jnp.exp(-bcum)).astype(BF16))
        k_ends.append((kk * jnp.exp(b_end_rows - bcum)).astype(BF16))
        decs.append(jnp.exp(b_end.reshape(nc, LANES)))
    scores = [lax.dot_general(qd, ki, (((1,), (1,)), ((), ())), preferred_element_type=F32)
              for qd, ki in zip(q_decs, k_invs)]
    scores = [jnp.where(it[5], a, 0.0).astype(BF16) for it, a in zip(items, scores)]
    outs = [jnp.dot(a, it[1], preferred_element_type=F32) for it, a in zip(items, scores)]
    return list(zip(outs, q_decs, k_ends, decs))


def _gla_kernel(q_ref, i_ref, g_ref, zf_ref, zb_ref, lb_ref, ng_ref, y_ref,
                of_ref, ob_ref, qd_ref, ke_ref, de_ref, st_ref, *, n_lat, n_ctx):
    c = HG_CHUNK
    blk = GLA_BLOCK
    nc = blk // c
    ltot = (n_lat + n_ctx) * c
    row = lax.broadcasted_iota(jnp.int32, (blk, blk), 0)
    col = lax.broadcasted_iota(jnp.int32, (blk, blk), 1)
    same = (row // c) == (col // c)
    masks = (same & (col <= row), same & (col >= row))
    tris = (masks[0].astype(BF16), masks[1].astype(BF16))
    z_refs = (zf_ref, zb_ref)
    o_refs = (of_ref, ob_ref)
    end_rows = (c - 1, 0)

    n_blk = ltot // blk
    per_step = 3 if n_blk % 3 == 0 else 1

    def local(t, carry):
        work = []
        for u in range(per_step):
            r0 = pl.multiple_of((t * per_step + u) * blk, blk)
            c0 = pl.multiple_of((t * per_step + u) * nc, nc)
            q = q_ref[0, 0, pl.ds(r0, blk), :].astype(F32)
            v = i_ref[0, 0, pl.ds(r0, blk), :]
            zs = [z_refs[d][0, 0, pl.ds(r0, blk), :].astype(F32) for d in range(2)]
            work.append((r0, c0, q, v, zs))
        res = _gla_local([(q, v, zs[d], lb_ref[0, d:d + 1, :], tris[d], masks[d], end_rows[d])
                          for (_, _, q, v, zs) in work for d in range(2)])
        for u, (r0, c0, _, _, _) in enumerate(work):
            for d in range(2):
                o, q_dec, k_end, dec = res[2 * u + d]
                o_refs[d][pl.ds(r0, blk), :] = o
                qd_ref[d, pl.ds(r0, blk), :] = q_dec
                ke_ref[d, pl.ds(r0, blk), :] = k_end
                de_ref[d, pl.ds(c0, nc), :] = dec
        return carry
    lax.fori_loop(0, n_blk // per_step, local, 0)

    st_ref[...] = jnp.zeros_like(st_ref)

    def segment(first, n):
        per = 16 if n % 16 == 0 else (4 if n % 4 == 0 else 1)

        def body(jj, carry):
            todo = []
            for u in range(per):
                j = jj * per + u
                for d, ch in enumerate((first + j, first + n - 1 - j)):
                    r0 = pl.multiple_of(ch * c, c)
                    kv = lax.dot_general(i_ref[0, 0, pl.ds(r0, c), :], ke_ref[d, pl.ds(r0, c), :],
                                         (((0,), (0,)), ((), ())), preferred_element_type=F32)
                    todo.append((d, r0, kv, de_ref[d, pl.ds(ch, 1), :], qd_ref[d, pl.ds(r0, c), :]))
            st = [st_ref[0], st_ref[1]]
            inter = []
            for (d, r0, kv, dec, q_dec) in todo:
                inter.append((d, r0, lax.dot_general(q_dec, st[d].astype(BF16), (((1,), (1,)), ((), ())),
                                                     preferred_element_type=F32)))
                st[d] = st[d] * dec + kv
            for (d, r0, o) in inter:
                o_refs[d][pl.ds(r0, c), :] += o
            st_ref[0] = st[0]
            st_ref[1] = st[1]
            return carry
        lax.fori_loop(0, n // per, body, 0)

    segment(n_lat, n_ctx)
    segment(0, n_lat)

    ng = ng_ref[0]
    rt = READOUT_ROWS

    def readout(t, carry):
        r0 = pl.multiple_of(t * rt, rt)
        o = of_ref[pl.ds(r0, rt), :] + ob_ref[pl.ds(r0, rt), :]
        ms = jnp.mean(o * o, axis=-1, keepdims=True)
        g = g_ref[0, 0, pl.ds(r0, rt), :].astype(F32)
        y = o * lax.rsqrt(ms + RMS_EPS) * ng * (g * jax.nn.sigmoid(g))
        y_ref[0, 0, pl.ds(r0, rt), :] = y.astype(BF16)
        return carry
    lax.fori_loop(0, ltot // rt, readout, 0, unroll=3 if (ltot // rt) % 3 == 0 else 1)


def _gla(p, lb, ng, n_lat_rows):
    b, nh5, ltot, _ = p.shape
    h = nh5 // 5
    n_lat = n_lat_rows // HG_CHUNK
    n_ctx = (ltot - n_lat_rows) // HG_CHUNK

    def spec(k):
        return pl.BlockSpec((1, 1, ltot, LANES), lambda i, j, k=k: (i, k * h + j, 0, 0))

    return pl.pallas_call(
        functools.partial(_gla_kernel, n_lat=n_lat, n_ctx=n_ctx),
        grid=(b, h),
        in_specs=[spec(0), spec(1), spec(2), spec(3), spec(4),
                  pl.BlockSpec((1, 2, LANES), lambda i, j: (j, 0, 0)),
                  pl.BlockSpec((1, 1, LANES), lambda i, j: (j, 0, 0))],
        out_specs=pl.BlockSpec((1, 1, ltot, LANES), lambda i, j: (i, j, 0, 0)),
        out_shape=jax.ShapeDtypeStruct((b, h, ltot, LANES), BF16),
        scratch_shapes=[pltpu.VMEM((ltot, LANES), F32), pltpu.VMEM((ltot, LANES), F32),
                        pltpu.VMEM((2, ltot, LANES), BF16), pltpu.VMEM((2, ltot, LANES), BF16),
                        pltpu.VMEM((2, ltot // HG_CHUNK, LANES), F32),
                        pltpu.VMEM((2, LANES, LANES), F32)],
        compiler_params=_cparams(("arbitrary", "arbitrary")),
        name="hgrn2_gla",
    )(p, p, p, p, p, lb, ng)


SUM_ROWS = 16


BOUND_SLACK = 1.0 + 2.0 ** -5
MIN_SOFTMAX_SUM = 1e-30


def _attn_kernel(lam_ref, q_ref, k_ref, v_ref, ng_ref, y_ref, vt_ref, kmax_ref, acc_ref, *, tk, out_scale):
    tq = q_ref.shape[2]
    lk = k_ref.shape[2]
    lam = lam_ref[0]
    n_chunks = lk // tk
    srow = lax.broadcasted_iota(jnp.int32, (8, LANES), 0)
    slane = lax.broadcasted_iota(jnp.int32, (8, LANES), 1)
    sel = (srow == slane // 64).astype(BF16)

    def sq_norms(x):
        return lax.dot_general(sel, x * x, (((1,), (1,)), ((), ())), preferred_element_type=F32)

    @pl.when(pl.program_id(2) == 0)
    def _():
        blk = MXU_COLS
        kn = jnp.zeros((8, blk), F32)
        for cb in range(lk // blk):
            vt_ref[0:LANES, cb * blk:(cb + 1) * blk] = jnp.transpose(
                v_ref[0, 0, cb * blk:(cb + 1) * blk, :].astype(F32)).astype(BF16)
            kn = jnp.maximum(kn, sq_norms(k_ref[0, 0, cb * blk:(cb + 1) * blk, :]))
        vt_ref[LANES:LANES + SUM_ROWS, :] = jnp.ones((SUM_ROWS, lk), BF16)
        kmax_ref[...] = jnp.broadcast_to(jnp.sqrt(jnp.max(kn, axis=1, keepdims=True)), kmax_ref.shape)

    q = q_ref[0, 0]
    lane = lax.broadcasted_iota(jnp.int32, (tq, LANES), 1)
    zero = jnp.zeros_like(q)
    qm = (jnp.where(lane < 64, q, zero), jnp.where(lane >= 64, q, zero))

    def scores(ci, c):
        kc = k_ref[0, 0, ci * tk:(ci + 1) * tk, :]
        return lax.dot_general(kc, qm[c], (((1,), (1,)), ((), ())), preferred_element_type=F32)

    bound = jnp.sqrt(sq_norms(q)) * kmax_ref[...] * BOUND_SLACK
    acc = [jnp.zeros((LANES, tq), F32) for _ in range(2)]
    part = [jnp.zeros((8, tq), F32) for _ in range(2)]
    for ci in range(n_chunks):
        vt = vt_ref[0:LANES, ci * tk:(ci + 1) * tk]
        for c in range(2):
            pf = jnp.exp2(scores(ci, c) - bound[c:c + 1])
            part[c] = part[c] + jnp.sum(pf.reshape(tk // 8, 8, tq), axis=0)
            acc[c] = acc[c] + jnp.dot(vt, pf.astype(BF16), preferred_element_type=F32)
    sums = []
    for c in range(2):
        total = jnp.sum(part[c], axis=0, keepdims=True)
        acc_ref[c, 0:LANES, :] = acc[c]
        acc_ref[c, LANES:LANES + SUM_ROWS, :] = jnp.broadcast_to(total, (SUM_ROWS, tq))
        sums.append(total)
    underflow = jnp.logical_not(jnp.min(jnp.minimum(sums[0], sums[1])) > MIN_SOFTMAX_SUM)

    @pl.when(underflow)
    def _():
        def body(ci, carry):
            r0 = pl.multiple_of(ci * tk, tk)
            kc = k_ref[0, 0, pl.ds(r0, tk), :]
            vt = vt_ref[:, pl.ds(r0, tk)]
            out = []
            for c in range(2):
                m_c, a_c = carry[c]
                s_c = lax.dot_general(kc, qm[c], (((1,), (1,)), ((), ())), preferred_element_type=F32)
                m_new = jnp.maximum(m_c, jnp.max(s_c, axis=0, keepdims=True))
                p = jnp.exp2(s_c - m_new).astype(BF16)
                out.append((m_new, jnp.exp2(m_c - m_new) * a_c + jnp.dot(vt, p, preferred_element_type=F32)))
            return tuple(out)
        init = tuple((jnp.full((1, tq), -1e30, F32), jnp.zeros((LANES + SUM_ROWS, tq), F32))
                     for _ in range(2))
        (_, b0), (_, b1) = lax.fori_loop(0, n_chunks, body, init)
        acc_ref[0] = b0
        acc_ref[1] = b1

    a0 = acc_ref[0]
    a1 = acc_ref[1]
    o = a0[:LANES] / a0[LANES:LANES + 1] - lam * (a1[:LANES] / a1[LANES:LANES + 1])
    ms = jnp.mean(o * o, axis=0, keepdims=True)
    y = o * lax.rsqrt(ms + RMS_EPS) * ng_ref[...] * out_scale
    y_ref[0, 0] = jnp.transpose(y).astype(BF16)


def _attn(p, lam, ng, n_lat_rows, out_scale, tq=ATTN_TQ, tk=ATTN_TK):
    b, nh3, ltot, _ = p.shape
    h = nh3 // 3
    assert ltot % tk == 0 and n_lat_rows % tq == 0 and ltot % MXU_COLS == 0
    return pl.pallas_call(
        functools.partial(_attn_kernel, tk=tk, out_scale=out_scale),
        grid=(b, h, n_lat_rows // tq),
        in_specs=[pl.BlockSpec(memory_space=pltpu.SMEM),
                  pl.BlockSpec((1, 1, tq, LANES), lambda i, j, t: (i, j, t, 0)),
                  pl.BlockSpec((1, 1, ltot, LANES), lambda i, j, t: (i, h + j, 0, 0)),
                  pl.BlockSpec((1, 1, ltot, LANES), lambda i, j, t: (i, 2 * h + j, 0, 0)),
                  pl.BlockSpec((LANES, 1), lambda i, j, t: (0, 0))],
        out_specs=pl.BlockSpec((1, 1, tq, LANES), lambda i, j, t: (i, j, t, 0)),
        out_shape=jax.ShapeDtypeStruct((b, h, n_lat_rows, LANES), BF16),
        scratch_shapes=[pltpu.VMEM((LANES + SUM_ROWS, ltot), BF16), pltpu.VMEM((8, tq), F32),
                        pltpu.VMEM((2, LANES + SUM_ROWS, tq), F32)],
        compiler_params=_cparams(("arbitrary", "arbitrary", "arbitrary")),
        name="diff_attn",
    )(lam, p, p, p, ng)


def _layer_norm(x, g, b):
    mu = jnp.mean(x, axis=-1, keepdims=True)
    xc = x - mu
    var = jnp.mean(xc * xc, axis=-1, keepdims=True)
    return xc * lax.rsqrt(var + LN_EPS) * g + b


def _post_kernel(y_ref, w_ref, *refs, alpha, two_streams, n_lat_tiles):
    x_ref, refs = refs[0], refs[1:]
    xc_ref = None
    if two_streams:
        xc_ref, refs = refs[0], refs[1:]
    g1_ref, lng_ref, lnb_ref, sc2_ref, sh2_ref, wr_ref, br_ref, xo_ref, h2_ref, lg_ref = refs
    nh = y_ref.shape[1]
    y = jnp.concatenate([y_ref[0, k] for k in range(nh)], axis=-1)
    m = jnp.dot(y, w_ref[...], preferred_element_type=F32)
    x_res = _token_tile(x_ref, xc_ref, n_lat_tiles)
    xl = _layer_norm(alpha * x_res + g1_ref[0] * m, lng_ref[...], lnb_ref[...])
    xo_ref[0] = xl
    h2 = xl * (1.0 + sc2_ref[0]) + sh2_ref[0]
    h_hi = h2.astype(BF16)
    h2_ref[0] = h_hi
    h_lo = (h2 - h_hi.astype(F32)).astype(BF16)
    tm = h_hi.shape[0]
    both = jnp.dot(jnp.concatenate([h_hi, h_lo], axis=0), wr_ref[0], preferred_element_type=F32)
    lg_ref[0] = (both[:tm] + both[tm:]
                 + jnp.dot(h_hi, wr_ref[1], preferred_element_type=F32) + br_ref[...])


def _post(y, w_out, x, g1, lng, lnb, sc2, sh2, wr, br, n_lat_tiles, n_tiles, tm, alpha):
    b, nh, _, _ = y.shape
    two_streams = isinstance(x, tuple)
    d = (x[0] if two_streams else x).shape[2]
    nb = g1.shape[0] - 1
    rows = n_tiles * tm

    def mod_map(i, j):
        return (jnp.where(j < n_lat_tiles, i, nb), 0, 0)

    tok = pl.BlockSpec((1, tm, d), lambda i, j: (i, j, 0))
    vec = pl.BlockSpec((1, d), lambda i, j: (0, 0))
    x_specs, x_args = _stream_specs(x, tm, n_lat_tiles)
    return pl.pallas_call(
        functools.partial(_post_kernel, alpha=alpha, two_streams=two_streams, n_lat_tiles=n_lat_tiles),
        grid=(b, n_tiles),
        in_specs=[pl.BlockSpec((1, nh, tm, LANES), lambda i, j: (i, 0, j, 0)),
                  pl.BlockSpec((d, d), lambda i, j: (0, 0)),
                  *x_specs, pl.BlockSpec((1, 1, d), mod_map), vec, vec,
                  pl.BlockSpec((1, 1, d), mod_map), pl.BlockSpec((1, 1, d), mod_map),
                  pl.BlockSpec((2, d, LANES), lambda i, j: (0, 0, 0)),
                  pl.BlockSpec((1, LANES), lambda i, j: (0, 0))],
        out_specs=[tok, tok, pl.BlockSpec((1, tm, LANES), lambda i, j: (i, j, 0))],
        out_shape=[jax.ShapeDtypeStruct((b, rows, d), F32),
                   jax.ShapeDtypeStruct((b, rows, d), BF16),
                   jax.ShapeDtypeStruct((b, rows, LANES), F32)],
        compiler_params=_cparams(("arbitrary", "arbitrary")),
        name="out_proj_ln",
    )(y, w_out, *x_args, g1, lng, lnb, sc2, sh2, wr, br)


def _router_kernel(lg_ref, gate_ref, lpos_ref, lpost_ref, tile_ref, cnt_ref, carry_ref):
    tm = lg_ref.shape[0]
    ne = N_EXPERTS

    @pl.when(pl.program_id(0) == 0)
    def _():
        carry_ref[...] = jnp.zeros_like(carry_ref)

    work = jnp.transpose(lg_ref[...])[0:ne, :]
    row_f = lax.broadcasted_iota(jnp.int32, (ne, tm), 0).astype(F32)
    vals, onehots = [], []
    for k in range(TOP_K):
        mx = jnp.max(work, axis=0, keepdims=True)
        am = jnp.min(jnp.where(work == mx, row_f, float(ne)), axis=0, keepdims=True)
        oh = row_f == am
        vals.append(mx)
        onehots.append(oh)
        work = jnp.where(oh, -3e38, work)
    es = [jnp.exp(v - vals[0]) for v in vals]
    denom = es[0] + es[1] + es[2] + es[3]
    member = (onehots[0] | onehots[1] | onehots[2] | onehots[3])
    s_idx = lax.broadcasted_iota(jnp.int32, (tm, tm), 0)
    t_idx = lax.broadcasted_iota(jnp.int32, (tm, tm), 1)
    before = jnp.dot(member.astype(BF16), (s_idx < t_idx).astype(BF16), preferred_element_type=F32)
    cnt = jnp.sum(member.astype(F32), axis=1, keepdims=True)
    cnt = jnp.floor((cnt + (SEG_ALIGN - 1)) * (1.0 / SEG_ALIGN)) * SEG_ALIGN
    cnt_b = jnp.broadcast_to(cnt, (ne, LANES))
    ei = lax.broadcasted_iota(jnp.int32, (ne, ne), 0)
    ej = lax.broadcasted_iota(jnp.int32, (ne, ne), 1)
    seg_start = jnp.dot((ej < ei).astype(BF16), cnt_b.astype(BF16), preferred_element_type=F32)[:, 0:1]
    slot = seg_start + before
    sub8 = lax.broadcasted_iota(jnp.int32, (8, tm), 0)
    lp8 = jnp.zeros((8, tm), F32)
    g8 = jnp.zeros((8, tm), F32)
    for k in range(TOP_K):
        lp8 = jnp.where(sub8 == k, jnp.sum(jnp.where(onehots[k], slot, 0.0), axis=0, keepdims=True), lp8)
        g8 = jnp.where(sub8 == k, es[k] / denom, g8)
    lpost_ref[...] = lp8.astype(jnp.int32)
    pad = jnp.zeros((LANES - 8, tm), F32)
    lpos_ref[...] = jnp.transpose(jnp.concatenate([lp8, pad], axis=0)).astype(jnp.int32)
    gate_ref[...] = jnp.transpose(jnp.concatenate([g8, pad], axis=0))
    cnt_lane = jnp.transpose(jnp.concatenate([cnt_b, jnp.zeros((LANES - ne, LANES), F32)], axis=0))[0:1, :]
    row8 = lax.broadcasted_iota(jnp.int32, (8, LANES), 0)
    tile_ref[...] = jnp.where(row8 == 0, carry_ref[...], jnp.where(row8 == 1, cnt_lane, 0.0))
    carry_ref[...] = carry_ref[...] + cnt_lane
    cnt_ref[...] = carry_ref[...]


def _router(logits, tm):
    t = logits.shape[0]
    tile = pl.BlockSpec((tm, LANES), lambda i: (i, 0))
    return pl.pallas_call(
        _router_kernel,
        grid=(t // tm,),
        in_specs=[tile],
        out_specs=[tile, tile, pl.BlockSpec((8, tm), lambda i: (0, i)),
                   pl.BlockSpec((8, LANES), lambda i: (i, 0)), pl.BlockSpec((1, LANES), lambda i: (0, 0))],
        out_shape=[jax.ShapeDtypeStruct((t, LANES), F32),
                   jax.ShapeDtypeStruct((t, LANES), jnp.int32),
                   jax.ShapeDtypeStruct((8, t), jnp.int32),
                   jax.ShapeDtypeStruct((t // tm * 8, LANES), F32),
                   jax.ShapeDtypeStruct((1, LANES), F32)],
        scratch_shapes=[pltpu.VMEM((1, LANES), F32)],
        compiler_params=_cparams(("arbitrary",)),
        name="router_topk",
    )(logits)


HI_MASK = 0xFFFF0000


def _pack_rows(x, is_bf16_valued=False):
    half = x.shape[1] // 2
    if not is_bf16_valued:
        x = x.astype(BF16).astype(F32)
    bits = lax.bitcast_convert_type(x, jnp.uint32)
    return bits[:, :half] | (bits[:, half:] >> 16)


def _unpack_rows(p):
    left = lax.bitcast_convert_type(p & jnp.uint32(HI_MASK), F32).astype(BF16)
    right = lax.bitcast_convert_type(p << 16, F32).astype(BF16)
    return left, right


def _copy_pieces(n, local_row, global_row, copy, max_rows, wait=False):
    for b in range(SEG_ALIGN.bit_length() - 1, max_rows.bit_length()):
        size = 1 << b

        @pl.when((n & size) != 0)
        def _():
            lo = n & (size - 1)
            piece = copy(pl.multiple_of(local_row + lo, SEG_ALIGN), pl.multiple_of(global_row + lo, SEG_ALIGN),
                         size)
            if wait:
                piece.wait()
            else:
                piece.start()


def _segment_copies(base_ref, cnt_ref, copy, max_rows, n_slots, fill_row):
    def per_expert(e, off):
        _copy_pieces(cnt_ref[e], off, base_ref[e], copy, max_rows)
        return off + cnt_ref[e]
    used = lax.fori_loop(0, N_EXPERTS, per_expert, 0)
    _copy_pieces(n_slots - used, used, fill_row, copy, max_rows)


def _tile_slots(tm):
    return tm * TOP_K + N_EXPERTS * SEG_ALIGN


def _dispatch_kernel(base_ref, cnt_ref, tail_ref, tail_len_ref, lpost_ref, h_ref, xs_ref,
                     cbuf_ref, zbuf_ref, sem, zsem, *, spare_row):
    tm = h_ref.shape[0]

    @pl.when(pl.program_id(0) == pl.num_programs(0) - 1)
    def _():
        zbuf_ref[...] = jnp.zeros_like(zbuf_ref)

        def zero_copy(src, dst, size):
            return pltpu.make_async_copy(zbuf_ref.at[pl.ds(src, size)], xs_ref.at[pl.ds(dst, size)], zsem)
        for wait in (False, True):
            def per_expert(e, carry, wait=wait):
                _copy_pieces(tail_len_ref[e], 0, tail_ref[e], zero_copy, zbuf_ref.shape[0], wait=wait)
                return carry
            lax.fori_loop(0, N_EXPERTS, per_expert, 0)

    n_slots = _tile_slots(tm)
    slot = lax.broadcasted_iota(jnp.int32, (n_slots, tm), 0)
    lp = lpost_ref[...]
    sel = slot == lp[0:1, :]
    for k in range(1, TOP_K):
        sel = sel | (slot == lp[k:k + 1, :])
    rows = jnp.dot(sel.astype(BF16), h_ref[...].astype(BF16), preferred_element_type=F32)
    step = pl.program_id(0)
    cur = step % 2
    cbuf_ref[cur] = _pack_rows(rows, is_bf16_valued=True)

    def copy_from(buf):
        def copy(src, dst, size):
            return pltpu.make_async_copy(cbuf_ref.at[buf, pl.ds(src, size)], xs_ref.at[pl.ds(dst, size)],
                                         sem.at[buf])
        return copy
    _segment_copies(base_ref, cnt_ref, copy_from(cur), tm, n_slots, spare_row + cur * tm)

    @pl.when(step > 0)
    def _():
        copy_from(1 - cur)(0, 0, n_slots).wait()

    @pl.when(step == pl.num_programs(0) - 1)
    def _():
        copy_from(cur)(0, 0, n_slots).wait()


def _dispatch(h2, base, cnt, tail, tail_len, lpost, n_rows, tm, tb):
    t, d = h2.shape
    seg = pl.BlockSpec((LANES,), lambda i: (i,), memory_space=pltpu.SMEM)
    whole = pl.BlockSpec(memory_space=pltpu.SMEM)
    return pl.pallas_call(
        functools.partial(_dispatch_kernel, spare_row=n_rows),
        grid=(t // tm,),
        in_specs=[seg, seg, whole, whole,
                  pl.BlockSpec((8, tm), lambda i: (0, i)),
                  pl.BlockSpec((tm, d), lambda i: (i, 0))],
        out_specs=pl.BlockSpec(memory_space=pl.ANY),
        out_shape=jax.ShapeDtypeStruct((n_rows + 2 * tm, d // 2), jnp.uint32),
        scratch_shapes=[pltpu.VMEM((2, _tile_slots(tm), d // 2), jnp.uint32),
                        pltpu.VMEM((tb, d // 2), jnp.uint32),
                        pltpu.SemaphoreType.DMA((2,)), pltpu.SemaphoreType.DMA(())],
        compiler_params=_cparams(("arbitrary",)),
        name="moe_dispatch",
    )(base, cnt, tail, tail_len, lpost, h2)


def _ffn_kernel(be_ref, nu_ref, x_ref, wgu_ref, bgu_ref, wdn_ref, bdn_ref, y_ref, wgu_bf_ref, wdn_bf_ref):
    de = wdn_ref.shape[2]
    i = pl.program_id(0)
    used = i < nu_ref[0]

    @pl.when(used & ((i == 0) | (be_ref[i] != be_ref[jnp.maximum(i - 1, 0)])))
    def _():
        wgu_bf_ref[...] = wgu_ref[0, 0].astype(BF16)
        wdn_bf_ref[...] = wdn_ref[0, 0].astype(BF16)

    @pl.when(used)
    def _():
        half = x_ref.shape[1]
        x_left, x_right = _unpack_rows(x_ref[...])
        gu = (jnp.dot(x_left, wgu_bf_ref[:half, :], preferred_element_type=F32)
              + jnp.dot(x_right, wgu_bf_ref[half:, :], preferred_element_type=F32) + bgu_ref[0, 0])
        gate = jnp.minimum(gu[:, :de], SWIGLU_LIMIT)
        up = jnp.clip(gu[:, de:], -SWIGLU_LIMIT, SWIGLU_LIMIT)
        act = (up + 1.0) * gate * jax.nn.sigmoid(SWIGLU_ALPHA * gate)
        y = jnp.dot(act.astype(BF16), wdn_bf_ref[...], preferred_element_type=F32) + bdn_ref[0, 0]
        y_ref[...] = _pack_rows(y)

    @pl.when(jnp.logical_not(used))
    def _():
        y_ref[...] = jnp.zeros_like(y_ref)


def _ffn(xs, n_rows, block_expert, n_used, layer, wgu, bgu, wdn, bdn, tb):
    p, half = n_rows, xs.shape[1]
    d = 2 * half
    nl, e, _, n2 = wgu.shape
    de = wdn.shape[2]
    grid_spec = pltpu.PrefetchScalarGridSpec(
        num_scalar_prefetch=2,
        grid=(p // tb,),
        in_specs=[pl.BlockSpec((tb, half), lambda i, be, nu: (i, 0)),
                  pl.BlockSpec((1, 1, d, n2), lambda i, be, nu: (layer, be[i], 0, 0)),
                  pl.BlockSpec((1, 1, 1, n2), lambda i, be, nu: (layer, be[i], 0, 0)),
                  pl.BlockSpec((1, 1, de, d), lambda i, be, nu: (layer, be[i], 0, 0)),
                  pl.BlockSpec((1, 1, 1, d), lambda i, be, nu: (layer, be[i], 0, 0))],
        out_specs=pl.BlockSpec((tb, half), lambda i, be, nu: (i, 0)),
        scratch_shapes=[pltpu.VMEM((d, n2), BF16), pltpu.VMEM((de, d), BF16)],
    )
    return pl.pallas_call(
        _ffn_kernel,
        grid_spec=grid_spec,
        out_shape=jax.ShapeDtypeStruct((p, half), jnp.uint32),
        compiler_params=_cparams(("arbitrary",)),
        name="moe_ffn",
    )(block_expert, n_used, xs, wgu, bgu.reshape(nl, e, 1, n2), wdn, bdn.reshape(nl, e, 1, d))


def _combine_kernel(base_ref, cnt_ref, base_next_ref, cnt_next_ref, lpos_ref, gate_ref, x_ref, g2_ref,
                    lng_ref, lnb_ref, ys_ref, o_ref, gbuf_ref, sem, *, alpha):
    tm = x_ref.shape[1]
    n_slots = _tile_slots(tm)
    step = pl.program_id(0) * pl.num_programs(1) + pl.program_id(1)
    n_steps = pl.num_programs(0) * pl.num_programs(1)
    cur = step % 2

    def gather(bases, cnts, buf):
        def copy(dst, src, size):
            return pltpu.make_async_copy(ys_ref.at[pl.ds(src, size)], gbuf_ref.at[buf, pl.ds(dst, size)],
                                         sem.at[buf])
        _segment_copies(bases, cnts, copy, tm, n_slots, 0)
        return copy

    @pl.when(step == 0)
    def _():
        gather(base_ref, cnt_ref, cur)

    @pl.when(step + 1 < n_steps)
    def _():
        gather(base_next_ref, cnt_next_ref, 1 - cur)

    def copy(dst, src, size):
        return pltpu.make_async_copy(ys_ref.at[pl.ds(src, size)], gbuf_ref.at[cur, pl.ds(dst, size)],
                                     sem.at[cur])

    lane = lax.broadcasted_iota(jnp.int32, (tm, n_slots), 1)
    lpos = lpos_ref[...]
    gate = gate_ref[...]
    w = jnp.zeros((tm, n_slots), F32)
    for k in range(TOP_K):
        w = jnp.where(lane == lpos[:, k:k + 1], gate[:, k:k + 1], w)
    w = w.astype(BF16)

    copy(0, 0, n_slots).wait()
    y_left, y_right = _unpack_rows(gbuf_ref[cur])
    f = jnp.concatenate([jnp.dot(w, y, preferred_element_type=F32) for y in (y_left, y_right)], axis=1)
    o_ref[0] = _layer_norm(alpha * x_ref[0] + g2_ref[0] * f, lng_ref[...], lnb_ref[...])


def _combine(base, cnt, lpos, gates, x, g2, lng, lnb, ys, n_lat_tiles, n_tiles, tm, alpha):
    b, _, d = x.shape
    nb = g2.shape[0] - 1

    def mod_map(i, j):
        return (jnp.where(j < n_lat_tiles, i, nb), 0, 0)

    tok = pl.BlockSpec((1, tm, d), lambda i, j: (i, j, 0))
    vec = pl.BlockSpec((1, d), lambda i, j: (0, 0))
    last = b * n_tiles - 1
    seg = pl.BlockSpec((LANES,), lambda i, j: (i * n_tiles + j,), memory_space=pltpu.SMEM)
    seg_next = pl.BlockSpec((LANES,), lambda i, j: (jnp.minimum(i * n_tiles + j + 1, last),),
                            memory_space=pltpu.SMEM)
    per_tok = pl.BlockSpec((tm, LANES), lambda i, j: (i * n_tiles + j, 0))
    return pl.pallas_call(
        functools.partial(_combine_kernel, alpha=alpha),
        grid=(b, n_tiles),
        in_specs=[seg, seg, seg_next, seg_next, per_tok, per_tok,
                  tok, pl.BlockSpec((1, 1, d), mod_map), vec, vec,
                  pl.BlockSpec(memory_space=pl.ANY)],
        out_specs=tok,
        out_shape=jax.ShapeDtypeStruct((b, n_tiles * tm, d), F32),
        scratch_shapes=[pltpu.VMEM((2, _tile_slots(tm), d // 2), jnp.uint32), pltpu.SemaphoreType.DMA((2,))],
        compiler_params=_cparams(("arbitrary", "arbitrary")),
        name="moe_combine_ln",
    )(base, cnt, base, cnt, lpos, gates, x, g2, lng, lnb, ys)


def _moe_layer(h2, logits, x_res, g2, lng, lnb, layer, wgu, bgu, wdn, bdn, n_lat_tiles, n_tiles, tm, alpha):
    b, rows, d = h2.shape
    t = b * rows
    tb = FFN_BLOCK
    gates, lpos, lpost, tiles, counts = _router(logits.reshape(t, LANES), tm)
    counts = counts[0, :N_EXPERTS].astype(jnp.int32)
    padded = (counts + tb - 1) // tb * tb
    pend = jnp.cumsum(padded)
    pstart = pend - padded
    tiles = tiles.reshape(t // tm, 8, LANES)[:, :2, :].astype(jnp.int32)
    seg_base = (jnp.pad(pstart, (0, LANES - N_EXPERTS))[None, :] + tiles[:, 0]).reshape(-1)
    seg_cnt = tiles[:, 1].reshape(-1)
    n_rows = t * TOP_K + (t // tm) * N_EXPERTS * (SEG_ALIGN - 1)
    n_rows = (n_rows + tb - 1) // tb * tb + N_EXPERTS * tb
    n_blocks = n_rows // tb
    block_start = jnp.arange(n_blocks, dtype=jnp.int32) * tb
    block_expert = jnp.minimum(jnp.sum((pend[None, :] <= block_start[:, None]).astype(jnp.int32), axis=1),
                               N_EXPERTS - 1).astype(jnp.int32)
    n_used = (pend[-1:] // tb).astype(jnp.int32)
    tail = jnp.pad(pstart + counts, (0, LANES - N_EXPERTS))
    tail_len = jnp.pad(padded - counts, (0, LANES - N_EXPERTS))
    xs = _dispatch(h2.reshape(t, d), seg_base, seg_cnt, tail, tail_len, lpost, n_rows, tm, tb)
    ys = _ffn(xs, n_rows, block_expert, n_used, layer, wgu, bgu, wdn, bdn, tb)
    return _combine(seg_base, seg_cnt, lpos, gates, x_res, g2, lng, lnb, ys, n_lat_tiles, n_tiles, tm, alpha)


def _rope_tables(n_lat_rows, n_rows, scale):
    pos = jnp.arange(n_lat_rows)
    lane = jnp.arange(LANES)
    m = lane % 64
    n_freq = 16
    inv = ROPE_BASE ** (-(m % n_freq).astype(F32) / n_freq)
    p = jnp.where((m // 32)[None, :] == 0, (pos // GRID_W)[:, None], (pos % GRID_W)[:, None]).astype(F32)
    ang = p * inv[None, :]
    sign = jnp.where((m % 32) < n_freq, -1.0, 1.0)[None, :]
    cos = jnp.concatenate([jnp.cos(ang), jnp.ones((n_rows - n_lat_rows, LANES), F32)], axis=0)
    sin = jnp.concatenate([jnp.sin(ang) * sign, jnp.zeros((n_rows - n_lat_rows, LANES), F32)], axis=0)
    return cos * scale, sin * scale


def kernel(x, c, ctx, c_ctx, w_ada, b_ada, ln_g, ln_b, hg_w_in, hg_lb, hg_norm_g, hg_w_out, da_w_in, da_lam,
           da_norm_g, da_w_out, moe_w_router, moe_b_router, moe_w_gu, moe_b_gu, moe_w_dn, moe_b_dn):
    bsz, seq, d = x.shape
    lc = ctx.shape[1]
    depth = w_ada.shape[0]
    assert depth == 2 and hg_w_in.shape[0] == 1 and da_w_in.shape[0] == 1
    ltot = seq + lc
    tm = TOKEN_TILE
    assert seq % tm == 0 and lc % tm == 0 and seq % GRID_W == 0 and d == DA_HEADS * LANES
    alpha = (2 * depth) ** 0.25
    n_lat_tiles = seq // tm
    n_all_tiles = ltot // tm
    h_heads = d // LANES

    n_mod = bsz + 1
    c_pad = jnp.zeros((16, d), F32).at[:bsz].set(c).at[bsz].set(c_ctx)
    mod = _ada(c_pad, w_ada, b_ada)[:, :n_mod]

    def mods(l):
        return [mod[l, :, k * d:(k + 1) * d].reshape(n_mod, 1, d) for k in range(6)]

    def router_params(l):
        wr = jnp.zeros((d, LANES), F32).at[:, :N_EXPERTS].set(moe_w_router[l])
        wr_hi = wr.astype(BF16)
        wr_lo = (wr - wr_hi.astype(F32)).astype(BF16)
        br = jnp.full((1, LANES), -1e30, F32).at[0, :N_EXPERTS].set(moe_b_router[l])
        return jnp.stack([wr_hi, wr_lo]), br

    xall = (x, ctx)

    sh1, sc1, g1, sh2, sc2, g2 = mods(0)
    p0 = _proj(xall, sc1, sh1, hg_w_in[0].astype(BF16), n_lat_tiles, tm)
    lb = jnp.cumsum(jax.nn.softmax(hg_lb.astype(F32), axis=1), axis=1)[:, 0]
    lb = lb.reshape(2, h_heads, LANES).transpose(1, 0, 2)
    y0 = _gla(p0, lb, hg_norm_g[0].reshape(h_heads, 1, LANES), seq)
    wr, br = router_params(0)
    x0, h20, lg0 = _post(y0, hg_w_out[0].astype(BF16), xall, g1, ln_g[0, 0].reshape(1, d),
                         ln_b[0, 0].reshape(1, d), sc2, sh2, wr, br, n_lat_tiles, n_all_tiles, tm, alpha)
    x1 = _moe_layer(h20, lg0, x0, g2, ln_g[0, 1].reshape(1, d), ln_b[0, 1].reshape(1, d),
                    0, moe_w_gu, moe_b_gu, moe_w_dn, moe_b_dn, n_lat_tiles, n_all_tiles, tm, alpha)

    sh1, sc1, g1, sh2, sc2, g2 = mods(1)
    dh = d // DA_HEADS // 2
    cq, sq = _rope_tables(seq, ltot, dh ** -0.5 * math.log2(math.e))
    ck, sk = _rope_tables(seq, ltot, 1.0)
    p1 = _proj(x1, sc1, sh1, da_w_in[0].astype(BF16), n_lat_tiles, tm, rope_tabs=(cq, sq, ck, sk))
    lam_init = 0.8 - 0.6 * math.exp(-0.3 * 1)
    lp = da_lam[0].astype(F32)
    lam = (jnp.exp(jnp.sum(lp[0] * lp[1])) - jnp.exp(jnp.sum(lp[2] * lp[3])) + lam_init).reshape(1)
    y1 = _attn(p1, lam, da_norm_g[0].reshape(LANES, 1), seq, 1.0 - lam_init)
    wr, br = router_params(1)
    x2, h21, lg1 = _post(y1, da_w_out[0].astype(BF16), x1, g1, ln_g[1, 0].reshape(1, d),
                         ln_b[1, 0].reshape(1, d), sc2, sh2, wr, br, n_lat_tiles, n_lat_tiles, tm, alpha)
    return _moe_layer(h21, lg1, x2, g2, ln_g[1, 1].reshape(1, d), ln_b[1, 1].reshape(1, d),
                      1, moe_w_gu, moe_b_gu, moe_w_dn, moe_b_dn, n_lat_tiles, n_lat_tiles, tm, alpha)
```

```python
import functools
import math

import jax
import jax.numpy as jnp
from jax import lax
from jax.experimental import pallas as pl
from jax.experimental.pallas import tpu as pltpu

F32 = jnp.float32
BF16 = jnp.bfloat16
HIGHEST = lax.Precision.HIGHEST

LANES = 128
HG_CHUNK = 64
GRID_W = 64
ROPE_BASE = 10000.0
DA_HEADS = 8
N_EXPERTS = 32
TOP_K = 4
SEG_ALIGN = 8
SWIGLU_LIMIT = 7.0
SWIGLU_ALPHA = 1.702
LN_EPS = 1e-5
RMS_EPS = 1e-6
VMEM_LIMIT = 56 * 1024 * 1024
MXU_COLS = 256

TOKEN_TILE = 256
FFN_BLOCK = 512
ADA_TILE_N = 1536
PROJ_CHUNK = 512
ATTN_TQ = 1024
ATTN_TK = 768
READOUT_ROWS = 256


def _cparams(sem):
    return pltpu.CompilerParams(dimension_semantics=sem, vmem_limit_bytes=VMEM_LIMIT)


def _ada_kernel(c_ref, w_ref, b_ref, o_ref):
    c = c_ref[...]
    s = c * jax.nn.sigmoid(c)
    o_ref[0] = jnp.dot(s, w_ref[0], precision=HIGHEST, preferred_element_type=F32) + b_ref[0]


def _ada(c_pad, w_ada, b_ada):
    depth, d, n = w_ada.shape
    rows = c_pad.shape[0]
    tn = ADA_TILE_N
    return pl.pallas_call(
        _ada_kernel,
        grid=(depth, n // tn),
        in_specs=[pl.BlockSpec((rows, d), lambda l, j: (0, 0)),
                  pl.BlockSpec((1, d, tn), lambda l, j: (l, 0, j)),
                  pl.BlockSpec((1, 1, tn), lambda l, j: (l, 0, j))],
        out_specs=pl.BlockSpec((1, rows, tn), lambda l, j: (l, 0, j)),
        out_shape=jax.ShapeDtypeStruct((depth, rows, n), F32),
        compiler_params=_cparams(("arbitrary", "arbitrary")),
        name="ada_mod",
    )(c_pad, w_ada, b_ada.reshape(depth, 1, n))


def _rope(piece, cos, sin, lo_mask):
    rot = jnp.where(lo_mask, pltpu.roll(piece, LANES - 16, 1), pltpu.roll(piece, 16, 1))
    return piece * cos + rot * sin


def _token_tile(x_ref, xc_ref, n_lat_tiles):
    if xc_ref is None:
        return x_ref[0]
    return jnp.where(pl.program_id(1) < n_lat_tiles, x_ref[0], xc_ref[0])


def _stream_specs(x, tm, n_lat_tiles):
    if not isinstance(x, tuple):
        return [pl.BlockSpec((1, tm, x.shape[2]), lambda i, j: (i, j, 0))], [x]
    d = x[0].shape[2]
    return ([pl.BlockSpec((1, tm, d), lambda i, j: (i, jnp.minimum(j, n_lat_tiles - 1), 0)),
             pl.BlockSpec((1, tm, d), lambda i, j: (i, jnp.maximum(j - n_lat_tiles, 0), 0))], list(x))


def _proj_kernel(*refs, n_rope, cw, two_streams, n_lat_tiles):
    x_ref, refs = refs[0], refs[1:]
    xc_ref = None
    if two_streams:
        xc_ref, refs = refs[0], refs[1:]
    sc_ref, sh_ref, w_ref, *rest = refs
    if n_rope:
        cq_ref, sq_ref, ck_ref, sk_ref, o_ref = rest
    else:
        (o_ref,) = rest
    n = w_ref.shape[1]
    h = (_token_tile(x_ref, xc_ref, n_lat_tiles) * (1.0 + sc_ref[0]) + sh_ref[0]).astype(BF16)
    if n_rope:
        lane = lax.broadcasted_iota(jnp.int32, (x_ref.shape[1], LANES), 1)
        lo_mask = (lane % 32) < 16
    per = cw // LANES
    for j in range(n // cw):
        r = jnp.dot(h, w_ref[:, j * cw:(j + 1) * cw], preferred_element_type=F32)
        for g in range(per):
            hd = j * per + g
            piece = r[:, g * LANES:(g + 1) * LANES]
            if hd < n_rope:
                piece = _rope(piece, cq_ref[...], sq_ref[...], lo_mask)
            elif hd < 2 * n_rope:
                piece = _rope(piece, ck_ref[...], sk_ref[...], lo_mask)
            o_ref[0, hd] = piece.astype(BF16)


def _proj(x, sc, sh, w, n_lat_tiles, tm, rope_tabs=None):
    two_streams = isinstance(x, tuple)
    b, _, d = (x[0] if two_streams else x).shape
    ltot = x[0].shape[1] + x[1].shape[1] if two_streams else x.shape[1]
    n = w.shape[1]
    nb = sc.shape[0] - 1
    n_rope = DA_HEADS if rope_tabs is not None else 0

    def mod_map(i, j):
        return (jnp.where(j < n_lat_tiles, i, nb), 0, 0)

    in_specs, args = _stream_specs(x, tm, n_lat_tiles)
    in_specs += [pl.BlockSpec((1, 1, d), mod_map),
                 pl.BlockSpec((1, 1, d), mod_map),
                 pl.BlockSpec((d, n), lambda i, j: (0, 0))]
    args += [sc, sh, w]
    if rope_tabs is not None:
        in_specs += [pl.BlockSpec((tm, LANES), lambda i, j: (j, 0))] * 4
        args += list(rope_tabs)
    return pl.pallas_call(
        functools.partial(_proj_kernel, n_rope=n_rope, cw=PROJ_CHUNK, two_streams=two_streams,
                          n_lat_tiles=n_lat_tiles),
        grid=(b, ltot // tm),
        in_specs=in_specs,
        out_specs=pl.BlockSpec((1, n // LANES, tm, LANES), lambda i, j: (i, 0, j, 0)),
        out_shape=jax.ShapeDtypeStruct((b, n // LANES, ltot, LANES), BF16),
        compiler_params=_cparams(("arbitrary", "arbitrary")),
        name="mod_proj_rope" if n_rope else "mod_proj",
    )(*args)


GLA_BLOCK = MXU_COLS


def _gla_local(items):
    nc = GLA_BLOCK // HG_CHUNK
    kks, cats = [], []
    for (_, _, z, lb, _, _, _) in items:
        f = lb + (1.0 - lb) * jax.nn.sigmoid(z)
        logf = jnp.log(f)
        kks.append(1.0 - f)
        hi = logf.astype(BF16)
        lo = (logf - hi.astype(F32)).astype(BF16)
        cats.append(jnp.concatenate([hi, lo], axis=1))
    parts = [jnp.dot(it[4], cat, preferred_element_type=F32) for it, cat in zip(items, cats)]
    q_decs, k_invs, k_ends, decs = [], [], [], []
    for (q, _, _, _, _, _, end_row), kk, part in zip(items, kks, parts):
        bcum = part[:, :LANES] + part[:, LANES:]
        b_end = bcum.reshape(nc, HG_CHUNK, LANES)[:, end_row:end_row + 1, :]
        b_end_rows = jnp.broadcast_to(b_end, (nc, HG_CHUNK, LANES)).reshape(GLA_BLOCK, LANES)
        q_decs.append((q * jnp.exp(bcum)).astype(BF16))
        k_invs.append((kk * jnp.exp(-bcum)).astype(BF16))
        k_ends.append((kk * jnp.exp(b_end_rows - bcum)).astype(BF16))
        decs.append(jnp.exp(b_end.reshape(nc, LANES)))
    scores = [lax.dot_general(qd, ki, (((1,), (1,)), ((), ())), preferred_element_type=F32)
              for qd, ki in zip(q_decs, k_invs)]
    scores = [jnp.where(it[5], a, 0.0).astype(BF16) for it, a in zip(items, scores)]
    outs = [jnp.dot(a, it[1], preferred_element_type=F32) for it, a in zip(items, scores)]
    return list(zip(outs, q_decs, k_ends, decs))


def _gla_kernel(q_ref, i_ref, g_ref, zf_ref, zb_ref, lb_ref, ng_ref, y_ref,
                of_ref, ob_ref, qd_ref, ke_ref, de_ref, st_ref, *, n_lat, n_ctx):
    c = HG_CHUNK
    blk = GLA_BLOCK
    nc = blk // c
    ltot = (n_lat + n_ctx) * c
    row = lax.broadcasted_iota(jnp.int32, (blk, blk), 0)
    col = lax.broadcasted_iota(jnp.int32, (blk, blk), 1)
    same = (row // c) == (col // c)
    masks = (same & (col <= row), same & (col >= row))
    tris = (masks[0].astype(BF16), masks[1].astype(BF16))
    z_refs = (zf_ref, zb_ref)
    o_refs = (of_ref, ob_ref)
    end_rows = (c - 1, 0)

    n_blk = ltot // blk
    per_step = 3 if n_blk % 3 == 0 else 1

    def local(t, carry):
        work = []
        for u in range(per_step):
            r0 = pl.multiple_of((t * per_step + u) * blk, blk)
            c0 = pl.multiple_of((t * per_step + u) * nc, nc)
            q = q_ref[0, 0, pl.ds(r0, blk), :].astype(F32)
            v = i_ref[0, 0, pl.ds(r0, blk), :]
            zs = [z_refs[d][0, 0, pl.ds(r0, blk), :].astype(F32) for d in range(2)]
            work.append((r0, c0, q, v, zs))
        res = _gla_local([(q, v, zs[d], lb_ref[0, d:d + 1, :], tris[d], masks[d], end_rows[d])
                          for (_, _, q, v, zs) in work for d in range(2)])
        for u, (r0, c0, _, _, _) in enumerate(work):
            for d in range(2):
                o, q_dec, k_end, dec = res[2 * u + d]
                o_refs[d][pl.ds(r0, blk), :] = o
                qd_ref[d, pl.ds(r0, blk), :] = q_dec
                ke_ref[d, pl.ds(r0, blk), :] = k_end
                de_ref[d, pl.ds(c0, nc), :] = dec
        return carry
    lax.fori_loop(0, n_blk // per_step, local, 0)

    st_ref[...] = jnp.zeros_like(st_ref)

    def segment(first, n):
        per = 32 if n % 32 == 0 else (4 if n % 4 == 0 else 1)

        def body(jj, carry):
            todo = []
            for u in range(per):
                j = jj * per + u
                for d, ch in enumerate((first + j, first + n - 1 - j)):
                    r0 = pl.multiple_of(ch * c, c)
                    kv = lax.dot_general(i_ref[0, 0, pl.ds(r0, c), :], ke_ref[d, pl.ds(r0, c), :],
                                         (((0,), (0,)), ((), ())), preferred_element_type=F32)
                    todo.append((d, r0, kv, de_ref[d, pl.ds(ch, 1), :], qd_ref[d, pl.ds(r0, c), :]))
            st = [st_ref[0], st_ref[1]]
            inter = []
            for (d, r0, kv, dec, q_dec) in todo:
                inter.append((d, r0, lax.dot_general(q_dec, st[d].astype(BF16), (((1,), (1,)), ((), ())),
                                                     preferred_element_type=F32)))
                st[d] = st[d] * dec + kv
            for (d, r0, o) in inter:
                o_refs[d][pl.ds(r0, c), :] += o
            st_ref[0] = st[0]
            st_ref[1] = st[1]
            return carry
        lax.fori_loop(0, n // per, body, 0)

    segment(n_lat, n_ctx)
    segment(0, n_lat)

    ng = ng_ref[0]
    rt = READOUT_ROWS

    def readout(t, carry):
        r0 = pl.multiple_of(t * rt, rt)
        o = of_ref[pl.ds(r0, rt), :] + ob_ref[pl.ds(r0, rt), :]
        ms = jnp.mean(o * o, axis=-1, keepdims=True)
        g = g_ref[0, 0, pl.ds(r0, rt), :].astype(F32)
        y = o * lax.rsqrt(ms + RMS_EPS) * ng * (g * jax.nn.sigmoid(g))
        y_ref[0, 0, pl.ds(r0, rt), :] = y.astype(BF16)
        return carry
    lax.fori_loop(0, ltot // rt, readout, 0, unroll=3 if (ltot // rt) % 3 == 0 else 1)


def _gla(p, lb, ng, n_lat_rows):
    b, nh5, ltot, _ = p.shape
    h = nh5 // 5
    n_lat = n_lat_rows // HG_CHUNK
    n_ctx = (ltot - n_lat_rows) // HG_CHUNK

    def spec(k):
        return pl.BlockSpec((1, 1, ltot, LANES), lambda i, j, k=k: (i, k * h + j, 0, 0))

    return pl.pallas_call(
        functools.partial(_gla_kernel, n_lat=n_lat, n_ctx=n_ctx),
        grid=(b, h),
        in_specs=[spec(0), spec(1), spec(2), spec(3), spec(4),
                  pl.BlockSpec((1, 2, LANES), lambda i, j: (j, 0, 0)),
                  pl.BlockSpec((1, 1, LANES), lambda i, j: (j, 0, 0))],
        out_specs=pl.BlockSpec((1, 1, ltot, LANES), lambda i, j: (i, j, 0, 0)),
        out_shape=jax.ShapeDtypeStruct((b, h, ltot, LANES), BF16),
        scratch_shapes=[pltpu.VMEM((ltot, LANES), F32), pltpu.VMEM((ltot, LANES), F32),
                        pltpu.VMEM((2, ltot, LANES), BF16), pltpu.VMEM((2, ltot, LANES), BF16),
                        pltpu.VMEM((2, ltot // HG_CHUNK, LANES), F32),
                        pltpu.VMEM((2, LANES, LANES), F32)],
        compiler_params=_cparams(("arbitrary", "arbitrary")),
        name="hgrn2_gla",
    )(p, p, p, p, p, lb, ng)


SUM_ROWS = 16


BOUND_SLACK = 1.0 + 2.0 ** -5
MIN_SOFTMAX_SUM = 1e-30


def _attn_kernel(lam_ref, q_ref, k_ref, v_ref, ng_ref, y_ref, vt_ref, kmax_ref, acc_ref, *, tk, out_scale):
    tq = q_ref.shape[2]
    lk = k_ref.shape[2]
    lam = lam_ref[0]
    n_chunks = lk // tk
    srow = lax.broadcasted_iota(jnp.int32, (8, LANES), 0)
    slane = lax.broadcasted_iota(jnp.int32, (8, LANES), 1)
    sel = (srow == slane // 64).astype(BF16)

    def sq_norms(x):
        return lax.dot_general(sel, x * x, (((1,), (1,)), ((), ())), preferred_element_type=F32)

    @pl.when(pl.program_id(2) == 0)
    def _():
        blk = MXU_COLS
        kn = jnp.zeros((8, blk), F32)
        for cb in range(lk // blk):
            vt_ref[0:LANES, cb * blk:(cb + 1) * blk] = jnp.transpose(
                v_ref[0, 0, cb * blk:(cb + 1) * blk, :].astype(F32)).astype(BF16)
            kn = jnp.maximum(kn, sq_norms(k_ref[0, 0, cb * blk:(cb + 1) * blk, :]))
        vt_ref[LANES:LANES + SUM_ROWS, :] = jnp.ones((SUM_ROWS, lk), BF16)
        kmax_ref[...] = jnp.broadcast_to(jnp.sqrt(jnp.max(kn, axis=1, keepdims=True)), kmax_ref.shape)

    q = q_ref[0, 0]
    lane = lax.broadcasted_iota(jnp.int32, (tq, LANES), 1)
    zero = jnp.zeros_like(q)
    qm = (jnp.where(lane < 64, q, zero), jnp.where(lane >= 64, q, zero))

    def scores(ci, c):
        kc = k_ref[0, 0, ci * tk:(ci + 1) * tk, :]
        return lax.dot_general(kc, qm[c], (((1,), (1,)), ((), ())), preferred_element_type=F32)

    bound = jnp.sqrt(sq_norms(q)) * kmax_ref[...] * BOUND_SLACK
    acc = [jnp.zeros((LANES, tq), F32) for _ in range(2)]
    part = [jnp.zeros((8, tq), F32) for _ in range(2)]
    for ci in range(n_chunks):
        vt = vt_ref[0:LANES, ci * tk:(ci + 1) * tk]
        for c in range(2):
            pf = jnp.exp2(scores(ci, c) - bound[c:c + 1])
            part[c] = part[c] + jnp.sum(pf.reshape(tk // 8, 8, tq), axis=0)
            acc[c] = acc[c] + jnp.dot(vt, pf.astype(BF16), preferred_element_type=F32)
    sums = []
    for c in range(2):
        total = jnp.sum(part[c], axis=0, keepdims=True)
        acc_ref[c, 0:LANES, :] = acc[c]
        acc_ref[c, LANES:LANES + SUM_ROWS, :] = jnp.broadcast_to(total, (SUM_ROWS, tq))
        sums.append(total)
    underflow = jnp.logical_not(jnp.min(jnp.minimum(sums[0], sums[1])) > MIN_SOFTMAX_SUM)

    @pl.when(underflow)
    def _():
        def body(ci, carry):
            r0 = pl.multiple_of(ci * tk, tk)
            kc = k_ref[0, 0, pl.ds(r0, tk), :]
            vt = vt_ref[:, pl.ds(r0, tk)]
            out = []
            for c in range(2):
                m_c, a_c = carry[c]
                s_c = lax.dot_general(kc, qm[c], (((1,), (1,)), ((), ())), preferred_element_type=F32)
                m_new = jnp.maximum(m_c, jnp.max(s_c, axis=0, keepdims=True))
                p = jnp.exp2(s_c - m_new).astype(BF16)
                out.append((m_new, jnp.exp2(m_c - m_new) * a_c + jnp.dot(vt, p, preferred_element_type=F32)))
            return tuple(out)
        init = tuple((jnp.full((1, tq), -1e30, F32), jnp.zeros((LANES + SUM_ROWS, tq), F32))
                     for _ in range(2))
        (_, b0), (_, b1) = lax.fori_loop(0, n_chunks, body, init)
        acc_ref[0] = b0
        acc_ref[1] = b1

    a0 = acc_ref[0]
    a1 = acc_ref[1]
    o = a0[:LANES] / a0[LANES:LANES + 1] - lam * (a1[:LANES] / a1[LANES:LANES + 1])
    ms = jnp.mean(o * o, axis=0, keepdims=True)
    y = o * lax.rsqrt(ms + RMS_EPS) * ng_ref[...] * out_scale
    y_ref[0, 0] = jnp.transpose(y).astype(BF16)


def _attn(p, lam, ng, n_lat_rows, out_scale, tq=ATTN_TQ, tk=ATTN_TK):
    b, nh3, ltot, _ = p.shape
    h = nh3 // 3
    assert ltot % tk == 0 and n_lat_rows % tq == 0 and ltot % MXU_COLS == 0
    return pl.pallas_call(
        functools.partial(_attn_kernel, tk=tk, out_scale=out_scale),
        grid=(b, h, n_lat_rows // tq),
        in_specs=[pl.BlockSpec(memory_space=pltpu.SMEM),
                  pl.BlockSpec((1, 1, tq, LANES), lambda i, j, t: (i, j, t, 0)),
                  pl.BlockSpec((1, 1, ltot, LANES), lambda i, j, t: (i, h + j, 0, 0)),
                  pl.BlockSpec((1, 1, ltot, LANES), lambda i, j, t: (i, 2 * h + j, 0, 0)),
                  pl.BlockSpec((LANES, 1), lambda i, j, t: (0, 0))],
        out_specs=pl.BlockSpec((1, 1, tq, LANES), lambda i, j, t: (i, j, t, 0)),
        out_shape=jax.ShapeDtypeStruct((b, h, n_lat_rows, LANES), BF16),
        scratch_shapes=[pltpu.VMEM((LANES + SUM_ROWS, ltot), BF16), pltpu.VMEM((8, tq), F32),
                        pltpu.VMEM((2, LANES + SUM_ROWS, tq), F32)],
        compiler_params=_cparams(("arbitrary", "arbitrary", "arbitrary")),
        name="diff_attn",
    )(lam, p, p, p, ng)


def _layer_norm(x, g, b):
    mu = jnp.mean(x, axis=-1, keepdims=True)
    xc = x - mu
    var = jnp.mean(xc * xc, axis=-1, keepdims=True)
    return xc * lax.rsqrt(var + LN_EPS) * g + b


def _post_kernel(y_ref, w_ref, *refs, alpha, two_streams, n_lat_tiles):
    x_ref, refs = refs[0], refs[1:]
    xc_ref = None
    if two_streams:
        xc_ref, refs = refs[0], refs[1:]
    g1_ref, lng_ref, lnb_ref, sc2_ref, sh2_ref, wr_ref, br_ref, xo_ref, h2_ref, lg_ref = refs
    nh = y_ref.shape[1]
    y = jnp.concatenate([y_ref[0, k] for k in range(nh)], axis=-1)
    m = jnp.dot(y, w_ref[...], preferred_element_type=F32)
    x_res = _token_tile(x_ref, xc_ref, n_lat_tiles)
    xl = _layer_norm(alpha * x_res + g1_ref[0] * m, lng_ref[...], lnb_ref[...])
    xo_ref[0] = xl
    h2 = xl * (1.0 + sc2_ref[0]) + sh2_ref[0]
    h_hi = h2.astype(BF16)
    h2_ref[0] = h_hi
    h_lo = (h2 - h_hi.astype(F32)).astype(BF16)
    tm = h_hi.shape[0]
    both = jnp.dot(jnp.concatenate([h_hi, h_lo], axis=0), wr_ref[0], preferred_element_type=F32)
    lg_ref[0] = (both[:tm] + both[tm:]
                 + jnp.dot(h_hi, wr_ref[1], preferred_element_type=F32) + br_ref[...])


def _post(y, w_out, x, g1, lng, lnb, sc2, sh2, wr, br, n_lat_tiles, n_tiles, tm, alpha):
    b, nh, _, _ = y.shape
    two_streams = isinstance(x, tuple)
    d = (x[0] if two_streams else x).shape[2]
    nb = g1.shape[0] - 1
    rows = n_tiles * tm

    def mod_map(i, j):
        return (jnp.where(j < n_lat_tiles, i, nb), 0, 0)

    tok = pl.BlockSpec((1, tm, d), lambda i, j: (i, j, 0))
    vec = pl.BlockSpec((1, d), lambda i, j: (0, 0))
    x_specs, x_args = _stream_specs(x, tm, n_lat_tiles)
    return pl.pallas_call(
        functools.partial(_post_kernel, alpha=alpha, two_streams=two_streams, n_lat_tiles=n_lat_tiles),
        grid=(b, n_tiles),
        in_specs=[pl.BlockSpec((1, nh, tm, LANES), lambda i, j: (i, 0, j, 0)),
                  pl.BlockSpec((d, d), lambda i, j: (0, 0)),
                  *x_specs, pl.BlockSpec((1, 1, d), mod_map), vec, vec,
                  pl.BlockSpec((1, 1, d), mod_map), pl.BlockSpec((1, 1, d), mod_map),
                  pl.BlockSpec((2, d, LANES), lambda i, j: (0, 0, 0)),
                  pl.BlockSpec((1, LANES), lambda i, j: (0, 0))],
        out_specs=[tok, tok, pl.BlockSpec((1, tm, LANES), lambda i, j: (i, j, 0))],
        out_shape=[jax.ShapeDtypeStruct((b, rows, d), F32),
                   jax.ShapeDtypeStruct((b, rows, d), BF16),
                   jax.ShapeDtypeStruct((b, rows, LANES), F32)],
        compiler_params=_cparams(("arbitrary", "arbitrary")),
        name="out_proj_ln",
    )(y, w_out, *x_args, g1, lng, lnb, sc2, sh2, wr, br)


def _router_kernel(lg_ref, gate_ref, lpos_ref, lpost_ref, tile_ref, cnt_ref, carry_ref):
    tm = lg_ref.shape[0]
    ne = N_EXPERTS

    @pl.when(pl.program_id(0) == 0)
    def _():
        carry_ref[...] = jnp.zeros_like(carry_ref)

    work = jnp.transpose(lg_ref[...])[0:ne, :]
    row_f = lax.broadcasted_iota(jnp.int32, (ne, tm), 0).astype(F32)
    vals, onehots = [], []
    for k in range(TOP_K):
        mx = jnp.max(work, axis=0, keepdims=True)
        am = jnp.min(jnp.where(work == mx, row_f, float(ne)), axis=0, keepdims=True)
        oh = row_f == am
        vals.append(mx)
        onehots.append(oh)
        work = jnp.where(oh, -3e38, work)
    es = [jnp.exp(v - vals[0]) for v in vals]
    denom = es[0] + es[1] + es[2] + es[3]
    member = (onehots[0] | onehots[1] | onehots[2] | onehots[3])
    s_idx = lax.broadcasted_iota(jnp.int32, (tm, tm), 0)
    t_idx = lax.broadcasted_iota(jnp.int32, (tm, tm), 1)
    before = jnp.dot(member.astype(BF16), (s_idx < t_idx).astype(BF16), preferred_element_type=F32)
    cnt = jnp.sum(member.astype(F32), axis=1, keepdims=True)
    cnt = jnp.floor((cnt + (SEG_ALIGN - 1)) * (1.0 / SEG_ALIGN)) * SEG_ALIGN
    cnt_b = jnp.broadcast_to(cnt, (ne, LANES))
    ei = lax.broadcasted_iota(jnp.int32, (ne, ne), 0)
    ej = lax.broadcasted_iota(jnp.int32, (ne, ne), 1)
    seg_start = jnp.dot((ej < ei).astype(BF16), cnt_b.astype(BF16), preferred_element_type=F32)[:, 0:1]
    slot = seg_start + before
    sub8 = lax.broadcasted_iota(jnp.int32, (8, tm), 0)
    lp8 = jnp.zeros((8, tm), F32)
    g8 = jnp.zeros((8, tm), F32)
    for k in range(TOP_K):
        lp8 = jnp.where(sub8 == k, jnp.sum(jnp.where(onehots[k], slot, 0.0), axis=0, keepdims=True), lp8)
        g8 = jnp.where(sub8 == k, es[k] / denom, g8)
    lpost_ref[...] = lp8.astype(jnp.int32)
    pad = jnp.zeros((LANES - 8, tm), F32)
    lpos_ref[...] = jnp.transpose(jnp.concatenate([lp8, pad], axis=0)).astype(jnp.int32)
    gate_ref[...] = jnp.transpose(jnp.concatenate([g8, pad], axis=0))
    cnt_lane = jnp.transpose(jnp.concatenate([cnt_b, jnp.zeros((LANES - ne, LANES), F32)], axis=0))[0:1, :]
    row8 = lax.broadcasted_iota(jnp.int32, (8, LANES), 0)
    tile_ref[...] = jnp.where(row8 == 0, carry_ref[...], jnp.where(row8 == 1, cnt_lane, 0.0))
    carry_ref[...] = carry_ref[...] + cnt_lane
    cnt_ref[...] = carry_ref[...]


def _router(logits, tm):
    t = logits.shape[0]
    tile = pl.BlockSpec((tm, LANES), lambda i: (i, 0))
    return pl.pallas_call(
        _router_kernel,
        grid=(t // tm,),
        in_specs=[tile],
        out_specs=[tile, tile, pl.BlockSpec((8, tm), lambda i: (0, i)),
                   pl.BlockSpec((8, LANES), lambda i: (i, 0)), pl.BlockSpec((1, LANES), lambda i: (0, 0))],
        out_shape=[jax.ShapeDtypeStruct((t, LANES), F32),
                   jax.ShapeDtypeStruct((t, LANES), jnp.int32),
                   jax.ShapeDtypeStruct((8, t), jnp.int32),
                   jax.ShapeDtypeStruct((t // tm * 8, LANES), F32),
                   jax.ShapeDtypeStruct((1, LANES), F32)],
        scratch_shapes=[pltpu.VMEM((1, LANES), F32)],
        compiler_params=_cparams(("arbitrary",)),
        name="router_topk",
    )(logits)


HI_MASK = 0xFFFF0000


def _pack_rows(x, is_bf16_valued=False):
    half = x.shape[1] // 2
    if not is_bf16_valued:
        x = x.astype(BF16).astype(F32)
    bits = lax.bitcast_convert_type(x, jnp.uint32)
    return bits[:, :half] | (bits[:, half:] >> 16)


def _unpack_rows(p):
    left = lax.bitcast_convert_type(p & jnp.uint32(HI_MASK), F32).astype(BF16)
    right = lax.bitcast_convert_type(p << 16, F32).astype(BF16)
    return left, right


def _copy_pieces(n, local_row, global_row, copy, max_rows, wait=False):
    for b in range(SEG_ALIGN.bit_length() - 1, max_rows.bit_length()):
        size = 1 << b

        @pl.when((n & size) != 0)
        def _():
            lo = n & (size - 1)
            piece = copy(pl.multiple_of(local_row + lo, SEG_ALIGN), pl.multiple_of(global_row + lo, SEG_ALIGN),
                         size)
            if wait:
                piece.wait()
            else:
                piece.start()


def _segment_copies(base_ref, cnt_ref, copy, max_rows, n_slots, fill_row):
    def per_expert(e, off):
        _copy_pieces(cnt_ref[e], off, base_ref[e], copy, max_rows)
        return off + cnt_ref[e]
    used = lax.fori_loop(0, N_EXPERTS, per_expert, 0)
    _copy_pieces(n_slots - used, used, fill_row, copy, max_rows)


def _tile_slots(tm):
    return tm * TOP_K + N_EXPERTS * SEG_ALIGN


def _dispatch_kernel(base_ref, cnt_ref, tail_ref, tail_len_ref, lpost_ref, h_ref, xs_ref,
                     cbuf_ref, zbuf_ref, sem, zsem, *, spare_row):
    tm = h_ref.shape[0]

    @pl.when(pl.program_id(0) == pl.num_programs(0) - 1)
    def _():
        zbuf_ref[...] = jnp.zeros_like(zbuf_ref)

        def zero_copy(src, dst, size):
            return pltpu.make_async_copy(zbuf_ref.at[pl.ds(src, size)], xs_ref.at[pl.ds(dst, size)], zsem)
        for wait in (False, True):
            def per_expert(e, carry, wait=wait):
                _copy_pieces(tail_len_ref[e], 0, tail_ref[e], zero_copy, zbuf_ref.shape[0], wait=wait)
                return carry
            lax.fori_loop(0, N_EXPERTS, per_expert, 0)

    n_slots = _tile_slots(tm)
    slot = lax.broadcasted_iota(jnp.int32, (n_slots, tm), 0)
    lp = lpost_ref[...]
    sel = slot == lp[0:1, :]
    for k in range(1, TOP_K):
        sel = sel | (slot == lp[k:k + 1, :])
    rows = jnp.dot(sel.astype(BF16), h_ref[...].astype(BF16), preferred_element_type=F32)
    step = pl.program_id(0)
    cur = step % 2
    cbuf_ref[cur] = _pack_rows(rows, is_bf16_valued=True)

    def copy_from(buf):
        def copy(src, dst, size):
            return pltpu.make_async_copy(cbuf_ref.at[buf, pl.ds(src, size)], xs_ref.at[pl.ds(dst, size)],
                                         sem.at[buf])
        return copy
    _segment_copies(base_ref, cnt_ref, copy_from(cur), tm, n_slots, spare_row + cur * tm)

    @pl.when(step > 0)
    def _():
        copy_from(1 - cur)(0, 0, n_slots).wait()

    @pl.when(step == pl.num_programs(0) - 1)
    def _():
        copy_from(cur)(0, 0, n_slots).wait()


def _dispatch(h2, base, cnt, tail, tail_len, lpost, n_rows, tm, tb):
    t, d = h2.shape
    seg = pl.BlockSpec((LANES,), lambda i: (i,), memory_space=pltpu.SMEM)
    whole = pl.BlockSpec(memory_space=pltpu.SMEM)
    return pl.pallas_call(
        functools.partial(_dispatch_kernel, spare_row=n_rows),
        grid=(t // tm,),
        in_specs=[seg, seg, whole, whole,
                  pl.BlockSpec((8, tm), lambda i: (0, i)),
                  pl.BlockSpec((tm, d), lambda i: (i, 0))],
        out_specs=pl.BlockSpec(memory_space=pl.ANY),
        out_shape=jax.ShapeDtypeStruct((n_rows + 2 * tm, d // 2), jnp.uint32),
        scratch_shapes=[pltpu.VMEM((2, _tile_slots(tm), d // 2), jnp.uint32),
                        pltpu.VMEM((tb, d // 2), jnp.uint32),
                        pltpu.SemaphoreType.DMA((2,)), pltpu.SemaphoreType.DMA(())],
        compiler_params=_cparams(("arbitrary",)),
        name="moe_dispatch",
    )(base, cnt, tail, tail_len, lpost, h2)


def _ffn_kernel(be_ref, nu_ref, x_ref, wgu_ref, bgu_ref, wdn_ref, bdn_ref, y_ref, wgu_bf_ref, wdn_bf_ref):
    de = wdn_ref.shape[2]
    i = pl.program_id(0)
    used = i < nu_ref[0]

    @pl.when(used & ((i == 0) | (be_ref[i] != be_ref[jnp.maximum(i - 1, 0)])))
    def _():
        wgu_bf_ref[...] = wgu_ref[0, 0].astype(BF16)
        wdn_bf_ref[...] = wdn_ref[0, 0].astype(BF16)

    @pl.when(used)
    def _():
        half = x_ref.shape[1]
        x_left, x_right = _unpack_rows(x_ref[...])
        gu = (jnp.dot(x_left, wgu_bf_ref[:half, :], preferred_element_type=F32)
              + jnp.dot(x_right, wgu_bf_ref[half:, :], preferred_element_type=F32) + bgu_ref[0, 0])
        gate = jnp.minimum(gu[:, :de], SWIGLU_LIMIT)
        up = jnp.clip(gu[:, de:], -SWIGLU_LIMIT, SWIGLU_LIMIT)
        act = (up + 1.0) * gate * jax.nn.sigmoid(SWIGLU_ALPHA * gate)
        y = jnp.dot(act.astype(BF16), wdn_bf_ref[...], preferred_element_type=F32) + bdn_ref[0, 0]
        y_ref[...] = _pack_rows(y)

    @pl.when(jnp.logical_not(used))
    def _():
        y_ref[...] = jnp.zeros_like(y_ref)


def _ffn(xs, n_rows, block_expert, n_used, layer, wgu, bgu, wdn, bdn, tb):
    p, half = n_rows, xs.shape[1]
    d = 2 * half
    nl, e, _, n2 = wgu.shape
    de = wdn.shape[2]
    grid_spec = pltpu.PrefetchScalarGridSpec(
        num_scalar_prefetch=2,
        grid=(p // tb,),
        in_specs=[pl.BlockSpec((tb, half), lambda i, be, nu: (i, 0)),
                  pl.BlockSpec((1, 1, d, n2), lambda i, be, nu: (layer, be[i], 0, 0)),
                  pl.BlockSpec((1, 1, 1, n2), lambda i, be, nu: (layer, be[i], 0, 0)),
                  pl.BlockSpec((1, 1, de, d), lambda i, be, nu: (layer, be[i], 0, 0)),
                  pl.BlockSpec((1, 1, 1, d), lambda i, be, nu: (layer, be[i], 0, 0))],
        out_specs=pl.BlockSpec((tb, half), lambda i, be, nu: (i, 0)),
        scratch_shapes=[pltpu.VMEM((d, n2), BF16), pltpu.VMEM((de, d), BF16)],
    )
    return pl.pallas_call(
        _ffn_kernel,
        grid_spec=grid_spec,
        out_shape=jax.ShapeDtypeStruct((p, half), jnp.uint32),
        compiler_params=_cparams(("arbitrary",)),
        name="moe_ffn",
    )(block_expert, n_used, xs, wgu, bgu.reshape(nl, e, 1, n2), wdn, bdn.reshape(nl, e, 1, d))


def _combine_kernel(base_ref, cnt_ref, base_next_ref, cnt_next_ref, lpos_ref, gate_ref, x_ref, g2_ref,
                    lng_ref, lnb_ref, ys_ref, o_ref, gbuf_ref, sem, *, alpha):
    tm = x_ref.shape[1]
    n_slots = _tile_slots(tm)
    step = pl.program_id(0) * pl.num_programs(1) + pl.program_id(1)
    n_steps = pl.num_programs(0) * pl.num_programs(1)
    cur = step % 2

    def gather(bases, cnts, buf):
        def copy(dst, src, size):
            return pltpu.make_async_copy(ys_ref.at[pl.ds(src, size)], gbuf_ref.at[buf, pl.ds(dst, size)],
                                         sem.at[buf])
        _segment_copies(bases, cnts, copy, tm, n_slots, 0)
        return copy

    @pl.when(step == 0)
    def _():
        gather(base_ref, cnt_ref, cur)

    @pl.when(step + 1 < n_steps)
    def _():
        gather(base_next_ref, cnt_next_ref, 1 - cur)

    def copy(dst, src, size):
        return pltpu.make_async_copy(ys_ref.at[pl.ds(src, size)], gbuf_ref.at[cur, pl.ds(dst, size)],
                                     sem.at[cur])

    lane = lax.broadcasted_iota(jnp.int32, (tm, n_slots), 1)
    lpos = lpos_ref[...]
    gate = gate_ref[...]
    w = jnp.zeros((tm, n_slots), F32)
    for k in range(TOP_K):
        w = jnp.where(lane == lpos[:, k:k + 1], gate[:, k:k + 1], w)
    w = w.astype(BF16)

    copy(0, 0, n_slots).wait()
    y_left, y_right = _unpack_rows(gbuf_ref[cur])
    f = jnp.concatenate([jnp.dot(w, y, preferred_element_type=F32) for y in (y_left, y_right)], axis=1)
    o_ref[0] = _layer_norm(alpha * x_ref[0] + g2_ref[0] * f, lng_ref[...], lnb_ref[...])


def _combine(base, cnt, lpos, gates, x, g2, lng, lnb, ys, n_lat_tiles, n_tiles, tm, alpha):
    b, _, d = x.shape
    nb = g2.shape[0] - 1

    def mod_map(i, j):
        return (jnp.where(j < n_lat_tiles, i, nb), 0, 0)

    tok = pl.BlockSpec((1, tm, d), lambda i, j: (i, j, 0))
    vec = pl.BlockSpec((1, d), lambda i, j: (0, 0))
    last = b * n_tiles - 1
    seg = pl.BlockSpec((LANES,), lambda i, j: (i * n_tiles + j,), memory_space=pltpu.SMEM)
    seg_next = pl.BlockSpec((LANES,), lambda i, j: (jnp.minimum(i * n_tiles + j + 1, last),),
                            memory_space=pltpu.SMEM)
    per_tok = pl.BlockSpec((tm, LANES), lambda i, j: (i * n_tiles + j, 0))
    return pl.pallas_call(
        functools.partial(_combine_kernel, alpha=alpha),
        grid=(b, n_tiles),
        in_specs=[seg, seg, seg_next, seg_next, per_tok, per_tok,
                  tok, pl.BlockSpec((1, 1, d), mod_map), vec, vec,
                  pl.BlockSpec(memory_space=pl.ANY)],
        out_specs=tok,
        out_shape=jax.ShapeDtypeStruct((b, n_tiles * tm, d), F32),
        scratch_shapes=[pltpu.VMEM((2, _tile_slots(tm), d // 2), jnp.uint32), pltpu.SemaphoreType.DMA((2,))],
        compiler_params=_cparams(("arbitrary", "arbitrary")),
        name="moe_combine_ln",
    )(base, cnt, base, cnt, lpos, gates, x, g2, lng, lnb, ys)


def _moe_layer(h2, logits, x_res, g2, lng, lnb, layer, wgu, bgu, wdn, bdn, n_lat_tiles, n_tiles, tm, alpha):
    b, rows, d = h2.shape
    t = b * rows
    tb = FFN_BLOCK
    gates, lpos, lpost, tiles, counts = _router(logits.reshape(t, LANES), tm)
    counts = counts[0, :N_EXPERTS].astype(jnp.int32)
    padded = (counts + tb - 1) // tb * tb
    pend = jnp.cumsum(padded)
    pstart = pend - padded
    tiles = tiles.reshape(t // tm, 8, LANES)[:, :2, :].astype(jnp.int32)
    seg_base = (jnp.pad(pstart, (0, LANES - N_EXPERTS))[None, :] + tiles[:, 0]).reshape(-1)
    seg_cnt = tiles[:, 1].reshape(-1)
    n_rows = t * TOP_K + (t // tm) * N_EXPERTS * (SEG_ALIGN - 1)
    n_rows = (n_rows + tb - 1) // tb * tb + N_EXPERTS * tb
    n_blocks = n_rows // tb
    block_start = jnp.arange(n_blocks, dtype=jnp.int32) * tb
    block_expert = jnp.minimum(jnp.sum((pend[None, :] <= block_start[:, None]).astype(jnp.int32), axis=1),
                               N_EXPERTS - 1).astype(jnp.int32)
    n_used = (pend[-1:] // tb).astype(jnp.int32)
    tail = jnp.pad(pstart + counts, (0, LANES - N_EXPERTS))
    tail_len = jnp.pad(padded - counts, (0, LANES - N_EXPERTS))
    xs = _dispatch(h2.reshape(t, d), seg_base, seg_cnt, tail, tail_len, lpost, n_rows, tm, tb)
    ys = _ffn(xs, n_rows, block_expert, n_used, layer, wgu, bgu, wdn, bdn, tb)
    return _combine(seg_base, seg_cnt, lpos, gates, x_res, g2, lng, lnb, ys, n_lat_tiles, n_tiles, tm, alpha)


def _rope_tables(n_lat_rows, n_rows, scale):
    pos = jnp.arange(n_lat_rows)
    lane = jnp.arange(LANES)
    m = lane % 64
    n_freq = 16
    inv = ROPE_BASE ** (-(m % n_freq).astype(F32) / n_freq)
    p = jnp.where((m // 32)[None, :] == 0, (pos // GRID_W)[:, None], (pos % GRID_W)[:, None]).astype(F32)
    ang = p * inv[None, :]
    sign = jnp.where((m % 32) < n_freq, -1.0, 1.0)[None, :]
    cos = jnp.concatenate([jnp.cos(ang), jnp.ones((n_rows - n_lat_rows, LANES), F32)], axis=0)
    sin = jnp.concatenate([jnp.sin(ang) * sign, jnp.zeros((n_rows - n_lat_rows, LANES), F32)], axis=0)
    return cos * scale, sin * scale


def kernel(x, c, ctx, c_ctx, w_ada, b_ada, ln_g, ln_b, hg_w_in, hg_lb, hg_norm_g, hg_w_out, da_w_in, da_lam,
           da_norm_g, da_w_out, moe_w_router, moe_b_router, moe_w_gu, moe_b_gu, moe_w_dn, moe_b_dn):
    bsz, seq, d = x.shape
    lc = ctx.shape[1]
    depth = w_ada.shape[0]
    assert depth == 2 and hg_w_in.shape[0] == 1 and da_w_in.shape[0] == 1
    ltot = seq + lc
    tm = TOKEN_TILE
    assert seq % tm == 0 and lc % tm == 0 and seq % GRID_W == 0 and d == DA_HEADS * LANES
    alpha = (2 * depth) ** 0.25
    n_lat_tiles = seq // tm
    n_all_tiles = ltot // tm
    h_heads = d // LANES

    n_mod = bsz + 1
    c_pad = jnp.zeros((16, d), F32).at[:bsz].set(c).at[bsz].set(c_ctx)
    mod = _ada(c_pad, w_ada, b_ada)[:, :n_mod]

    def mods(l):
        return [mod[l, :, k * d:(k + 1) * d].reshape(n_mod, 1, d) for k in range(6)]

    def router_params(l):
        wr = jnp.zeros((d, LANES), F32).at[:, :N_EXPERTS].set(moe_w_router[l])
        wr_hi = wr.astype(BF16)
        wr_lo = (wr - wr_hi.astype(F32)).astype(BF16)
        br = jnp.full((1, LANES), -1e30, F32).at[0, :N_EXPERTS].set(moe_b_router[l])
        return jnp.stack([wr_hi, wr_lo]), br

    xall = (x, ctx)

    sh1, sc1, g1, sh2, sc2, g2 = mods(0)
    p0 = _proj(xall, sc1, sh1, hg_w_in[0].astype(BF16), n_lat_tiles, tm)
    lb = jnp.cumsum(jax.nn.softmax(hg_lb.astype(F32), axis=1), axis=1)[:, 0]
    lb = lb.reshape(2, h_heads, LANES).transpose(1, 0, 2)
    y0 = _gla(p0, lb, hg_norm_g[0].reshape(h_heads, 1, LANES), seq)
    wr, br = router_params(0)
    x0, h20, lg0 = _post(y0, hg_w_out[0].astype(BF16), xall, g1, ln_g[0, 0].reshape(1, d),
                         ln_b[0, 0].reshape(1, d), sc2, sh2, wr, br, n_lat_tiles, n_all_tiles, tm, alpha)
    x1 = _moe_layer(h20, lg0, x0, g2, ln_g[0, 1].reshape(1, d), ln_b[0, 1].reshape(1, d),
                    0, moe_w_gu, moe_b_gu, moe_w_dn, moe_b_dn, n_lat_tiles, n_all_tiles, tm, alpha)

    sh1, sc1, g1, sh2, sc2, g2 = mods(1)
    dh = d // DA_HEADS // 2
    cq, sq = _rope_tables(seq, ltot, dh ** -0.5 * math.log2(math.e))
    ck, sk = _rope_tables(seq, ltot, 1.0)
    p1 = _proj(x1, sc1, sh1, da_w_in[0].astype(BF16), n_lat_tiles, tm, rope_tabs=(cq, sq, ck, sk))
    lam_init = 0.8 - 0.6 * math.exp(-0.3 * 1)
    lp = da_lam[0].astype(F32)
    lam = (jnp.exp(jnp.sum(lp[0] * lp[1])) - jnp.exp(jnp.sum(lp[2] * lp[3])) + lam_init).reshape(1)
    y1 = _attn(p1, lam, da_norm_g[0].reshape(LANES, 1), seq, 1.0 - lam_init)
    wr, br = router_params(1)
    x2, h21, lg1 = _post(y1, da_w_out[0].astype(BF16), x1, g1, ln_g[1, 0].reshape(1, d),
                         ln_b[1, 0].reshape(1, d), sc2, sh2, wr, br, n_lat_tiles, n_lat_tiles, tm, alpha)
    return _moe_layer(h21, lg1, x2, g2, ln_g[1, 1].reshape(1, d), ln_b[1, 1].reshape(1, d),
                      1, moe_w_gu, moe_b_gu, moe_w_dn, moe_b_dn, n_lat_tiles, n_lat_tiles, tm, alpha)
```
